```python
import jax, jax.numpy as jnp
from jax import lax
import numpy as np

D_MODEL = 1024
BATCH = 4
SEQ = 8192
DEPTH = 2
DEC_BATCH = 16
DEC_SEQ = 32
PAST_LEN = 4096

CHUNK = 64
N_MIX = 2
N_HEADS = 16
HEAD_DIM = D_MODEL // N_HEADS
N_PAST_CHUNKS = 8
BAND_PAST = N_PAST_CHUNKS * CHUNK
BAND = BAND_PAST + CHUNK
REL_CLIP = 128
N_REL = 2 * REL_CLIP + 1
SB_BLOCK = 128
D_FF = D_MODEL * 7 // 2
N_EXPERTS = 8
TOP_K = 2
N_A_LAYERS = (DEPTH + 1) // 2
N_B_LAYERS = DEPTH // 2
EPS = 1e-6
NEG_INF = -1e30

kernel_name = "hybrid_chunkband_stickbreaking_stream_step"


def rms_norm(x, g):
    xf = x.astype(jnp.float32)
    y = xf * lax.rsqrt(jnp.mean(xf * xf, axis=-1, keepdims=True) + EPS)
    return (y * g.astype(jnp.float32)).astype(x.dtype)


def ada_modulation(c, w, b):
    m = jax.nn.silu(c) @ w + b
    return jnp.split(m, 6, axis=-1)


def modulated_norm(x, g, shift, scale):
    h = rms_norm(x, g)
    return h * (1.0 + scale[:, None, :]) + shift[:, None, :]


def split_heads(qkv):
    b, s, _ = qkv.shape
    q, k, v = jnp.split(qkv, 3, axis=-1)
    shp = (b, s, N_HEADS, HEAD_DIM)
    return q.reshape(shp), k.reshape(shp), v.reshape(shp)


def band_attention(q, k, v, q_pos, k_pos, rel_bias):
    s = jnp.einsum('bqhd,bkhd->bhqk', q, k).astype(jnp.float32) * (HEAD_DIM ** -0.5)
    rel = jnp.clip(q_pos[:, None] - k_pos[None, :], -REL_CLIP, REL_CLIP) + REL_CLIP
    s = s + rel_bias.astype(jnp.float32)[:, rel]
    qc = q_pos // CHUNK
    kc = k_pos // CHUNK
    valid = ((k_pos[None, :] >= 0) & (kc[None, :] <= qc[:, None])
             & (kc[None, :] >= qc[:, None] - N_PAST_CHUNKS))
    s = jnp.where(valid, s, NEG_INF)
    p = jax.nn.softmax(s, axis=-1)
    return jnp.einsum('bhqk,bkhd->bqhd', p.astype(v.dtype), v)


def band_attention_prompt(q, k, v, rel_bias):
    b, s = q.shape[0], q.shape[1]
    n_chunks = s // CHUNK
    pad = ((0, 0), (BAND_PAST, 0), (0, 0), (0, 0))
    kp = jnp.pad(k, pad)
    vp = jnp.pad(v, pad)

    def one_chunk(c):
        start = c * CHUNK
        qc = lax.dynamic_slice_in_dim(q, start, CHUNK, axis=1)
        kb = lax.dynamic_slice_in_dim(kp, start, BAND, axis=1)
        vb = lax.dynamic_slice_in_dim(vp, start, BAND, axis=1)
        q_pos = start + jnp.arange(CHUNK, dtype=jnp.int32)
        k_pos = start - BAND_PAST + jnp.arange(BAND, dtype=jnp.int32)
        return band_attention(qc, kb, vb, q_pos, k_pos, rel_bias)

    out = lax.map(one_chunk, jnp.arange(n_chunks, dtype=jnp.int32))
    return jnp.moveaxis(out, 0, 1).reshape(b, s, N_HEADS, HEAD_DIM)


def stick_breaking(q, k, v, q_pos, k_pos):
    z = jnp.einsum('bqhd,bkhd->bhqk', q, k).astype(jnp.float32) * (HEAD_DIM ** -0.5)
    causal = k_pos[None, :] < q_pos[:, None]
    log_keep = jnp.where(causal, jax.nn.log_sigmoid(-z), 0.0)
    after = lax.cumsum(log_keep, axis=3, reverse=True) - log_keep
    a = jnp.where(causal, jnp.exp(jax.nn.log_sigmoid(z) + after), 0.0)
    return jnp.einsum('bhqk,bkhd->bqhd', a.astype(v.dtype), v)


def stick_breaking_blocks(q, k, v, q_pos, k_pos):
    b, nq = q.shape[0], q.shape[1]
    qb = SB_BLOCK if nq % SB_BLOCK == 0 else nq
    nb = nq // qb
    q_blocks = jnp.moveaxis(q.reshape(b, nb, qb, N_HEADS, HEAD_DIM), 1, 0)
    pos_blocks = q_pos.reshape(nb, qb)
    out = lax.map(lambda a: stick_breaking(a[0], k, v, a[1], k_pos), (q_blocks, pos_blocks))
    return jnp.moveaxis(out, 0, 1).reshape(b, nq, N_HEADS, HEAD_DIM)


def swiglu(h, w_gate, w_up, w_down):
    return (jax.nn.silu(h @ w_gate) * (h @ w_up)) @ w_down


def moe_swiglu(h, w_router, w_gate, w_up, w_down):
    logits = (h @ w_router).astype(jnp.float32)
    top_val, top_idx = lax.top_k(logits, TOP_K)
    top_w = jax.nn.softmax(top_val, axis=-1)
    gates = jnp.sum(jax.nn.one_hot(top_idx, N_EXPERTS, dtype=jnp.float32) * top_w[..., None], axis=-2)
    out = jnp.zeros_like(h)
    for e in range(N_EXPERTS):
        out = out + gates[..., e:e + 1].astype(h.dtype) * swiglu(h, w_gate[e], w_up[e], w_down[e])
    return out


def setup_inputs(seed: int = 0) -> dict:
    key = jax.random.key(seed)
    ks = jax.random.split(key, 24)

    def nrm(k, shape, scale):
        return jax.random.normal(k, shape, jnp.float32) * scale

    win = min(BAND_PAST, PAST_LEN)
    d, f = D_MODEL, D_FF
    return {
        "x_prompt": nrm(ks[0], (BATCH, SEQ, d), 1.0),
        "x_sample": nrm(ks[1], (DEC_BATCH, DEC_SEQ, d), 1.0),
        "cache_a_k": nrm(ks[2], (N_A_LAYERS, DEC_BATCH, win, N_HEADS, HEAD_DIM), 1.0),
        "cache_a_v": nrm(ks[3], (N_A_LAYERS, DEC_BATCH, win, N_HEADS, HEAD_DIM), 1.0),
        "cache_b_k": nrm(ks[4], (N_B_LAYERS, DEC_BATCH, PAST_LEN, N_HEADS, HEAD_DIM), 1.0),
        "cache_b_v": nrm(ks[5], (N_B_LAYERS, DEC_BATCH, PAST_LEN, N_HEADS, HEAD_DIM), 1.0),
        "c_prompt": nrm(ks[6], (BATCH, d), 1.0),
        "c_sample": nrm(ks[7], (DEC_BATCH, d), 1.0),
        "w_qkv": nrm(ks[8], (DEPTH, d, 3 * d), d ** -0.5),
        "w_o": nrm(ks[9], (DEPTH, d, d), d ** -0.5),
        "norm1_g": 1.0 + nrm(ks[10], (DEPTH, d), 0.05),
        "norm2_g": 1.0 + nrm(ks[11], (DEPTH, d), 0.05),
        "w_ada": nrm(ks[12], (DEPTH, d, 6 * d), 0.5 * d ** -0.5),
        "b_ada": nrm(ks[13], (DEPTH, 6 * d), 0.02),
        "q_norm_g": 1.0 + nrm(ks[14], (N_A_LAYERS, HEAD_DIM), 0.05),
        "k_norm_g": 1.0 + nrm(ks[15], (N_A_LAYERS, HEAD_DIM), 0.05),
        "rel_bias": nrm(ks[16], (N_A_LAYERS, N_HEADS, N_REL), 0.1),
        "w_gate_d": nrm(ks[17], (N_A_LAYERS, d, f), d ** -0.5),
        "w_up_d": nrm(ks[18], (N_A_LAYERS, d, f), d ** -0.5),
        "w_down_d": nrm(ks[19], (N_A_LAYERS, f, d), f ** -0.5),
        "w_router": nrm(ks[20], (N_B_LAYERS, d, N_EXPERTS), d ** -0.5),
        "w_gate_e": nrm(ks[21], (N_B_LAYERS, N_EXPERTS, d, f), d ** -0.5),
        "w_up_e": nrm(ks[22], (N_B_LAYERS, N_EXPERTS, d, f), d ** -0.5),
        "w_down_e": nrm(ks[23], (N_B_LAYERS, N_EXPERTS, f, d), f ** -0.5),
    }


def reference(x_prompt, x_sample, cache_a_k, cache_a_v, cache_b_k, cache_b_v, c_prompt, c_sample,
              w_qkv, w_o, norm1_g, norm2_g, w_ada, b_ada, q_norm_g, k_norm_g, rel_bias,
              w_gate_d, w_up_d, w_down_d, w_router, w_gate_e, w_up_e, w_down_e):
    bp, sp = x_prompt.shape[0], x_prompt.shape[1]
    bs, ts = x_sample.shape[0], x_sample.shape[1]
    past_len = cache_b_k.shape[2]
    win = cache_a_k.shape[2]
    keep_p = min(BAND_PAST, sp)
    pos_p = jnp.arange(sp, dtype=jnp.int32)
    pos_s = past_len + jnp.arange(ts, dtype=jnp.int32)
    pos_win = past_len - win + jnp.arange(win, dtype=jnp.int32)
    pos_all_b = jnp.arange(past_len + ts, dtype=jnp.int32)

    a_k_p, a_v_p, a_k_s, a_v_s = [], [], [], []
    b_k_p, b_v_p, b_k_s, b_v_s = [], [], [], []
    xp, xs = x_prompt, x_sample
    for i in range(DEPTH):
        j = i // N_MIX
        sh1p, sc1p, g1p, sh2p, sc2p, g2p = ada_modulation(c_prompt, w_ada[i], b_ada[i])
        sh1s, sc1s, g1s, sh2s, sc2s, g2s = ada_modulation(c_sample, w_ada[i], b_ada[i])

        qp, kp, vp = split_heads(modulated_norm(xp, norm1_g[i], sh1p, sc1p) @ w_qkv[i])
        qs, ks, vs = split_heads(modulated_norm(xs, norm1_g[i], sh1s, sc1s) @ w_qkv[i])
        if i % N_MIX == 0:
            qp, kp = rms_norm(qp, q_norm_g[j]), rms_norm(kp, k_norm_g[j])
            qs, ks = rms_norm(qs, q_norm_g[j]), rms_norm(ks, k_norm_g[j])
            op = band_attention_prompt(qp, kp, vp, rel_bias[j])
            k_all = jnp.concatenate([cache_a_k[j], ks], axis=1)
            v_all = jnp.concatenate([cache_a_v[j], vs], axis=1)
            os_ = band_attention(qs, k_all, v_all, pos_s, jnp.concatenate([pos_win, pos_s]), rel_bias[j])
            a_k_p.append(kp[:, sp - keep_p:])
            a_v_p.append(vp[:, sp - keep_p:])
            a_k_s.append(k_all[:, ts:])
            a_v_s.append(v_all[:, ts:])
        else:
            op = stick_breaking_blocks(qp, kp, vp, pos_p, pos_p)
            k_all = jnp.concatenate([cache_b_k[j], ks], axis=1)
            v_all = jnp.concatenate([cache_b_v[j], vs], axis=1)
            os_ = stick_breaking_blocks(qs, k_all, v_all, pos_s, pos_all_b)
            b_k_p.append(kp)
            b_v_p.append(vp)
            b_k_s.append(ks)
            b_v_s.append(vs)
        xp = xp + g1p[:, None, :] * (op.reshape(bp, sp, D_MODEL) @ w_o[i])
        xs = xs + g1s[:, None, :] * (os_.reshape(bs, ts, D_MODEL) @ w_o[i])

        hp = modulated_norm(xp, norm2_g[i], sh2p, sc2p)
        hs = modulated_norm(xs, norm2_g[i], sh2s, sc2s)
        if i % 2 == 0:
            fp = swiglu(hp, w_gate_d[j], w_up_d[j], w_down_d[j])
            fs = swiglu(hs, w_gate_d[j], w_up_d[j], w_down_d[j])
        else:
            fp = moe_swiglu(hp, w_router[j], w_gate_e[j], w_up_e[j], w_down_e[j])
            fs = moe_swiglu(hs, w_router[j], w_gate_e[j], w_up_e[j], w_down_e[j])
        xp = xp + g2p[:, None, :] * fp
        xs = xs + g2s[:, None, :] * fs

    return (xp, xs,
            jnp.stack(a_k_p), jnp.stack(a_v_p), jnp.stack(a_k_s), jnp.stack(a_v_s),
            jnp.stack(b_k_p), jnp.stack(b_v_p), jnp.stack(b_k_s), jnp.stack(b_v_s))
```

```python
import functools
import math

import jax
import jax.numpy as jnp
from jax import lax
from jax.experimental import pallas as pl
from jax.experimental.pallas import tpu as pltpu

F32 = jnp.float32
BF16 = jnp.bfloat16

CHUNK = 64
N_HEADS = 16
HEAD_DIM = 64
N_PAST_CHUNKS = 8
BAND_PAST = N_PAST_CHUNKS * CHUNK
REL_CLIP = 128
N_EXPERTS = 8
EPS = 1e-6
NEG_INF = -1e30

LANES = 128
VMEM_LIMIT = 56 * 1024 * 1024

GROUP = 32
TM = 512
FC = 512
TQ = 128
TK = 128
NP = N_HEADS // 2
SB_PG = 2
SB_CB = 512
SB_STOP = -110.0
TR = 512
TMC = 256


def _cparams(sem):
    return pltpu.CompilerParams(dimension_semantics=sem, vmem_limit_bytes=VMEM_LIMIT)


def _dot(a, b):
    return jnp.dot(a, b, preferred_element_type=F32)


def _dot_nt(a, b):
    return lax.dot_general(a, b, (((1,), (1,)), ((), ())), preferred_element_type=F32)


def _split_hi_lo(x):
    hi = x.astype(BF16)
    lo = (x - hi.astype(F32)).astype(BF16)
    return hi, lo


def _silu(x):
    return x * (1.0 / (1.0 + jnp.exp(-x)))


def _lane_lt64(shape):
    return lax.broadcasted_iota(jnp.int32, shape, len(shape) - 1) < HEAD_DIM


def _group_rows(vec_rows, rows):
    g, d = vec_rows.shape
    return jnp.broadcast_to(vec_rows[:, None, :], (g, rows // g, d)).reshape(rows, d)


def _mod_norm(x, g, shift_g, scale_g):
    rows = x.shape[0]
    ms = jnp.mean(x * x, axis=-1, keepdims=True)
    y = x * lax.rsqrt(ms + EPS) * g
    return y * (1.0 + _group_rows(scale_g, rows)) + _group_rows(shift_g, rows)


def _ada_kernel(c_ref, w_ref, b_ref, o_ref):
    s = _silu(c_ref[...]).astype(BF16)
    o_ref[0] = _dot(s, w_ref[0].astype(BF16)) + b_ref[0]


def _ada(c_all, w_ada, b_ada):
    depth, d, n = w_ada.shape
    nb = c_all.shape[0]
    tn = 1536
    return pl.pallas_call(
        _ada_kernel,
        grid=(depth, n // tn),
        in_specs=[
            pl.BlockSpec((nb, d), lambda l, j: (0, 0)),
            pl.BlockSpec((1, d, tn), lambda l, j: (l, 0, j)),
            pl.BlockSpec((1, 1, tn), lambda l, j: (l, 0, j)),
        ],
        out_specs=pl.BlockSpec((1, nb, tn), lambda l, j: (l, 0, j)),
        out_shape=jax.ShapeDtypeStruct((depth, nb, n), F32),
        compiler_params=_cparams(("arbitrary", "arbitrary")),
        name="ada_modulation",
    )(c_all, w_ada, b_ada.reshape(depth, 1, n))


def _head_mean_sq(x, bd_ref):
    sq = x * x
    hi, lo = _split_hi_lo(sq)
    w = bd_ref.shape[0]
    cols = []
    for c in range(x.shape[1] // w):
        sl = slice(c * w, (c + 1) * w)
        cols.append(_dot(hi[:, sl], bd_ref[...]) + _dot(lo[:, sl], bd_ref[...]))
    return jnp.concatenate(cols, axis=1)


def _qkv_kernel(x_ref, g_ref, sh_ref, sc_ref, w_ref, qg_ref, kg_ref, bd_ref,
                pm_ref, kf_ref, vf_ref, *, qk_norm):
    d = x_ref.shape[1]
    h = _mod_norm(x_ref[...], g_ref[...], sh_ref[...], sc_ref[...]).astype(BF16)
    res = _dot(h, w_ref[...])
    q, k, v = res[:, :d], res[:, d:2 * d], res[:, 2 * d:]
    if qk_norm:
        q = q * lax.rsqrt(_head_mean_sq(q, bd_ref) + EPS) * qg_ref[...]
        k = k * lax.rsqrt(_head_mean_sq(k, bd_ref) + EPS) * kg_ref[...]
    kf_ref[...] = k
    vf_ref[...] = v
    qs = (q * (HEAD_DIM ** -0.5)).astype(BF16)
    kb = k.astype(BF16)
    vb = v.astype(BF16)
    for p in range(NP):
        sl = slice(p * LANES, (p + 1) * LANES)
        pm_ref[p] = qs[:, sl]
        pm_ref[NP + p] = kb[:, sl]
        pm_ref[2 * NP + p] = vb[:, sl]


def _qkv(x_all, g, shift_g, scale_g, w_bf, qg_row, kg_row, bd, qk_norm):
    t, d = x_all.shape
    gpt = TM // GROUP
    row = lambda i: (i, 0)
    const = lambda i: (0, 0)
    return pl.pallas_call(
        functools.partial(_qkv_kernel, qk_norm=qk_norm),
        grid=(t // TM,),
        in_specs=[
            pl.BlockSpec((TM, d), row),
            pl.BlockSpec((1, d), const),
            pl.BlockSpec((gpt, d), row),
            pl.BlockSpec((gpt, d), row),
            pl.BlockSpec((d, 3 * d), const),
            pl.BlockSpec((1, d), const),
            pl.BlockSpec((1, d), const),
            pl.BlockSpec(bd.shape, const),
        ],
        out_specs=[
            pl.BlockSpec((3 * NP, TM, LANES), lambda i: (0, i, 0)),
            pl.BlockSpec((TM, d), row),
            pl.BlockSpec((TM, d), row),
        ],
        out_shape=[
            jax.ShapeDtypeStruct((3 * NP, t, LANES), BF16),
            jax.ShapeDtypeStruct((t, d), F32),
            jax.ShapeDtypeStruct((t, d), F32),
        ],
        compiler_params=_cparams(("arbitrary",)),
        name="qkv_qknorm" if qk_norm else "qkv",
    )(x_all, g, shift_g, scale_g, w_bf, qg_row, kg_row, bd)


N_BAND_BLK = BAND_PAST // TK + 1


def _stack_heads(q2):
    m = _lane_lt64(q2.shape)
    z = jnp.zeros_like(q2)
    return jnp.concatenate([jnp.where(m, q2, z), jnp.where(m, z, q2)], axis=0)


def _band_prompt_kernel(q_ref, k_ref, v_ref, bias_ref, o_ref):
    s_len = q_ref.shape[1]

    def tile(i, carry):
        q2 = q_ref[0, pl.ds(pl.multiple_of(i * TQ, TQ), TQ), :]
        qm = _stack_heads(q2)
        scores = []
        vals = []
        for dd in range(N_BAND_BLK):
            blk = i + dd - (N_BAND_BLK - 1)
            start = pl.multiple_of(jnp.maximum(blk, 0) * TK, TK)
            k2 = k_ref[0, pl.ds(start, TK), :]
            vals.append(v_ref[0, pl.ds(start, TK), :])
            s = _dot_nt(qm, k2) + bias_ref[:, dd].reshape(2 * TQ, TK)
            scores.append(jnp.where(blk >= 0, s, NEG_INF))
        m = scores[0].max(axis=1, keepdims=True)
        for s in scores[1:]:
            m = jnp.maximum(m, s.max(axis=1, keepdims=True))
        l = jnp.zeros((2 * TQ, 1), F32)
        acc = jnp.zeros((TQ, LANES), F32)
        for s, v2 in zip(scores, vals):
            p = jnp.exp(s - m)
            l = l + p.sum(axis=1, keepdims=True)
            pb = p.astype(BF16)
            vm = _lane_lt64(v2.shape)
            zv = jnp.zeros_like(v2)
            acc = acc + _dot(pb[:TQ], jnp.where(vm, v2, zv)) + _dot(pb[TQ:], jnp.where(vm, zv, v2))
        linv = jnp.where(_lane_lt64((TQ, LANES)), 1.0 / l[:TQ], 1.0 / l[TQ:])
        o_ref[0, pl.ds(pl.multiple_of(i * TQ, TQ), TQ), :] = (acc * linv).astype(BF16)
        return carry

    lax.fori_loop(0, s_len // TQ, tile, 0)


def _band_prompt(pm, bias_p, n_batch, s_len):
    t = pm.shape[1]
    return pl.pallas_call(
        _band_prompt_kernel,
        grid=(n_batch, NP),
        in_specs=[
            pl.BlockSpec((1, s_len, LANES), lambda b, p: (p, b, 0)),
            pl.BlockSpec((1, s_len, LANES), lambda b, p: (NP + p, b, 0)),
            pl.BlockSpec((1, s_len, LANES), lambda b, p: (2 * NP + p, b, 0)),
            pl.BlockSpec((2, N_BAND_BLK, TQ, TK), lambda b, p: (p, 0, 0, 0)),
        ],
        out_specs=pl.BlockSpec((1, s_len, LANES), lambda b, p: (p, b, 0)),
        out_shape=jax.ShapeDtypeStruct((NP, t, LANES), BF16),
        compiler_params=_cparams(("arbitrary", "arbitrary")),
        name="band_attention_prompt",
    )(pm, pm, pm, bias_p)


def _band_sample_kernel(q_ref, kn_ref, vn_ref, kc_ref, vc_ref, bc_ref, bn_ref, o_prev_ref, o_ref):
    del o_prev_ref
    ts = q_ref.shape[1]
    for p in range(NP):
        sl = slice(p * LANES, (p + 1) * LANES)
        qm = _stack_heads(q_ref[p])
        kc = kc_ref[0, :, sl].astype(BF16)
        vc = vc_ref[0, :, sl].astype(BF16)
        kn = kn_ref[p]
        vn = vn_ref[p]
        s1 = _dot_nt(qm, kc) + bc_ref[2 * p:2 * p + 2].reshape(2 * ts, kc.shape[0])
        s2 = _dot_nt(qm, kn) + bn_ref[2 * p:2 * p + 2].reshape(2 * ts, ts)
        m = jnp.maximum(s1.max(axis=1, keepdims=True), s2.max(axis=1, keepdims=True))
        p1 = jnp.exp(s1 - m)
        p2 = jnp.exp(s2 - m)
        l = p1.sum(axis=1, keepdims=True) + p2.sum(axis=1, keepdims=True)
        p1 = p1.astype(BF16)
        p2 = p2.astype(BF16)
        mc = _lane_lt64(vc.shape)
        mn = _lane_lt64(vn.shape)
        zc = jnp.zeros_like(vc)
        zn = jnp.zeros_like(vn)
        acc = (_dot(p1[:ts], jnp.where(mc, vc, zc)) + _dot(p1[ts:], jnp.where(mc, zc, vc))
               + _dot(p2[:ts], jnp.where(mn, vn, zn)) + _dot(p2[ts:], jnp.where(mn, zn, vn)))
        linv = jnp.where(_lane_lt64((ts, LANES)), 1.0 / l[:ts], 1.0 / l[ts:])
        o_ref[p] = (acc * linv).astype(BF16)


def _band_sample(pm, cache_k, cache_v, bias_c, bias_n, o_pm, tp, n_streams, ts):
    win = cache_k.shape[1]
    d = cache_k.shape[2]
    rb = tp // ts
    return pl.pallas_call(
        _band_sample_kernel,
        grid=(n_streams,),
        in_specs=[
            pl.BlockSpec((NP, ts, LANES), lambda s: (0, rb + s, 0)),
            pl.BlockSpec((NP, ts, LANES), lambda s: (1, rb + s, 0)),
            pl.BlockSpec((NP, ts, LANES), lambda s: (2, rb + s, 0)),
            pl.BlockSpec((1, win, d), lambda s: (s, 0, 0)),
            pl.BlockSpec((1, win, d), lambda s: (s, 0, 0)),
            pl.BlockSpec(bias_c.shape, lambda s: (0, 0, 0)),
            pl.BlockSpec(bias_n.shape, lambda s: (0, 0, 0)),
            pl.BlockSpec(memory_space=pl.ANY),
        ],
        out_specs=pl.BlockSpec((NP, ts, LANES), lambda s: (0, rb + s, 0)),
        out_shape=jax.ShapeDtypeStruct(o_pm.shape, o_pm.dtype),
        input_output_aliases={7: 0},
        compiler_params=_cparams(("arbitrary",)),
        name="band_attention_sample",
    )(pm, pm, pm, cache_k, cache_v, bias_c, bias_n, o_pm)


def _sb_block(qm, k2, v2, upper, carry, acc, mask):
    r = acc.shape[0]
    z = _dot_nt(qm, k2)
    sp = jnp.maximum(z, 0.0) + jnp.log1p(jnp.exp(-jnp.abs(z)))
    lk = -sp
    if mask is not None:
        lk = jnp.where(mask, lk, 0.0)
    hi, lo = _split_hi_lo(lk)
    after = _dot(hi, upper) + _dot(lo, upper) + carry
    a = jnp.exp(z - sp + after)
    if mask is not None:
        a = jnp.where(mask, a, 0.0)
    ab = a.astype(BF16)
    vm = _lane_lt64(v2.shape)
    zv = jnp.zeros_like(v2)
    acc = acc + _dot(ab[:r], jnp.where(vm, v2, zv)) + _dot(ab[r:], jnp.where(vm, zv, v2))
    carry = carry + lk.sum(axis=1, keepdims=True)
    return carry, acc


def _upper_tri(n):
    return (lax.broadcasted_iota(jnp.int32, (n, n), 0)
            > lax.broadcasted_iota(jnp.int32, (n, n), 1)).astype(BF16)


def _causal_mask(rows, cols):
    r = lax.broadcasted_iota(jnp.int32, (2 * rows, cols), 0)
    r = jnp.where(r >= rows, r - rows, r)
    c = lax.broadcasted_iota(jnp.int32, (2 * rows, cols), 1)
    return c < r


def _sb_prompt_kernel(q_ref, k_ref, v_ref, o_ref):
    npg, s_len = q_ref.shape[0], q_ref.shape[1]
    upper = _upper_tri(TK)
    diag_mask = _causal_mask(TQ, TK)

    def tile(i, c0):
        row0 = pl.multiple_of(i * TQ, TQ)
        qms = [_stack_heads(q_ref[g, pl.ds(row0, TQ), :]) for g in range(npg)]
        carries, accs = [], []
        for g in range(npg):
            c, a = _sb_block(qms[g], k_ref[g, pl.ds(row0, TK), :], v_ref[g, pl.ds(row0, TK), :],
                             upper, jnp.zeros((2 * TQ, 1), F32), jnp.zeros((TQ, LANES), F32),
                             diag_mask)
            carries.append(c)
            accs.append(a)

        def live(cs):
            m = cs[0].max()
            for c in cs[1:]:
                m = jnp.maximum(m, c.max())
            return (m >= SB_STOP).astype(jnp.int32)

        def cond(st):
            return jnp.logical_and(st[0] >= 0, st[1] > 0)

        def body(st):
            j = st[0]
            cs, as_ = list(st[2]), list(st[3])
            start = pl.multiple_of(j * TK, TK)
            for g in range(npg):
                cs[g], as_[g] = _sb_block(qms[g], k_ref[g, pl.ds(start, TK), :],
                                          v_ref[g, pl.ds(start, TK), :], upper, cs[g], as_[g], None)
            return (j - 1, live(cs), tuple(cs), tuple(as_))

        st = lax.while_loop(cond, body, (i - 1, live(carries), tuple(carries), tuple(accs)))
        for g in range(npg):
            o_ref[g, pl.ds(row0, TQ), :] = st[3][g].astype(BF16)
        return c0

    lax.fori_loop(0, s_len // TQ, tile, 0)


def _sb_prompt(pm, n_batch, s_len):
    t = pm.shape[1]
    ng = NP // SB_PG
    return pl.pallas_call(
        _sb_prompt_kernel,
        grid=(n_batch, ng),
        in_specs=[
            pl.BlockSpec((SB_PG, s_len, LANES), lambda b, g: (g, b, 0)),
            pl.BlockSpec((SB_PG, s_len, LANES), lambda b, g: (ng + g, b, 0)),
            pl.BlockSpec((SB_PG, s_len, LANES), lambda b, g: (2 * ng + g, b, 0)),
        ],
        out_specs=pl.BlockSpec((SB_PG, s_len, LANES), lambda b, g: (g, b, 0)),
        out_shape=jax.ShapeDtypeStruct((NP, t, LANES), BF16),
        compiler_params=_cparams(("arbitrary", "arbitrary")),
        name="stick_breaking_prompt",
    )(pm, pm, pm)


def _sb_sample_kernel(q_ref, kn_ref, vn_ref, kc_ref, vc_ref, o_prev_ref, o_ref, carry_sc, acc_sc):
    del o_prev_ref
    ts = q_ref.shape[1]
    c_idx = pl.program_id(1)
    upper_n = _upper_tri(ts)
    upper = _upper_tri(TK)

    @pl.when(c_idx == 0)
    def _():
        mask = _causal_mask(ts, ts)
        for p in range(NP):
            c, a = _sb_block(_stack_heads(q_ref[p]), kn_ref[p], vn_ref[p], upper_n,
                             jnp.zeros((2 * ts, 1), F32), jnp.zeros((ts, LANES), F32), mask)
            carry_sc[p] = jnp.broadcast_to(c, (2 * ts, LANES))
            acc_sc[p] = a

    @pl.when(carry_sc[...].max() >= SB_STOP)
    def _():
        for p in range(NP):
            sl = slice(p * LANES, (p + 1) * LANES)
            qm = _stack_heads(q_ref[p])
            c = carry_sc[p][:, :1]
            a = acc_sc[p]
            for sub in reversed(range(SB_CB // TK)):
                rows = slice(sub * TK, (sub + 1) * TK)
                c, a = _sb_block(qm, kc_ref[0, rows, sl].astype(BF16), vc_ref[0, rows, sl].astype(BF16),
                                 upper, c, a, None)
            carry_sc[p] = jnp.broadcast_to(c, (2 * ts, LANES))
            acc_sc[p] = a

    @pl.when(c_idx == pl.num_programs(1) - 1)
    def _():
        o_ref[...] = acc_sc[...].astype(BF16)


def _sb_sample(pm, cache_k, cache_v, o_pm, tp, n_streams, ts):
    past = cache_k.shape[1]
    d = cache_k.shape[2]
    nb = past // SB_CB
    rb = tp // ts
    return pl.pallas_call(
        _sb_sample_kernel,
        grid=(n_streams, nb),
        in_specs=[
            pl.BlockSpec((NP, ts, LANES), lambda s, c: (0, rb + s, 0)),
            pl.BlockSpec((NP, ts, LANES), lambda s, c: (1, rb + s, 0)),
            pl.BlockSpec((NP, ts, LANES), lambda s, c: (2, rb + s, 0)),
            pl.BlockSpec((1, SB_CB, d), lambda s, c: (s, nb - 1 - c, 0)),
            pl.BlockSpec((1, SB_CB, d), lambda s, c: (s, nb - 1 - c, 0)),
            pl.BlockSpec(memory_space=pl.ANY),
        ],
        out_specs=pl.BlockSpec((NP, ts, LANES), lambda s, c: (0, rb + s, 0)),
        out_shape=jax.ShapeDtypeStruct(o_pm.shape, o_pm.dtype),
        scratch_shapes=[pltpu.VMEM((NP, 2 * ts, LANES), F32), pltpu.VMEM((NP, ts, LANES), F32)],
        input_output_aliases={5: 0},
        compiler_params=_cparams(("arbitrary", "arbitrary")),
        name="stick_breaking_sample",
    )(pm, pm, pm, cache_k, cache_v, o_pm)


def _attn_out(x_ref, o_ref, wo_ref, g1_ref, n2_ref, sh_ref, sc_ref):
    rows = x_ref.shape[0]
    attn = jnp.concatenate([o_ref[p] for p in range(NP)], axis=1)
    x1 = x_ref[...] + _group_rows(g1_ref[...], rows) * _dot(attn, wo_ref[...])
    h2 = _mod_norm(x1, n2_ref[...], sh_ref[...], sc_ref[...])
    return x1, h2


def _post_dense_kernel(x_ref, o_ref, wo_ref, g1_ref, n2_ref, sh_ref, sc_ref, g2_ref,
                       wg_ref, wu_ref, wd_ref, out_ref, x1_sc, h2_sc, acc_sc):
    j = pl.program_id(1)

    @pl.when(j == 0)
    def _():
        x1, h2 = _attn_out(x_ref, o_ref, wo_ref, g1_ref, n2_ref, sh_ref, sc_ref)
        x1_sc[...] = x1
        h2_sc[...] = h2.astype(BF16)
        acc_sc[...] = jnp.zeros_like(acc_sc)

    h2 = h2_sc[...]
    act = (_silu(_dot(h2, wg_ref[...])) * _dot(h2, wu_ref[...])).astype(BF16)
    acc_sc[...] += _dot(act, wd_ref[...])

    @pl.when(j == pl.num_programs(1) - 1)
    def _():
        out_ref[...] = x1_sc[...] + _group_rows(g2_ref[...], x1_sc.shape[0]) * acc_sc[...]


def _post_dense(x_all, o_pm, wo, g1_g, n2, sh_g, sc_g, g2_g, wg, wu, wd):
    t, d = x_all.shape
    f = wg.shape[1]
    gpt = TM // GROUP
    row = lambda i, j: (i, 0)
    const = lambda i, j: (0, 0)
    return pl.pallas_call(
        _post_dense_kernel,
        grid=(t // TM, f // FC),
        in_specs=[
            pl.BlockSpec((TM, d), row),
            pl.BlockSpec((NP, TM, LANES), lambda i, j: (0, i, 0)),
            pl.BlockSpec((d, d), const),
            pl.BlockSpec((gpt, d), row),
            pl.BlockSpec((1, d), const),
            pl.BlockSpec((gpt, d), row),
            pl.BlockSpec((gpt, d), row),
            pl.BlockSpec((gpt, d), row),
            pl.BlockSpec((d, FC), lambda i, j: (0, j)),
            pl.BlockSpec((d, FC), lambda i, j: (0, j)),
            pl.BlockSpec((FC, d), lambda i, j: (j, 0)),
        ],
        out_specs=pl.BlockSpec((TM, d), row),
        out_shape=jax.ShapeDtypeStruct((t, d), F32),
        scratch_shapes=[pltpu.VMEM((TM, d), F32), pltpu.VMEM((TM, d), BF16), pltpu.VMEM((TM, d), F32)],
        compiler_params=_cparams(("arbitrary", "arbitrary")),
        name="attn_out_dense_ffn",
    )(x_all, o_pm, wo, g1_g, n2, sh_g, sc_g, g2_g, wg, wu, wd)


def _post_router_kernel(x_ref, o_ref, wo_ref, g1_ref, n2_ref, sh_ref, sc_ref, wr_ref,
                        x1_ref, h2_ref, route_ref, cnt_ref, cnt_sc):
    i = pl.program_id(0)
    rows = x_ref.shape[0]

    @pl.when(i == 0)
    def _():
        cnt_sc[...] = jnp.zeros_like(cnt_sc)

    x1, h2 = _attn_out(x_ref, o_ref, wo_ref, g1_ref, n2_ref, sh_ref, sc_ref)
    x1_ref[...] = x1
    h2_ref[...] = h2

    hh, hl = _split_hi_lo(h2)
    wr = wr_ref[...]
    wh, wl = _split_hi_lo(wr)
    logits = _dot(hh, wh) + _dot(hl, wh) + _dot(hh, wl)
    lane = lax.broadcasted_iota(jnp.int32, logits.shape, 1).astype(F32)
    logits = jnp.where(lane < N_EXPERTS, logits, -jnp.inf)
    v1 = logits.max(axis=1, keepdims=True)
    i1 = jnp.where(logits == v1, lane, float(LANES)).min(axis=1, keepdims=True)
    rest = jnp.where(lane == i1, -jnp.inf, logits)
    v2 = rest.max(axis=1, keepdims=True)
    i2 = jnp.where(rest == v2, lane, float(LANES)).min(axis=1, keepdims=True)
    w1 = 1.0 / (1.0 + jnp.exp(v2 - v1))
    w2 = 1.0 - w1

    oh1 = (lane == i1).astype(F32)
    oh2 = (lane == i2).astype(F32)
    cnt = oh1 + oh2
    lower = (lax.broadcasted_iota(jnp.int32, (rows, rows), 0)
             > lax.broadcasted_iota(jnp.int32, (rows, rows), 1)).astype(BF16)
    rank = _dot(lower, cnt.astype(BF16)) + cnt_sc[...]
    r1 = (oh1 * rank).sum(axis=1, keepdims=True)
    r2 = (oh2 * rank).sum(axis=1, keepdims=True)
    cnt_sc[...] += cnt.sum(axis=0, keepdims=True)
    cnt_ref[...] = cnt_sc[...]

    out = jnp.where(lane == 0, i1, 0.0)
    out = jnp.where(lane == 1, i2, out)
    out = jnp.where(lane == 2, w1, out)
    out = jnp.where(lane == 3, w2, out)
    out = jnp.where(lane == 4, r1, out)
    out = jnp.where(lane == 5, r2, out)
    route_ref[...] = out[:, :route_ref.shape[1]]


def _post_router(x_all, o_pm, wo, g1_g, n2, sh_g, sc_g, wr_pad):
    t, d = x_all.shape
    gpt = TM // GROUP
    row = lambda i: (i, 0)
    const = lambda i: (0, 0)
    return pl.pallas_call(
        _post_router_kernel,
        grid=(t // TM,),
        in_specs=[
            pl.BlockSpec((TM, d), row),
            pl.BlockSpec((NP, TM, LANES), lambda i: (0, i, 0)),
            pl.BlockSpec((d, d), const),
            pl.BlockSpec((gpt, d), row),
            pl.BlockSpec((1, d), const),
            pl.BlockSpec((gpt, d), row),
            pl.BlockSpec((gpt, d), row),
            pl.BlockSpec((d, LANES), const),
        ],
        out_specs=[
            pl.BlockSpec((TM, d), row),
            pl.BlockSpec((TM, d), row),
            pl.BlockSpec((TM, 8), row),
            pl.BlockSpec((1, LANES), const),
        ],
        out_shape=[
            jax.ShapeDtypeStruct((t, d), F32),
            jax.ShapeDtypeStruct((t, d), F32),
            jax.ShapeDtypeStruct((t, 8), F32),
            jax.ShapeDtypeStruct((1, LANES), F32),
        ],
        scratch_shapes=[pltpu.VMEM((1, LANES), F32)],
        compiler_params=_cparams(("arbitrary",)),
        name="attn_out_router",
    )(x_all, o_pm, wo, g1_g, n2, sh_g, sc_g, wr_pad)


def _dispatch_kernel(pos_ref, h2_ref, init_ref, xs_ref, sem):
    del init_ref
    rows = h2_ref.shape[0]

    def issue(r, c):
        src = h2_ref.at[pl.ds(r, 1)]
        pltpu.make_async_copy(src, xs_ref.at[pl.ds(pos_ref[0, 0, r], 1)], sem).start()
        pltpu.make_async_copy(src, xs_ref.at[pl.ds(pos_ref[0, 0, rows + r], 1)], sem).start()
        return c

    lax.fori_loop(0, rows, issue, 0)
    for _ in range(2):
        pltpu.make_async_copy(h2_ref, xs_ref.at[pl.ds(0, rows)], sem).wait()


def _dispatch(pos_tiles, h2, xs_init):
    t, d = h2.shape
    return pl.pallas_call(
        _dispatch_kernel,
        grid=(t // TM,),
        in_specs=[
            pl.BlockSpec((1, 1, 2 * TM), lambda i: (i, 0, 0), memory_space=pltpu.SMEM),
            pl.BlockSpec((TM, d), lambda i: (i, 0)),
            pl.BlockSpec(memory_space=pl.ANY),
        ],
        out_specs=pl.BlockSpec(memory_space=pl.ANY),
        out_shape=jax.ShapeDtypeStruct(xs_init.shape, xs_init.dtype),
        scratch_shapes=[pltpu.SemaphoreType.DMA],
        input_output_aliases={2: 0},
        compiler_params=_cparams(("arbitrary",)),
        name="moe_dispatch",
    )(pos_tiles, h2, xs_init)


def _expert_ffn_kernel(te_ref, tv_ref, xs_ref, wg_ref, wu_ref, wd_ref, y_ref, xb_sc, acc_sc):
    r = pl.program_id(0)
    j = pl.program_id(1)
    valid = tv_ref[r] > 0

    @pl.when(j == 0)
    def _():
        xb_sc[...] = xs_ref[...].astype(BF16)
        acc_sc[...] = jnp.zeros_like(acc_sc)

    @pl.when(valid)
    def _():
        x = xb_sc[...]
        act = (_silu(_dot(x, wg_ref[0])) * _dot(x, wu_ref[0])).astype(BF16)
        acc_sc[...] += _dot(act, wd_ref[0])

    @pl.when(j == pl.num_programs(1) - 1)
    def _():
        y_ref[...] = acc_sc[...]


def _expert_ffn(tile_expert, tile_valid, xs, wg, wu, wd):
    r_pad, d = xs.shape
    f = wg.shape[2]
    grid_spec = pltpu.PrefetchScalarGridSpec(
        num_scalar_prefetch=2,
        grid=(r_pad // TR, f // FC),
        in_specs=[
            pl.BlockSpec((TR, d), lambda r, j, te, tv: (r, 0)),
            pl.BlockSpec((1, d, FC), lambda r, j, te, tv: (te[r], 0, j)),
            pl.BlockSpec((1, d, FC), lambda r, j, te, tv: (te[r], 0, j)),
            pl.BlockSpec((1, FC, d), lambda r, j, te, tv: (te[r], j, 0)),
        ],
        out_specs=pl.BlockSpec((TR, d), lambda r, j, te, tv: (r, 0)),
        scratch_shapes=[pltpu.VMEM((TR, d), BF16), pltpu.VMEM((TR, d), F32)],
    )
    return pl.pallas_call(
        _expert_ffn_kernel,
        grid_spec=grid_spec,
        out_shape=jax.ShapeDtypeStruct((r_pad, d), F32),
        compiler_params=_cparams(("arbitrary", "arbitrary")),
        name="moe_expert_ffn",
    )(tile_expert, tile_valid, xs, wg, wu, wd)


def _combine_kernel(pos_ref, x1_ref, g2_ref, route_ref, y_ref, out_ref, buf, sem):
    rows = x1_ref.shape[0]

    def issue(r, c):
        pltpu.make_async_copy(y_ref.at[pl.ds(pos_ref[0, 0, r], 1)], buf.at[0, pl.ds(r, 1)], sem).start()
        pltpu.make_async_copy(y_ref.at[pl.ds(pos_ref[0, 0, rows + r], 1)], buf.at[1, pl.ds(r, 1)], sem).start()
        return c

    lax.fori_loop(0, rows, issue, 0)
    for k in range(2):
        pltpu.make_async_copy(y_ref.at[pl.ds(0, rows)], buf.at[k], sem).wait()
    route = route_ref[...]
    moe = route[:, 2:3] * buf[0] + route[:, 3:4] * buf[1]
    out_ref[...] = x1_ref[...] + _group_rows(g2_ref[...], rows) * moe


def _combine(pos_tiles, x1, g2_g, route, y):
    t, d = x1.shape
    gpt = TMC // GROUP
    row = lambda i: (i, 0)
    return pl.pallas_call(
        _combine_kernel,
        grid=(t // TMC,),
        in_specs=[
            pl.BlockSpec((1, 1, 2 * TMC), lambda i: (i, 0, 0), memory_space=pltpu.SMEM),
            pl.BlockSpec((TMC, d), row),
            pl.BlockSpec((gpt, d), row),
            pl.BlockSpec((TMC, 8), row),
            pl.BlockSpec(memory_space=pl.ANY),
        ],
        out_specs=pl.BlockSpec((TMC, d), row),
        out_shape=jax.ShapeDtypeStruct((t, d), F32),
        scratch_shapes=[pltpu.VMEM((2, TMC, d), F32), pltpu.SemaphoreType.DMA],
        compiler_params=_cparams(("arbitrary",)),
        name="moe_combine",
    )(pos_tiles, x1, g2_g, route, y)


def _band_valid(q_pos, k_pos):
    qc = q_pos // CHUNK
    kc = k_pos // CHUNK
    return (k_pos[None, :] >= 0) & (kc[None, :] <= qc[:, None]) & (kc[None, :] >= qc[:, None] - N_PAST_CHUNKS)


def _band_bias(rel_bias, q_pos, k_pos):
    rel = jnp.clip(q_pos[:, None] - k_pos[None, :], -REL_CLIP, REL_CLIP) + REL_CLIP
    return jnp.where(_band_valid(q_pos, k_pos)[None], rel_bias.astype(F32)[:, rel], NEG_INF)


def _tile_pos(pos2, tile):
    t = pos2.shape[1]
    return pos2.reshape(2, t // tile, tile).transpose(1, 0, 2).reshape(t // tile, 1, 2 * tile)


def kernel(x_prompt, x_sample, cache_a_k, cache_a_v, cache_b_k, cache_b_v, c_prompt, c_sample, w_qkv, w_o, norm1_g, norm2_g, w_ada, b_ada, q_norm_g, k_norm_g, rel_bias, w_gate_d, w_up_d, w_down_d, w_router, w_gate_e, w_up_e, w_down_e):
    bp, sp, d = x_prompt.shape
    bs, ts, _ = x_sample.shape
    past_len = cache_b_k.shape[2]
    win = cache_a_k.shape[2]
    depth = w_qkv.shape[0]
    assert depth == 2 and d == N_HEADS * HEAD_DIM and ts == GROUP
    tp, tsm = bp * sp, bs * ts
    t_all = tp + tsm
    assert sp % TM == 0 and tsm % TM == 0 and sp % TQ == 0 and t_all % TMC == 0
    assert past_len % SB_CB == 0 and win == BAND_PAST and sp >= BAND_PAST and past_len >= win

    x_all = jnp.concatenate([x_prompt.reshape(tp, d), x_sample.reshape(tsm, d)], axis=0)

    mod = _ada(jnp.concatenate([c_prompt, c_sample], axis=0), w_ada, b_ada)

    def groups(v):
        return jnp.concatenate([jnp.repeat(v[:bp], sp // GROUP, axis=0), v[bp:]], axis=0)

    bd = (jnp.arange(256)[:, None] // HEAD_DIM == jnp.arange(256)[None, :] // HEAD_DIM)
    bd = (bd.astype(F32) / HEAD_DIM).astype(BF16)
    ones_row = jnp.ones((1, d), F32)

    outs = {}
    for i in range(depth):
        sh1, sc1, g1, sh2, sc2, g2 = [groups(m) for m in jnp.split(mod[i], 6, axis=-1)]
        band = i % 2 == 0
        qg = jnp.tile(q_norm_g[0], N_HEADS)[None] if band else ones_row
        kg = jnp.tile(k_norm_g[0], N_HEADS)[None] if band else ones_row
        pm, kf, vf = _qkv(x_all, norm1_g[i][None], sh1, sc1, w_qkv[i].astype(BF16), qg, kg, bd, band)
        if band:
            q_pos = jnp.arange(TQ, dtype=jnp.int32) + BAND_PAST
            k_pos = jnp.arange(N_BAND_BLK * TK, dtype=jnp.int32) + BAND_PAST - (N_BAND_BLK - 1) * TK
            bias_p = _band_bias(rel_bias[0], q_pos, k_pos).reshape(N_HEADS, TQ, N_BAND_BLK, TK)
            bias_p = bias_p.transpose(0, 2, 1, 3)
            o_pm = _band_prompt(pm, bias_p, bp, sp)
            pos_s = past_len + jnp.arange(ts, dtype=jnp.int32)
            pos_win = past_len - win + jnp.arange(win, dtype=jnp.int32)
            o_pm = _band_sample(pm, cache_a_k[0].reshape(bs, win, d), cache_a_v[0].reshape(bs, win, d),
                                _band_bias(rel_bias[0], pos_s, pos_win), _band_bias(rel_bias[0], pos_s, pos_s),
                                o_pm, tp, bs, ts)
            kp = kf[:tp].reshape(bp, sp, N_HEADS, HEAD_DIM)
            vp = vf[:tp].reshape(bp, sp, N_HEADS, HEAD_DIM)
            ks = kf[tp:].reshape(bs, ts, N_HEADS, HEAD_DIM)
            vs = vf[tp:].reshape(bs, ts, N_HEADS, HEAD_DIM)
            outs["a_k_p"] = kp[:, sp - BAND_PAST:][None]
            outs["a_v_p"] = vp[:, sp - BAND_PAST:][None]
            outs["a_k_s"] = jnp.concatenate([cache_a_k[0], ks], axis=1)[:, ts:][None]
            outs["a_v_s"] = jnp.concatenate([cache_a_v[0], vs], axis=1)[:, ts:][None]
            x_all = _post_dense(x_all, o_pm, w_o[i].astype(BF16), g1, norm2_g[i][None], sh2, sc2, g2,
                                w_gate_d[0].astype(BF16), w_up_d[0].astype(BF16), w_down_d[0].astype(BF16))
        else:
            o_pm = _sb_prompt(pm, bp, sp)
            o_pm = _sb_sample(pm, cache_b_k[0].reshape(bs, past_len, d), cache_b_v[0].reshape(bs, past_len, d),
                              o_pm, tp, bs, ts)
            outs["b_k_p"] = kf[:tp].reshape(1, bp, sp, N_HEADS, HEAD_DIM)
            outs["b_v_p"] = vf[:tp].reshape(1, bp, sp, N_HEADS, HEAD_DIM)
            outs["b_k_s"] = kf[tp:].reshape(1, bs, ts, N_HEADS, HEAD_DIM)
            outs["b_v_s"] = vf[tp:].reshape(1, bs, ts, N_HEADS, HEAD_DIM)
            wr_pad = jnp.pad(w_router[0], ((0, 0), (0, LANES - N_EXPERTS)))
            x1, h2, route, cnt = _post_router(x_all, o_pm, w_o[i].astype(BF16), g1, norm2_g[i][None],
                                              sh2, sc2, wr_pad)
            counts = cnt[0, :N_EXPERTS].astype(jnp.int32)
            sizes = (counts + TR - 1) // TR * TR
            ends = jnp.cumsum(sizes)
            offs = ends - sizes
            e12 = route[:, 0:2].astype(jnp.int32)
            pos2 = (offs[e12] + route[:, 4:6].astype(jnp.int32)).T
            r_pad = (2 * t_all // TR + N_EXPERTS) * TR
            tile_start = jnp.arange(r_pad // TR, dtype=jnp.int32) * TR
            tile_expert = jnp.minimum(jnp.searchsorted(ends, tile_start, side="right"),
                                      N_EXPERTS - 1).astype(jnp.int32)
            tile_valid = (tile_start < ends[-1]).astype(jnp.int32)
            xs = _dispatch(_tile_pos(pos2, TM), h2, jnp.zeros((r_pad, d), F32))
            y = _expert_ffn(tile_expert, tile_valid, xs, w_gate_e[0].astype(BF16),
                            w_up_e[0].astype(BF16), w_down_e[0].astype(BF16))
            x_all = _combine(_tile_pos(pos2, TMC), x1, g2, route, y)

    return (x_all[:tp].reshape(bp, sp, d), x_all[tp:].reshape(bs, ts, d),
            outs["a_k_p"], outs["a_v_p"], outs["a_k_s"], outs["a_v_s"],
            outs["b_k_p"], outs["b_v_p"], outs["b_k_s"], outs["b_v_s"])
```

```python
import functools

import numpy as np
import jax
import jax.numpy as jnp
from jax import lax
from jax.experimental import pallas as pl
from jax.experimental.pallas import tpu as pltpu

F32 = jnp.float32
BF16 = jnp.bfloat16

CHUNK = 64
N_HEADS = 16
HEAD_DIM = 64
N_PAST_CHUNKS = 8
BAND_PAST = N_PAST_CHUNKS * CHUNK
REL_CLIP = 128
N_EXPERTS = 8
EPS = 1e-6
NEG_INF = -1e30

LANES = 128
VMEM_LIMIT = 56 * 1024 * 1024

GROUP = 32
TM = 512
FC = 512
TQ = 128
TK = 128
NP = N_HEADS // 2
BAND_KEYS = BAND_PAST + TQ
SB_PG = 2
SB_STOP = -110.0
TR = 512
TMC = 256


def _cparams(sem):
    return pltpu.CompilerParams(dimension_semantics=sem, vmem_limit_bytes=VMEM_LIMIT)


def _dot(a, b):
    return jnp.dot(a, b, preferred_element_type=F32)


def _dot_nt(a, b):
    return lax.dot_general(a, b, (((1,), (1,)), ((), ())), preferred_element_type=F32)


def _split_hi_lo(x):
    hi = x.astype(BF16)
    lo = (x - hi.astype(F32)).astype(BF16)
    return hi, lo


def _silu(x):
    return x * (1.0 / (1.0 + jnp.exp(-x)))


def _lane_lt64(shape):
    return lax.broadcasted_iota(jnp.int32, shape, len(shape) - 1) < HEAD_DIM


def _row_lt64(shape):
    return lax.broadcasted_iota(jnp.int32, shape, 0) < HEAD_DIM


def _group_rows(vec_rows, rows):
    g, d = vec_rows.shape
    return jnp.broadcast_to(vec_rows[:, None, :], (g, rows // g, d)).reshape(rows, d)


def _mod_norm(x, g, shift_g, scale_g):
    rows = x.shape[0]
    ms = jnp.mean(x * x, axis=-1, keepdims=True)
    y = x * lax.rsqrt(ms + EPS) * g
    return y * (1.0 + _group_rows(scale_g, rows)) + _group_rows(shift_g, rows)


def _stack_heads(q2):
    m = _lane_lt64(q2.shape)
    z = jnp.zeros_like(q2)
    return jnp.concatenate([jnp.where(m, q2, z), jnp.where(m, z, q2)], axis=0)


def _pv_feature_major(pb, vt):
    r = pb.shape[0] // 2
    m = _row_lt64(vt.shape)
    z = jnp.zeros_like(vt)
    return _dot_nt(pb[:r], jnp.where(m, vt, z)) + _dot_nt(pb[r:], jnp.where(m, z, vt))


def _pv_row_major(pb, v2):
    r = pb.shape[0] // 2
    m = _lane_lt64(v2.shape)
    z = jnp.zeros_like(v2)
    return _dot(pb[:r], jnp.where(m, v2, z)) + _dot(pb[r:], jnp.where(m, z, v2))


def _ada_kernel(c_ref, w_ref, b_ref, o_ref):
    s = _silu(c_ref[...]).astype(BF16)
    o_ref[0] = _dot(s, w_ref[0].astype(BF16)) + b_ref[0]


def _ada(c_all, w_ada, b_ada):
    depth, d, n = w_ada.shape
    nb = c_all.shape[0]
    tn = 1536
    return pl.pallas_call(
        _ada_kernel,
        grid=(depth, n // tn),
        in_specs=[
            pl.BlockSpec((nb, d), lambda l, j: (0, 0)),
            pl.BlockSpec((1, d, tn), lambda l, j: (l, 0, j)),
            pl.BlockSpec((1, 1, tn), lambda l, j: (l, 0, j)),
        ],
        out_specs=pl.BlockSpec((1, nb, tn), lambda l, j: (l, 0, j)),
        out_shape=jax.ShapeDtypeStruct((depth, nb, n), F32),
        compiler_params=_cparams(("arbitrary", "arbitrary")),
        name="ada_modulation",
    )(c_all, w_ada, b_ada.reshape(depth, 1, n))


def _head_mean_sq_rows(x, bd_ref):
    hi, lo = _split_hi_lo(x * x)
    w = bd_ref.shape[0]
    cols = []
    for c in range(x.shape[1] // w):
        sl = slice(c * w, (c + 1) * w)
        cols.append(_dot(hi[:, sl], bd_ref[...]) + _dot(lo[:, sl], bd_ref[...]))
    return jnp.concatenate(cols, axis=1)


def _head_norm_cols(xt, gain_col):
    d, c = xt.shape
    x3 = xt.reshape(N_HEADS, HEAD_DIM, c)
    ms = jnp.mean(x3 * x3, axis=1, keepdims=True)
    return (x3 * lax.rsqrt(ms + EPS)).reshape(d, c) * gain_col


def _project(x, g_ref, sh_ref, sc_ref, wq_ref, wkvt_ref, qg_ref, kg_ref, bd_ref, qk_norm):
    d = x.shape[1]
    h = _mod_norm(x, g_ref[...], sh_ref[...], sc_ref[...]).astype(BF16)
    q = _dot(h, wq_ref[...])
    kvt = _dot_nt(wkvt_ref[...], h)
    kt, vt = kvt[:d], kvt[d:]
    if qk_norm:
        q = q * lax.rsqrt(_head_mean_sq_rows(q, bd_ref) + EPS) * qg_ref[...]
        kt = _head_norm_cols(kt, kg_ref[...])
    return (q * (HEAD_DIM ** -0.5)).astype(BF16), kt, vt


def _qkv_prompt_kernel(x_ref, g_ref, sh_ref, sc_ref, wq_ref, wkvt_ref, qg_ref, kg_ref, bd_ref,
                       q_ref, ktb_ref, vtb_ref, ktf_ref, vtf_ref, *, qk_norm, tail_only):
    qs, kt, vt = _project(x_ref[...], g_ref, sh_ref, sc_ref, wq_ref, wkvt_ref, qg_ref, kg_ref, bd_ref,
                          qk_norm)
    for p in range(NP):
        q_ref[p] = qs[:, p * LANES:(p + 1) * LANES]
    ktb_ref[0] = kt.astype(BF16)
    vtb_ref[0] = vt.astype(BF16)
    if tail_only:
        @pl.when((pl.program_id(0) + 1) % tail_only == 0)
        def _():
            ktf_ref[0] = kt
            vtf_ref[0] = vt
    else:
        ktf_ref[0] = kt
        vtf_ref[0] = vt


def _qkv_prompt(x, g, shift_g, scale_g, wq, wkvt, qg_row, kg_col, bd, n_batch, s_len, qk_norm):
    d = x.shape[1]
    tpb = s_len // TM
    gpt = TM // GROUP
    row = lambda i: (i, 0)
    const = lambda i: (0, 0)
    fmaj = lambda i: (i // tpb, 0, i % tpb)
    if qk_norm:
        f_spec = pl.BlockSpec((1, d, TM), lambda i: (i // tpb, 0, 0))
        f_shape = jax.ShapeDtypeStruct((n_batch, d, TM), F32)
    else:
        f_spec = pl.BlockSpec((1, d, TM), fmaj)
        f_shape = jax.ShapeDtypeStruct((n_batch, d, s_len), F32)
    return pl.pallas_call(
        functools.partial(_qkv_prompt_kernel, qk_norm=qk_norm, tail_only=tpb if qk_norm else 0),
        grid=(n_batch * tpb,),
        in_specs=[
            pl.BlockSpec((TM, d), row),
            pl.BlockSpec((1, d), const),
            pl.BlockSpec((gpt, d), row),
            pl.BlockSpec((gpt, d), row),
            pl.BlockSpec((d, d), const),
            pl.BlockSpec((2 * d, d), const),
            pl.BlockSpec((1, d), const),
            pl.BlockSpec((d, 1), const),
            pl.BlockSpec(bd.shape, const),
        ],
        out_specs=[
            pl.BlockSpec((NP, TM, LANES), lambda i: (0, i, 0)),
            pl.BlockSpec((1, d, TM), fmaj),
            pl.BlockSpec((1, d, TM), fmaj),
            f_spec, f_spec,
        ],
        out_shape=[
            jax.ShapeDtypeStruct((NP, n_batch * s_len, LANES), BF16),
            jax.ShapeDtypeStruct((n_batch, d, s_len), BF16),
            jax.ShapeDtypeStruct((n_batch, d, s_len), BF16),
            f_shape, f_shape,
        ],
        compiler_params=_cparams(("arbitrary",)),
        name="qkv_prompt_qknorm" if qk_norm else "qkv_prompt",
    )(x, g, shift_g, scale_g, wq, wkvt, qg_row, kg_col, bd)


def _qkv_sample_kernel(x_ref, g_ref, sh_ref, sc_ref, wq_ref, wkvt_ref, qg_ref, kg_ref, bd_ref,
                       q_ref, kv_ref, kf_ref, vf_ref, *, qk_norm):
    qs, kt, vt = _project(x_ref[...], g_ref, sh_ref, sc_ref, wq_ref, wkvt_ref, qg_ref, kg_ref, bd_ref,
                          qk_norm)
    k, v = kt.T, vt.T
    if qk_norm:
        kf_ref[...] = kt
        vf_ref[...] = vt
    else:
        kf_ref[...] = k
        vf_ref[...] = v
    kb, vb = k.astype(BF16), v.astype(BF16)
    for p in range(NP):
        sl = slice(p * LANES, (p + 1) * LANES)
        q_ref[p] = qs[:, sl]
        kv_ref[p] = kb[:, sl]
        kv_ref[NP + p] = vb[:, sl]


def _qkv_sample(x, row_block, g, shift_g, scale_g, wq, wkvt, qg_row, kg_col, bd, qk_norm):
    d = x.shape[1]
    gpt = TM // GROUP
    row = lambda i: (row_block, 0)
    const = lambda i: (0, 0)
    f_shape = jax.ShapeDtypeStruct((d, TM) if qk_norm else (TM, d), F32)
    return pl.pallas_call(
        functools.partial(_qkv_sample_kernel, qk_norm=qk_norm),
        grid=(1,),
        in_specs=[
            pl.BlockSpec((TM, d), row),
            pl.BlockSpec((1, d), const),
            pl.BlockSpec((gpt, d), const),
            pl.BlockSpec((gpt, d), const),
            pl.BlockSpec((d, d), const),
            pl.BlockSpec((2 * d, d), const),
            pl.BlockSpec((1, d), const),
            pl.BlockSpec((d, 1), const),
            pl.BlockSpec(bd.shape, const),
        ],
        out_specs=[
            pl.BlockSpec((NP, TM, LANES), lambda i: (0, 0, 0)),
            pl.BlockSpec((2 * NP, TM, LANES), lambda i: (0, 0, 0)),
            pl.BlockSpec(f_shape.shape, const),
            pl.BlockSpec(f_shape.shape, const),
        ],
        out_shape=[
            jax.ShapeDtypeStruct((NP, TM, LANES), BF16),
            jax.ShapeDtypeStruct((2 * NP, TM, LANES), BF16),
            f_shape, f_shape,
        ],
        compiler_params=_cparams(("arbitrary",)),
        name="qkv_sample_qknorm" if qk_norm else "qkv_sample",
    )(x, g, shift_g, scale_g, wq, wkvt, qg_row, kg_col, bd)


def _band_prompt_kernel(q_ref, kt_ref, vt_ref, bias_ref, o_ref):
    s_len = q_ref.shape[1]
    back = BAND_PAST // TK

    def tile(i, carry):
        row0 = pl.multiple_of(i * TQ, TQ)
        key0 = pl.multiple_of(jnp.maximum(i - back, 0) * TK, TK)
        col0 = pl.multiple_of(jnp.maximum(back - i, 0) * TK, TK)
        qm = _stack_heads(q_ref[0, pl.ds(row0, TQ), :])
        s = _dot(qm, kt_ref[0, :, pl.ds(key0, BAND_KEYS)])
        s = s + bias_ref[:, :, pl.ds(col0, BAND_KEYS)].reshape(2 * TQ, BAND_KEYS)
        m = s.max(axis=1, keepdims=True)
        p = jnp.exp(s - m)
        l = p.sum(axis=1, keepdims=True)
        acc = _pv_feature_major(p.astype(BF16), vt_ref[0, :, pl.ds(key0, BAND_KEYS)])
        linv = jnp.where(_lane_lt64((TQ, LANES)), 1.0 / l[:TQ], 1.0 / l[TQ:])
        o_ref[0, pl.ds(row0, TQ), :] = (acc * linv).astype(BF16)
        return carry

    lax.fori_loop(0, s_len // TQ, tile, 0, unroll=2)


def _band_prompt(q_pm, ktb, vtb, bias_p, n_batch, s_len):
    return pl.pallas_call(
        _band_prompt_kernel,
        grid=(n_batch, NP),
        in_specs=[
            pl.BlockSpec((1, s_len, LANES), lambda b, p: (p, b, 0)),
            pl.BlockSpec((1, LANES, s_len), lambda b, p: (b, p, 0)),
            pl.BlockSpec((1, LANES, s_len), lambda b, p: (b, p, 0)),
            pl.BlockSpec((2, TQ, bias_p.shape[2]), lambda b, p: (p, 0, 0)),
        ],
        out_specs=pl.BlockSpec((1, s_len, LANES), lambda b, p: (p, b, 0)),
        out_shape=jax.ShapeDtypeStruct((NP, n_batch * s_len, LANES), BF16),
        compiler_params=_cparams(("arbitrary", "arbitrary")),
        name="band_attention_prompt",
    )(q_pm, ktb, vtb, bias_p)


def _band_sample_kernel(q_ref, kv_ref, kc_ref, vc_ref, bc_ref, bn_ref, o_ref):
    ts = q_ref.shape[1]
    for p in range(NP):
        rows = slice(p * LANES, (p + 1) * LANES)
        qm = _stack_heads(q_ref[p])
        kn, vn = kv_ref[p], kv_ref[NP + p]
        s1 = _dot(qm, kc_ref[0, rows, :].astype(BF16)) + bc_ref[2 * p:2 * p + 2].reshape(2 * ts, -1)
        s2 = _dot_nt(qm, kn) + bn_ref[2 * p:2 * p + 2].reshape(2 * ts, ts)
        m = jnp.maximum(s1.max(axis=1, keepdims=True), s2.max(axis=1, keepdims=True))
        p1 = jnp.exp(s1 - m)
        p2 = jnp.exp(s2 - m)
        l = p1.sum(axis=1, keepdims=True) + p2.sum(axis=1, keepdims=True)
        acc = (_pv_feature_major(p1.astype(BF16), vc_ref[0, rows, :].astype(BF16))
               + _pv_row_major(p2.astype(BF16), vn))
        linv = jnp.where(_lane_lt64((ts, LANES)), 1.0 / l[:ts], 1.0 / l[ts:])
        o_ref[p] = (acc * linv).astype(BF16)


def _band_sample(q_s, kv_s, cache_kt, cache_vt, bias_c, bias_n, n_streams, ts):
    d, win = cache_kt.shape[1], cache_kt.shape[2]
    return pl.pallas_call(
        _band_sample_kernel,
        grid=(n_streams,),
        in_specs=[
            pl.BlockSpec((NP, ts, LANES), lambda s: (0, s, 0)),
            pl.BlockSpec((2 * NP, ts, LANES), lambda s: (0, s, 0)),
            pl.BlockSpec((1, d, win), lambda s: (s, 0, 0)),
            pl.BlockSpec((1, d, win), lambda s: (s, 0, 0)),
            pl.BlockSpec(bias_c.shape, lambda s: (0, 0, 0)),
            pl.BlockSpec(bias_n.shape, lambda s: (0, 0, 0)),
        ],
        out_specs=pl.BlockSpec((NP, ts, LANES), lambda s: (0, s, 0)),
        out_shape=jax.ShapeDtypeStruct(q_s.shape, BF16),
        compiler_params=_cparams(("arbitrary",)),
        name="band_attention_sample",
    )(q_s, kv_s, cache_kt, cache_vt, bias_c, bias_n)


def _sb_weights(z, upper, carry, mask):
    sp = jnp.maximum(z, 0.0) + jnp.log(1.0 + jnp.exp(-jnp.abs(z)))
    lk = -sp
    if mask is not None:
        lk = jnp.where(mask, lk, 0.0)
    hi, lo = _split_hi_lo(lk)
    after = _dot(hi, upper) + _dot(lo, upper) + carry
    a = jnp.exp(z - sp + after)
    if mask is not None:
        a = jnp.where(mask, a, 0.0)
    return a.astype(BF16), carry + lk.sum(axis=1, keepdims=True)


def _upper_tri(n):
    return (lax.broadcasted_iota(jnp.int32, (n, n), 0)
            > lax.broadcasted_iota(jnp.int32, (n, n), 1)).astype(BF16)


def _causal_mask(rows, cols):
    r = lax.broadcasted_iota(jnp.int32, (2 * rows, cols), 0)
    r = jnp.where(r >= rows, r - rows, r)
    c = lax.broadcasted_iota(jnp.int32, (2 * rows, cols), 1)
    return c < r


def _any_live(carries):
    m = carries[0].max()
    for c in carries[1:]:
        m = jnp.maximum(m, c.max())
    return (m >= SB_STOP).astype(jnp.int32)


def _sb_prompt_kernel(q_ref, kt_ref, vt_ref, o_ref):
    npg, s_len = q_ref.shape[0], q_ref.shape[1]
    upper = _upper_tri(TK)
    diag_mask = _causal_mask(TQ, TK)
    all_true = diag_mask | True

    units = [(g, t) for t in range(2) for g in range(npg)]

    def block(g, qm, blk, carry, acc, mask):
        cols = pl.ds(pl.multiple_of(blk * TK, TK), TK)
        rows = slice(g * LANES, (g + 1) * LANES)
        ab, carry = _sb_weights(_dot(qm, kt_ref[0, rows, cols]), upper, carry, mask)
        return carry, acc + _pv_feature_major(ab, vt_ref[0, rows, cols])

    def tile_pair(i2, c0):
        first = 2 * i2
        qms, carries, accs = [], [], []
        for g, t in units:
            row0 = pl.multiple_of((first + t) * TQ, TQ)
            qm = _stack_heads(q_ref[g, pl.ds(row0, TQ), :])
            c, a = block(g, qm, first + t, jnp.zeros((2 * TQ, 1), F32), jnp.zeros((TQ, LANES), F32),
                         diag_mask)
            qms.append(qm)
            carries.append(c)
            accs.append(a)

        def cond(st):
            return jnp.logical_and(first + 1 - st[0] >= 0, st[1] > 0)

        def body(st):
            k = st[0]
            cs, as_ = list(st[2]), list(st[3])
            for u, (g, t) in enumerate(units):
                blk = first + t - k
                mask = None if t == 1 else jnp.logical_and(all_true, blk >= 0)
                cs[u], as_[u] = block(g, qms[u], jnp.maximum(blk, 0), cs[u], as_[u], mask)
            return (k + 1, _any_live(cs), tuple(cs), tuple(as_))

        st = lax.while_loop(cond, body, (1, _any_live(carries), tuple(carries), tuple(accs)))
        for u, (g, t) in enumerate(units):
            row0 = pl.multiple_of((first + t) * TQ, TQ)
            o_ref[g, pl.ds(row0, TQ), :] = st[3][u].astype(BF16)
        return c0

    lax.fori_loop(0, s_len // (2 * TQ), tile_pair, 0)


def _sb_prompt(q_pm, ktb, vtb, n_batch, s_len):
    ng = NP // SB_PG
    return pl.pallas_call(
        _sb_prompt_kernel,
        grid=(n_batch, ng),
        in_specs=[
            pl.BlockSpec((SB_PG, s_len, LANES), lambda b, g: (g, b, 0)),
            pl.BlockSpec((1, SB_PG * LANES, s_len), lambda b, g: (b, g, 0)),
            pl.BlockSpec((1, SB_PG * LANES, s_len), lambda b, g: (b, g, 0)),
        ],
        out_specs=pl.BlockSpec((SB_PG, s_len, LANES), lambda b, g: (g, b, 0)),
        out_shape=jax.ShapeDtypeStruct((NP, n_batch * s_len, LANES), BF16),
        compiler_params=_cparams(("arbitrary", "arbitrary")),
        name="stick_breaking_prompt",
    )(q_pm, ktb, vtb)


def _sb_sample_kernel(q_ref, kv_ref, kc_hbm, vc_hbm, o_ref, kbuf, vbuf, sem):
    s = pl.program_id(0)
    ts = q_ref.shape[1]
    nblk = kc_hbm.shape[2] // TK
    upper = _upper_tri(TK)

    def copies(j):
        slot = j % 2
        cols = pl.ds(pl.multiple_of(j * TK, TK), TK)
        return (pltpu.make_async_copy(kc_hbm.at[s, :, cols], kbuf.at[slot], sem.at[0, slot]),
                pltpu.make_async_copy(vc_hbm.at[s, :, cols], vbuf.at[slot], sem.at[1, slot]))

    def start(j):
        for c in copies(j):
            c.start()

    def wait(j):
        for c in copies(j):
            c.wait()

    start(nblk - 1)

    qms, carries, accs = [], [], []
    mask = _causal_mask(ts, ts)
    upper_n = _upper_tri(ts)
    for p in range(NP):
        qm = _stack_heads(q_ref[p])
        ab, c = _sb_weights(_dot_nt(qm, kv_ref[p]), upper_n, jnp.zeros((2 * ts, 1), F32), mask)
        qms.append(qm)
        carries.append(c)
        accs.append(_pv_row_major(ab, kv_ref[NP + p]))

    def cond(st):
        return jnp.logical_and(st[0] >= 0, st[1] > 0)

    def body(st):
        j = st[0]
        cs, as_ = list(st[2]), list(st[3])
        wait(j)

        @pl.when(j > 0)
        def _():
            start(j - 1)

        slot = j % 2
        for p in range(NP):
            rows = slice(p * LANES, (p + 1) * LANES)
            ab, cs[p] = _sb_weights(_dot(qms[p], kbuf[slot, rows, :].astype(BF16)), upper, cs[p], None)
            as_[p] = as_[p] + _pv_feature_major(ab, vbuf[slot, rows, :].astype(BF16))
        return (j - 1, _any_live(cs), tuple(cs), tuple(as_))

    st = lax.while_loop(cond, body, (nblk - 1, _any_live(carries), tuple(carries), tuple(accs)))

    @pl.when(st[0] >= 0)
    def _():
        wait(st[0])

    for p in range(NP):
        o_ref[p] = st[3][p].astype(BF16)


def _sb_sample(q_s, kv_s, cache_kt, cache_vt, n_streams, ts):
    d = cache_kt.shape[1]
    return pl.pallas_call(
        _sb_sample_kernel,
        grid=(n_streams,),
        in_specs=[
            pl.BlockSpec((NP, ts, LANES), lambda s: (0, s, 0)),
            pl.BlockSpec((2 * NP, ts, LANES), lambda s: (0, s, 0)),
            pl.BlockSpec(memory_space=pl.ANY),
            pl.BlockSpec(memory_space=pl.ANY),
        ],
        out_specs=pl.BlockSpec((NP, ts, LANES), lambda s: (0, s, 0)),
        out_shape=jax.ShapeDtypeStruct(q_s.shape, BF16),
        scratch_shapes=[pltpu.VMEM((2, d, TK), F32), pltpu.VMEM((2, d, TK), F32),
                        pltpu.SemaphoreType.DMA((2, 2))],
        compiler_params=_cparams(("arbitrary",)),
        name="stick_breaking_sample",
    )(q_s, kv_s, cache_kt, cache_vt)


def _pick(is_sample, prompt_ref, sample_ref):
    return jnp.where(is_sample, sample_ref[...], prompt_ref[...])


def _attn_out(x, o, wo_ref, g1_ref, n2_ref, sh_ref, sc_ref):
    rows = x.shape[0]
    attn = jnp.concatenate([o[p] for p in range(NP)], axis=1)
    x1 = x + _group_rows(g1_ref[...], rows) * _dot(attn, wo_ref[...])
    h2 = _mod_norm(x1, n2_ref[...], sh_ref[...], sc_ref[...])
    return x1, h2


def _post_dense_kernel(xp_ref, xs_ref, op_ref, os_ref, wo_ref, g1_ref, n2_ref, sh_ref, sc_ref, g2_ref,
                       wg_ref, wu_ref, wd_ref, out_ref, x1_sc, h2_sc, acc_sc):
    j = pl.program_id(1)

    @pl.when(j == 0)
    def _():
        is_sample = pl.program_id(0) == pl.num_programs(0) - 1
        x1, h2 = _attn_out(_pick(is_sample, xp_ref, xs_ref), _pick(is_sample, op_ref, os_ref),
                           wo_ref, g1_ref, n2_ref, sh_ref, sc_ref)
        x1_sc[...] = x1
        h2_sc[...] = h2.astype(BF16)
        acc_sc[...] = jnp.zeros_like(acc_sc)

    h2 = h2_sc[...]
    act = (_silu(_dot(h2, wg_ref[...])) * _dot(h2, wu_ref[...])).astype(BF16)
    acc_sc[...] += _dot(act, wd_ref[...])

    @pl.when(j == pl.num_programs(1) - 1)
    def _():
        out_ref[...] = x1_sc[...] + _group_rows(g2_ref[...], x1_sc.shape[0]) * acc_sc[...]


def _post_dense(x_p, x_s, o_p, o_s, wo, g1_g, n2, sh_g, sc_g, g2_g, wg, wu, wd):
    tp, d = x_p.shape
    ntp = tp // TM
    f = wg.shape[1]
    gpt = TM // GROUP
    row = lambda i, j: (i, 0)
    prow = lambda i, j: (jnp.minimum(i, ntp - 1), 0)
    const = lambda i, j: (0, 0)
    return pl.pallas_call(
        _post_dense_kernel,
        grid=(ntp + 1, f // FC),
        in_specs=[
            pl.BlockSpec((TM, d), prow),
            pl.BlockSpec((TM, d), const),
            pl.BlockSpec((NP, TM, LANES), lambda i, j: (0, jnp.minimum(i, ntp - 1), 0)),
            pl.BlockSpec((NP, TM, LANES), lambda i, j: (0, 0, 0)),
            pl.BlockSpec((d, d), const),
            pl.BlockSpec((gpt, d), row),
            pl.BlockSpec((1, d), const),
            pl.BlockSpec((gpt, d), row),
            pl.BlockSpec((gpt, d), row),
            pl.BlockSpec((gpt, d), row),
            pl.BlockSpec((d, FC), lambda i, j: (0, j)),
            pl.BlockSpec((d, FC), lambda i, j: (0, j)),
            pl.BlockSpec((FC, d), lambda i, j: (j, 0)),
        ],
        out_specs=pl.BlockSpec((TM, d), row),
        out_shape=jax.ShapeDtypeStruct((tp + TM, d), F32),
        scratch_shapes=[pltpu.VMEM((TM, d), F32), pltpu.VMEM((TM, d), BF16), pltpu.VMEM((TM, d), F32)],
        compiler_params=_cparams(("arbitrary", "arbitrary")),
        name="attn_out_dense_ffn",
    )(x_p, x_s, o_p, o_s, wo, g1_g, n2, sh_g, sc_g, g2_g, wg, wu, wd)


def _post_router_kernel(x_ref, op_ref, os_ref, wo_ref, g1_ref, n2_ref, sh_ref, sc_ref, wr_ref,
                        x1_ref, h2_ref, route_ref, cnt_ref, cnt_sc):
    i = pl.program_id(0)
    rows = x_ref.shape[0]

    @pl.when(i == 0)
    def _():
        cnt_sc[...] = jnp.zeros_like(cnt_sc)

    is_sample = i == pl.num_programs(0) - 1
    x1, h2 = _attn_out(x_ref[...], _pick(is_sample, op_ref, os_ref), wo_ref, g1_ref, n2_ref, sh_ref, sc_ref)
    x1_ref[...] = x1
    h2_ref[...] = h2

    hh, hl = _split_hi_lo(h2)
    wh, wl = _split_hi_lo(wr_ref[...])
    logits = _dot(hh, wh) + _dot(hl, wh) + _dot(hh, wl)
    lane = lax.broadcasted_iota(jnp.int32, logits.shape, 1).astype(F32)
    logits = jnp.where(lane < N_EXPERTS, logits, -jnp.inf)
    v1 = logits.max(axis=1, keepdims=True)
    i1 = jnp.where(logits == v1, lane, float(LANES)).min(axis=1, keepdims=True)
    rest = jnp.where(lane == i1, -jnp.inf, logits)
    v2 = rest.max(axis=1, keepdims=True)
    i2 = jnp.where(rest == v2, lane, float(LANES)).min(axis=1, keepdims=True)
    w1 = 1.0 / (1.0 + jnp.exp(v2 - v1))
    w2 = 1.0 - w1

    oh1 = (lane == i1).astype(F32)
    oh2 = (lane == i2).astype(F32)
    cnt = oh1 + oh2
    lower = (lax.broadcasted_iota(jnp.int32, (rows, rows), 0)
             > lax.broadcasted_iota(jnp.int32, (rows, rows), 1)).astype(BF16)
    rank = _dot(lower, cnt.astype(BF16)) + cnt_sc[...]
    r1 = (oh1 * rank).sum(axis=1, keepdims=True)
    r2 = (oh2 * rank).sum(axis=1, keepdims=True)
    cnt_sc[...] += cnt.sum(axis=0, keepdims=True)
    cnt_ref[...] = cnt_sc[...]

    out = jnp.where(lane == 0, i1, 0.0)
    out = jnp.where(lane == 1, i2, out)
    out = jnp.where(lane == 2, w1, out)
    out = jnp.where(lane == 3, w2, out)
    out = jnp.where(lane == 4, r1, out)
    out = jnp.where(lane == 5, r2, out)
    route_ref[...] = out[:, :route_ref.shape[1]]


def _post_router(x_all, o_p, o_s, wo, g1_g, n2, sh_g, sc_g, wr_pad):
    t, d = x_all.shape
    ntp = t // TM - 1
    gpt = TM // GROUP
    row = lambda i: (i, 0)
    const = lambda i: (0, 0)
    return pl.pallas_call(
        _post_router_kernel,
        grid=(t // TM,),
        in_specs=[
            pl.BlockSpec((TM, d), row),
            pl.BlockSpec((NP, TM, LANES), lambda i: (0, jnp.minimum(i, ntp - 1), 0)),
            pl.BlockSpec((NP, TM, LANES), lambda i: (0, 0, 0)),
            pl.BlockSpec((d, d), const),
            pl.BlockSpec((gpt, d), row),
            pl.BlockSpec((1, d), const),
            pl.BlockSpec((gpt, d), row),
            pl.BlockSpec((gpt, d), row),
            pl.BlockSpec((d, LANES), const),
        ],
        out_specs=[
            pl.BlockSpec((TM, d), row),
            pl.BlockSpec((TM, d), row),
            pl.BlockSpec((TM, 8), row),
            pl.BlockSpec((1, LANES), const),
        ],
        out_shape=[
            jax.ShapeDtypeStruct((t, d), F32),
            jax.ShapeDtypeStruct((t, d), F32),
            jax.ShapeDtypeStruct((t, 8), F32),
            jax.ShapeDtypeStruct((1, LANES), F32),
        ],
        scratch_shapes=[pltpu.VMEM((1, LANES), F32)],
        compiler_params=_cparams(("arbitrary",)),
        name="attn_out_router",
    )(x_all, o_p, o_s, wo, g1_g, n2, sh_g, sc_g, wr_pad)


def _dispatch_kernel(pos_ref, h2_ref, init_ref, xs_ref, sem):
    del init_ref
    rows = h2_ref.shape[0]

    def issue(r, c):
        src = h2_ref.at[pl.ds(r, 1)]
        pltpu.make_async_copy(src, xs_ref.at[pl.ds(pos_ref[0, 0, r], 1)], sem).start()
        pltpu.make_async_copy(src, xs_ref.at[pl.ds(pos_ref[0, 0, rows + r], 1)], sem).start()
        return c

    lax.fori_loop(0, rows, issue, 0)
    for _ in range(2):
        pltpu.make_async_copy(h2_ref, xs_ref.at[pl.ds(0, rows)], sem).wait()


def _dispatch(pos_tiles, h2, xs_init):
    t, d = h2.shape
    return pl.pallas_call(
        _dispatch_kernel,
        grid=(t // TM,),
        in_specs=[
            pl.BlockSpec((1, 1, 2 * TM), lambda i: (i, 0, 0), memory_space=pltpu.SMEM),
            pl.BlockSpec((TM, d), lambda i: (i, 0)),
            pl.BlockSpec(memory_space=pl.ANY),
        ],
        out_specs=pl.BlockSpec(memory_space=pl.ANY),
        out_shape=jax.ShapeDtypeStruct(xs_init.shape, xs_init.dtype),
        scratch_shapes=[pltpu.SemaphoreType.DMA],
        input_output_aliases={2: 0},
        compiler_params=_cparams(("arbitrary",)),
        name="moe_dispatch",
    )(pos_tiles, h2, xs_init)


def _expert_ffn_kernel(te_ref, tv_ref, xs_ref, wg_ref, wu_ref, wd_ref, y_ref, xb_sc, acc_sc):
    r = pl.program_id(0)
    j = pl.program_id(1)
    valid = tv_ref[r] > 0

    @pl.when(j == 0)
    def _():
        xb_sc[...] = xs_ref[...].astype(BF16)
        acc_sc[...] = jnp.zeros_like(acc_sc)

    @pl.when(valid)
    def _():
        x = xb_sc[...]
        act = (_silu(_dot(x, wg_ref[0])) * _dot(x, wu_ref[0])).astype(BF16)
        acc_sc[...] += _dot(act, wd_ref[0])

    @pl.when(j == pl.num_programs(1) - 1)
    def _():
        y_ref[...] = acc_sc[...]


def _expert_ffn(tile_expert, tile_valid, xs, wg, wu, wd):
    r_pad, d = xs.shape
    f = wg.shape[2]
    grid_spec = pltpu.PrefetchScalarGridSpec(
        num_scalar_prefetch=2,
        grid=(r_pad // TR, f // FC),
        in_specs=[
            pl.BlockSpec((TR, d), lambda r, j, te, tv: (r, 0)),
            pl.BlockSpec((1, d, FC), lambda r, j, te, tv: (te[r], 0, j)),
            pl.BlockSpec((1, d, FC), lambda r, j, te, tv: (te[r], 0, j)),
            pl.BlockSpec((1, FC, d), lambda r, j, te, tv: (te[r], j, 0)),
        ],
        out_specs=pl.BlockSpec((TR, d), lambda r, j, te, tv: (r, 0)),
        scratch_shapes=[pltpu.VMEM((TR, d), BF16), pltpu.VMEM((TR, d), F32)],
    )
    return pl.pallas_call(
        _expert_ffn_kernel,
        grid_spec=grid_spec,
        out_shape=jax.ShapeDtypeStruct((r_pad, d), F32),
        compiler_params=_cparams(("arbitrary", "arbitrary")),
        name="moe_expert_ffn",
    )(tile_expert, tile_valid, xs, wg, wu, wd)


def _combine_kernel(pos_ref, x1_ref, g2_ref, route_ref, y_ref, outp_ref, outs_ref, buf, sem, *, ntp):
    i = pl.program_id(0)
    rows = x1_ref.shape[0]

    def issue(r, c):
        pltpu.make_async_copy(y_ref.at[pl.ds(pos_ref[0, 0, r], 1)], buf.at[0, pl.ds(r, 1)], sem).start()
        pltpu.make_async_copy(y_ref.at[pl.ds(pos_ref[0, 0, rows + r], 1)], buf.at[1, pl.ds(r, 1)], sem).start()
        return c

    lax.fori_loop(0, rows, issue, 0)
    for k in range(2):
        pltpu.make_async_copy(y_ref.at[pl.ds(0, rows)], buf.at[k], sem).wait()
    route = route_ref[...]
    moe = route[:, 2:3] * buf[0] + route[:, 3:4] * buf[1]
    res = x1_ref[...] + _group_rows(g2_ref[...], rows) * moe

    @pl.when(i < ntp)
    def _():
        outp_ref[...] = res

    @pl.when(i >= ntp)
    def _():
        outs_ref[...] = res


def _combine(pos_tiles, x1, g2_g, route, y, tp):
    t, d = x1.shape
    ntp = tp // TMC
    gpt = TMC // GROUP
    row = lambda i: (i, 0)
    return pl.pallas_call(
        functools.partial(_combine_kernel, ntp=ntp),
        grid=(t // TMC,),
        in_specs=[
            pl.BlockSpec((1, 1, 2 * TMC), lambda i: (i, 0, 0), memory_space=pltpu.SMEM),
            pl.BlockSpec((TMC, d), row),
            pl.BlockSpec((gpt, d), row),
            pl.BlockSpec((TMC, 8), row),
            pl.BlockSpec(memory_space=pl.ANY),
        ],
        out_specs=[
            pl.BlockSpec((TMC, d), lambda i: (jnp.minimum(i, ntp - 1), 0)),
            pl.BlockSpec((TMC, d), lambda i: (jnp.maximum(i - ntp, 0), 0)),
        ],
        out_shape=[jax.ShapeDtypeStruct((tp, d), F32), jax.ShapeDtypeStruct((t - tp, d), F32)],
        scratch_shapes=[pltpu.VMEM((2, TMC, d), F32), pltpu.SemaphoreType.DMA],
        compiler_params=_cparams(("arbitrary",)),
        name="moe_combine",
    )(pos_tiles, x1, g2_g, route, y)


def _band_valid(q_pos, k_pos):
    qc = q_pos // CHUNK
    kc = k_pos // CHUNK
    return (k_pos[None, :] >= 0) & (kc[None, :] <= qc[:, None]) & (kc[None, :] >= qc[:, None] - N_PAST_CHUNKS)


def _band_bias(rel_bias, q_pos, k_pos):
    nq, nk = len(q_pos), len(k_pos)
    assert np.all(np.diff(q_pos) == 1) and np.all(np.diff(k_pos) == 1)
    period = nq + nk
    dist = (q_pos[0] - k_pos[0]) + (nq - 1) - np.arange(period)
    vec = rel_bias.astype(F32)[:, np.clip(dist, -REL_CLIP, REL_CLIP) + REL_CLIP]
    flat = jnp.tile(jnp.roll(vec, -(nq - 1), axis=1), (1, nq))[:, :nq * (period - 1)]
    tab = flat.reshape(-1, nq, period - 1)[:, :, :nk]
    return jnp.where(jnp.asarray(_band_valid(q_pos, k_pos))[None], tab, NEG_INF)


def _tile_pos(pos2, tile):
    t = pos2.shape[1]
    return pos2.reshape(2, t // tile, tile).transpose(1, 0, 2).reshape(t // tile, 1, 2 * tile)


def _feature_major(cache):
    n, length = cache.shape[0], cache.shape[1]
    return cache.transpose(0, 2, 3, 1).reshape(n, N_HEADS * HEAD_DIM, length)


def _token_major(xt):
    n, _, length = xt.shape
    return xt.reshape(n, N_HEADS, HEAD_DIM, length).transpose(0, 3, 1, 2)


def kernel(x_prompt, x_sample, cache_a_k, cache_a_v, cache_b_k, cache_b_v, c_prompt, c_sample, w_qkv, w_o, norm1_g, norm2_g, w_ada, b_ada, q_norm_g, k_norm_g, rel_bias, w_gate_d, w_up_d, w_down_d, w_router, w_gate_e, w_up_e, w_down_e):
    bp, sp, d = x_prompt.shape
    bs, ts, _ = x_sample.shape
    past_len = cache_b_k.shape[2]
    win = cache_a_k.shape[2]
    depth = w_qkv.shape[0]
    assert depth == 2 and d == N_HEADS * HEAD_DIM and ts == GROUP
    tp, tsm = bp * sp, bs * ts
    t_all = tp + tsm
    assert sp % TM == 0 and tsm == TM and sp % (2 * TQ) == 0 and t_all % TMC == 0 and tp % TMC == 0
    assert win == BAND_PAST == TM and sp >= BAND_KEYS and past_len % TK == 0 and past_len >= win

    x_p = x_prompt.reshape(tp, d)
    x_s = x_sample.reshape(tsm, d)

    mod = _ada(jnp.concatenate([c_prompt, c_sample], axis=0), w_ada, b_ada)

    def groups(v):
        return jnp.concatenate([jnp.repeat(v[:bp], sp // GROUP, axis=0), v[bp:]], axis=0)

    bd = (np.arange(256)[:, None] // HEAD_DIM == np.arange(256)[None, :] // HEAD_DIM)
    bd = jnp.asarray(bd.astype(np.float32) / HEAD_DIM, BF16)
    ones_row = jnp.ones((1, d), F32)
    ones_col = jnp.ones((d, 1), F32)
    gps = tp // GROUP

    for i in range(depth):
        sh1, sc1, g1, sh2, sc2, g2 = [groups(m) for m in jnp.split(mod[i], 6, axis=-1)]
        band = i % 2 == 0
        wq = w_qkv[i][:, :d].astype(BF16)
        wkvt = w_qkv[i][:, d:].T.astype(BF16)
        qg = jnp.tile(q_norm_g[0], N_HEADS)[None] if band else ones_row
        kg = jnp.tile(k_norm_g[0], N_HEADS)[:, None] if band else ones_col
        n1 = norm1_g[i][None]
        if band:
            q_p, ktb, vtb, ktf, vtf = _qkv_prompt(x_p, n1, sh1, sc1, wq, wkvt, qg, kg, bd, bp, sp, True)
            q_s, kv_s, ksf, vsf = _qkv_sample(x_s, 0, n1, sh1[gps:], sc1[gps:], wq, wkvt, qg, kg, bd, True)
            q_pos = np.arange(TQ) + BAND_PAST
            bias_p = _band_bias(rel_bias[0], q_pos, np.arange(BAND_KEYS))
            bias_p = jnp.pad(bias_p, ((0, 0), (0, 0), (0, BAND_PAST)), constant_values=NEG_INF)
            o_p = _band_prompt(q_p, ktb, vtb, bias_p, bp, sp)
            pos_s = past_len + np.arange(ts)
            pos_win = past_len - win + np.arange(win)
            ca_k, ca_v = _feature_major(cache_a_k[0]), _feature_major(cache_a_v[0])
            o_s = _band_sample(q_s, kv_s, ca_k, ca_v, _band_bias(rel_bias[0], pos_s, pos_win),
                               _band_bias(rel_bias[0], pos_s, pos_s), bs, ts)
            a_k_p = _token_major(ktf)[None]
            a_v_p = _token_major(vtf)[None]
            new_k = ksf.reshape(d, bs, ts).transpose(1, 0, 2)
            new_v = vsf.reshape(d, bs, ts).transpose(1, 0, 2)
            a_k_s = _token_major(jnp.concatenate([ca_k[:, :, ts:], new_k], axis=2))[None]
            a_v_s = _token_major(jnp.concatenate([ca_v[:, :, ts:], new_v], axis=2))[None]
            x_all = _post_dense(x_p, x_s, o_p, o_s, w_o[i].astype(BF16), g1, norm2_g[i][None], sh2, sc2, g2,
                                w_gate_d[0].astype(BF16), w_up_d[0].astype(BF16), w_down_d[0].astype(BF16))
        else:
            q_p, ktb, vtb, ktf, vtf = _qkv_prompt(x_all, n1, sh1, sc1, wq, wkvt, qg, kg, bd, bp, sp, False)
            q_s, kv_s, ksf, vsf = _qkv_sample(x_all, tp // TM, n1, sh1[gps:], sc1[gps:], wq, wkvt, qg, kg, bd,
                                              False)
            o_p = _sb_prompt(q_p, ktb, vtb, bp, sp)
            o_s = _sb_sample(q_s, kv_s, _feature_major(cache_b_k[0]), _feature_major(cache_b_v[0]), bs, ts)
            b_k_p = _token_major(ktf)[None]
            b_v_p = _token_major(vtf)[None]
            b_k_s = ksf.reshape(1, bs, ts, N_HEADS, HEAD_DIM)
            b_v_s = vsf.reshape(1, bs, ts, N_HEADS, HEAD_DIM)
            wr_pad = jnp.pad(w_router[0], ((0, 0), (0, LANES - N_EXPERTS)))
            x1, h2, route, cnt = _post_router(x_all, o_p, o_s, w_o[i].astype(BF16), g1, norm2_g[i][None],
                                              sh2, sc2, wr_pad)
            counts = cnt[0, :N_EXPERTS].astype(jnp.int32)
            sizes = (counts + TR - 1) // TR * TR
            ends = jnp.cumsum(sizes)
            offs = ends - sizes
            e12 = route[:, 0:2].astype(jnp.int32)
            off12 = jnp.sum(jnp.where(e12[:, :, None] == jnp.arange(N_EXPERTS), offs, 0), axis=-1)
            pos2 = (off12 + route[:, 4:6].astype(jnp.int32)).T
            r_pad = (2 * t_all // TR + N_EXPERTS) * TR
            tile_start = jnp.arange(r_pad // TR, dtype=jnp.int32) * TR
            tile_expert = jnp.minimum(jnp.sum(tile_start[:, None] >= ends[None, :], axis=1),
                                      N_EXPERTS - 1).astype(jnp.int32)
            tile_valid = (tile_start < ends[-1]).astype(jnp.int32)
            xs = _dispatch(_tile_pos(pos2, TM), h2, jnp.zeros((r_pad, d), F32))
            y = _expert_ffn(tile_expert, tile_valid, xs, w_gate_e[0].astype(BF16),
                            w_up_e[0].astype(BF16), w_down_e[0].astype(BF16))
            y_p, y_s = _combine(_tile_pos(pos2, TMC), x1, g2, route, y, tp)

    return (y_p.reshape(bp, sp, d), y_s.reshape(bs, ts, d), a_k_p, a_v_p, a_k_s, a_v_s,
            b_k_p, b_v_p, b_k_s, b_v_s)
```

```python
import functools

import numpy as np
import jax
import jax.numpy as jnp
from jax import lax
from jax.experimental import pallas as pl
from jax.experimental.pallas import tpu as pltpu

F32 = jnp.float32
BF16 = jnp.bfloat16

CHUNK = 64
N_HEADS = 16
HEAD_DIM = 64
N_PAST_CHUNKS = 8
BAND_PAST = N_PAST_CHUNKS * CHUNK
REL_CLIP = 128
N_EXPERTS = 8
EPS = 1e-6
NEG_INF = -1e30

LANES = 128
VMEM_LIMIT = 56 * 1024 * 1024

GROUP = 32
TM = 512
FC = 1792
TQ = 128
TK = 128
NP = N_HEADS // 2
BAND_KEYS = BAND_PAST + TQ
BAND_UNROLL = 4
BAND_ROWS = 32
SB_PG = 2
SB_TILES = 4
SB_ROWS = 64
SB_STOP = -90.0
TR = 512
TMC = 256


def _cparams(sem):
    return pltpu.CompilerParams(dimension_semantics=sem, vmem_limit_bytes=VMEM_LIMIT)


def _dot(a, b):
    return jnp.dot(a, b, preferred_element_type=F32)


def _dot_nt(a, b):
    return lax.dot_general(a, b, (((1,), (1,)), ((), ())), preferred_element_type=F32)


def _split_hi_lo(x):
    hi = x.astype(BF16)
    lo = (x - hi.astype(F32)).astype(BF16)
    return hi, lo


def _silu(x):
    return x * (1.0 / (1.0 + jnp.exp(-x)))


def _lane_lt64(shape):
    return lax.broadcasted_iota(jnp.int32, shape, len(shape) - 1) < HEAD_DIM


def _group_rows(vec_rows, rows):
    g, d = vec_rows.shape
    return jnp.broadcast_to(vec_rows[:, None, :], (g, rows // g, d)).reshape(rows, d)


def _mod_norm(x, g, shift_g, scale_g):
    rows = x.shape[0]
    ms = jnp.mean(x * x, axis=-1, keepdims=True)
    y = x * lax.rsqrt(ms + EPS) * g
    return y * (1.0 + _group_rows(scale_g, rows)) + _group_rows(shift_g, rows)


def _stack_heads(q2):
    m = _lane_lt64(q2.shape)
    z = jnp.zeros_like(q2)
    return jnp.concatenate([jnp.where(m, q2, z), jnp.where(m, z, q2)], axis=0)


def _pv_feature_major(pb, vt):
    r = pb.shape[0] // 2
    both = _dot_nt(pb, vt)
    return jnp.where(_lane_lt64((r, LANES)), both[:r], both[r:])


def _pv_row_major(pb, v2):
    r = pb.shape[0] // 2
    both = _dot(pb, v2)
    return jnp.where(_lane_lt64((r, LANES)), both[:r], both[r:])


def _ada_kernel(c_ref, w_ref, b_ref, o_ref):
    s = _silu(c_ref[...]).astype(BF16)
    o_ref[0] = _dot(s, w_ref[0].astype(BF16)) + b_ref[0]


def _ada(c_all, w_ada, b_ada):
    depth, d, n = w_ada.shape
    nb = c_all.shape[0]
    tn = 1536
    return pl.pallas_call(
        _ada_kernel,
        grid=(depth, n // tn),
        in_specs=[
            pl.BlockSpec((nb, d), lambda l, j: (0, 0)),
            pl.BlockSpec((1, d, tn), lambda l, j: (l, 0, j)),
            pl.BlockSpec((1, 1, tn), lambda l, j: (l, 0, j)),
        ],
        out_specs=pl.BlockSpec((1, nb, tn), lambda l, j: (l, 0, j)),
        out_shape=jax.ShapeDtypeStruct((depth, nb, n), F32),
        compiler_params=_cparams(("arbitrary", "arbitrary")),
        name="ada_modulation",
    )(c_all, w_ada, b_ada.reshape(depth, 1, n))


def _head_mean_sq_rows(x, bd_ref):
    hi, lo = _split_hi_lo(x * x)
    w = bd_ref.shape[0]
    cols = []
    for c in range(x.shape[1] // w):
        sl = slice(c * w, (c + 1) * w)
        cols.append(_dot(hi[:, sl], bd_ref[...]) + _dot(lo[:, sl], bd_ref[...]))
    return jnp.concatenate(cols, axis=1)


def _head_norm_cols(xt, gain_col):
    d, c = xt.shape
    x3 = xt.reshape(N_HEADS, HEAD_DIM, c)
    ms = jnp.mean(x3 * x3, axis=1, keepdims=True)
    return (x3 * lax.rsqrt(ms + EPS)).reshape(d, c) * gain_col


def _project(x, g_ref, sh_ref, sc_ref, wq_ref, wkvt_ref, qg_ref, kg_ref, bd_ref, qk_norm):
    d = x.shape[1]
    h = _mod_norm(x, g_ref[...], sh_ref[...], sc_ref[...]).astype(BF16)
    q = _dot(h, wq_ref[...])
    kvt = _dot_nt(wkvt_ref[...], h)
    kt, vt = kvt[:d], kvt[d:]
    if qk_norm:
        q = q * lax.rsqrt(_head_mean_sq_rows(q, bd_ref) + EPS) * qg_ref[...]
        kt = _head_norm_cols(kt, kg_ref[...])
    return (q * (HEAD_DIM ** -0.5)).astype(BF16), kt, vt


def _qkv_prompt_kernel(x_ref, g_ref, sh_ref, sc_ref, wq_ref, wkvt_ref, qg_ref, kg_ref, bd_ref,
                       q_ref, ktb_ref, vtb_ref, ktf_ref, vtf_ref, *, qk_norm, tail_only):
    qs, kt, vt = _project(x_ref[...], g_ref, sh_ref, sc_ref, wq_ref, wkvt_ref, qg_ref, kg_ref, bd_ref,
                          qk_norm)
    for p in range(NP):
        q_ref[p] = qs[:, p * LANES:(p + 1) * LANES]
    ktb_ref[0] = kt.astype(BF16)
    vtb_ref[0] = vt.astype(BF16)
    if tail_only:
        @pl.when((pl.program_id(0) + 1) % tail_only == 0)
        def _():
            ktf_ref[0] = kt
            vtf_ref[0] = vt
    else:
        ktf_ref[0] = kt
        vtf_ref[0] = vt


def _qkv_prompt(x, g, shift_g, scale_g, wq, wkvt, qg_row, kg_col, bd, n_batch, s_len, qk_norm):
    d = x.shape[1]
    tpb = s_len // TM
    gpt = TM // GROUP
    row = lambda i: (i, 0)
    const = lambda i: (0, 0)
    fmaj = lambda i: (i // tpb, 0, i % tpb)
    if qk_norm:
        f_spec = pl.BlockSpec((1, d, TM), lambda i: (i // tpb, 0, 0))
        f_shape = jax.ShapeDtypeStruct((n_batch, d, TM), F32)
    else:
        f_spec = pl.BlockSpec((1, d, TM), fmaj)
        f_shape = jax.ShapeDtypeStruct((n_batch, d, s_len), F32)
    return pl.pallas_call(
        functools.partial(_qkv_prompt_kernel, qk_norm=qk_norm, tail_only=tpb if qk_norm else 0),
        grid=(n_batch * tpb,),
        in_specs=[
            pl.BlockSpec((TM, d), row),
            pl.BlockSpec((1, d), const),
            pl.BlockSpec((gpt, d), row),
            pl.BlockSpec((gpt, d), row),
            pl.BlockSpec((d, d), const),
            pl.BlockSpec((2 * d, d), const),
            pl.BlockSpec((1, d), const),
            pl.BlockSpec((d, 1), const),
            pl.BlockSpec(bd.shape, const),
        ],
        out_specs=[
            pl.BlockSpec((NP, TM, LANES), lambda i: (0, i, 0)),
            pl.BlockSpec((1, d, TM), fmaj),
            pl.BlockSpec((1, d, TM), fmaj),
            f_spec, f_spec,
        ],
        out_shape=[
            jax.ShapeDtypeStruct((NP, n_batch * s_len, LANES), BF16),
            jax.ShapeDtypeStruct((n_batch, d, s_len), BF16),
            jax.ShapeDtypeStruct((n_batch, d, s_len), BF16),
            f_shape, f_shape,
        ],
        compiler_params=_cparams(("arbitrary",)),
        name="qkv_prompt_qknorm" if qk_norm else "qkv_prompt",
    )(x, g, shift_g, scale_g, wq, wkvt, qg_row, kg_col, bd)


def _qkv_sample_kernel(x_ref, g_ref, sh_ref, sc_ref, wq_ref, wkvt_ref, qg_ref, kg_ref, bd_ref,
                       q_ref, kv_ref, kf_ref, vf_ref, *, qk_norm):
    qs, kt, vt = _project(x_ref[...], g_ref, sh_ref, sc_ref, wq_ref, wkvt_ref, qg_ref, kg_ref, bd_ref,
                          qk_norm)
    k, v = kt.T, vt.T
    if qk_norm:
        kf_ref[...] = kt
        vf_ref[...] = vt
    else:
        kf_ref[...] = k
        vf_ref[...] = v
    kb, vb = k.astype(BF16), v.astype(BF16)
    for p in range(NP):
        sl = slice(p * LANES, (p + 1) * LANES)
        q_ref[p] = qs[:, sl]
        kv_ref[p] = kb[:, sl]
        kv_ref[NP + p] = vb[:, sl]


def _qkv_sample(x, row_block, g, shift_g, scale_g, wq, wkvt, qg_row, kg_col, bd, qk_norm):
    d = x.shape[1]
    gpt = TM // GROUP
    row = lambda i: (row_block, 0)
    const = lambda i: (0, 0)
    f_shape = jax.ShapeDtypeStruct((d, TM) if qk_norm else (TM, d), F32)
    return pl.pallas_call(
        functools.partial(_qkv_sample_kernel, qk_norm=qk_norm),
        grid=(1,),
        in_specs=[
            pl.BlockSpec((TM, d), row),
            pl.BlockSpec((1, d), const),
            pl.BlockSpec((gpt, d), const),
            pl.BlockSpec((gpt, d), const),
            pl.BlockSpec((d, d), const),
            pl.BlockSpec((2 * d, d), const),
            pl.BlockSpec((1, d), const),
            pl.BlockSpec((d, 1), const),
            pl.BlockSpec(bd.shape, const),
        ],
        out_specs=[
            pl.BlockSpec((NP, TM, LANES), lambda i: (0, 0, 0)),
            pl.BlockSpec((2 * NP, TM, LANES), lambda i: (0, 0, 0)),
            pl.BlockSpec(f_shape.shape, const),
            pl.BlockSpec(f_shape.shape, const),
        ],
        out_shape=[
            jax.ShapeDtypeStruct((NP, TM, LANES), BF16),
            jax.ShapeDtypeStruct((2 * NP, TM, LANES), BF16),
            f_shape, f_shape,
        ],
        compiler_params=_cparams(("arbitrary",)),
        name="qkv_sample_qknorm" if qk_norm else "qkv_sample",
    )(x, g, shift_g, scale_g, wq, wkvt, qg_row, kg_col, bd)


def _band_prompt_kernel(q_ref, kt_ref, vt_ref, bias_ref, o_ref):
    s_len = q_ref.shape[1]
    back = BAND_PAST // TK

    def tiles(n, carry):
        idx = [n * BAND_UNROLL + u for u in range(BAND_UNROLL)]
        row0 = [pl.multiple_of(i * TQ, TQ) for i in idx]
        key0 = [pl.multiple_of(jnp.maximum(i - back, 0) * TK, TK) for i in idx]
        col0 = [pl.multiple_of(jnp.maximum(back - i, 0) * TK, TK) for i in idx]
        scores = [_dot(_stack_heads(q_ref[0, pl.ds(row0[u], TQ), :]), kt_ref[0, :, pl.ds(key0[u], BAND_KEYS)])
                  for u in range(BAND_UNROLL)]
        for u in range(BAND_UNROLL):
            ps, ls = [], []
            for h in range(2):
                for r0 in range(0, TQ, BAND_ROWS):
                    s = (scores[u][h * TQ + r0:h * TQ + r0 + BAND_ROWS]
                         + bias_ref[h, r0:r0 + BAND_ROWS, pl.ds(col0[u], BAND_KEYS)])
                    p = jnp.exp(s - s.max(axis=1, keepdims=True))
                    ls.append(p.sum(axis=1, keepdims=True))
                    ps.append(p.astype(BF16))
            l = jnp.concatenate(ls, axis=0)
            acc = _pv_feature_major(jnp.concatenate(ps, axis=0), vt_ref[0, :, pl.ds(key0[u], BAND_KEYS)])
            linv = jnp.where(_lane_lt64((TQ, LANES)), 1.0 / l[:TQ], 1.0 / l[TQ:])
            o_ref[0, pl.ds(row0[u], TQ), :] = (acc * linv).astype(BF16)
        return carry

    lax.fori_loop(0, s_len // (BAND_UNROLL * TQ), tiles, 0)


def _band_prompt(q_pm, ktb, vtb, bias_p, n_batch, s_len):
    return pl.pallas_call(
        _band_prompt_kernel,
        grid=(n_batch, NP),
        in_specs=[
            pl.BlockSpec((1, s_len, LANES), lambda b, p: (p, b, 0)),
            pl.BlockSpec((1, LANES, s_len), lambda b, p: (b, p, 0)),
            pl.BlockSpec((1, LANES, s_len), lambda b, p: (b, p, 0)),
            pl.BlockSpec((2, TQ, bias_p.shape[2]), lambda b, p: (p, 0, 0)),
        ],
        out_specs=pl.BlockSpec((1, s_len, LANES), lambda b, p: (p, b, 0)),
        out_shape=jax.ShapeDtypeStruct((NP, n_batch * s_len, LANES), BF16),
        compiler_params=_cparams(("arbitrary", "arbitrary")),
        name="band_attention_prompt",
    )(q_pm, ktb, vtb, bias_p)


def _band_sample_kernel(q_ref, kv_ref, kc_ref, vc_ref, bc_ref, bn_ref, o_ref):
    ts = q_ref.shape[1]
    for p in range(NP):
        rows = slice(p * LANES, (p + 1) * LANES)
        qm = _stack_heads(q_ref[p])
        kn, vn = kv_ref[p], kv_ref[NP + p]
        s1 = _dot(qm, kc_ref[0, rows, :].astype(BF16)) + bc_ref[2 * p:2 * p + 2].reshape(2 * ts, -1)
        s2 = _dot_nt(qm, kn) + bn_ref[2 * p:2 * p + 2].reshape(2 * ts, ts)
        m = jnp.maximum(s1.max(axis=1, keepdims=True), s2.max(axis=1, keepdims=True))
        p1 = jnp.exp(s1 - m)
        p2 = jnp.exp(s2 - m)
        l = p1.sum(axis=1, keepdims=True) + p2.sum(axis=1, keepdims=True)
        acc = (_pv_feature_major(p1.astype(BF16), vc_ref[0, rows, :].astype(BF16))
               + _pv_row_major(p2.astype(BF16), vn))
        linv = jnp.where(_lane_lt64((ts, LANES)), 1.0 / l[:ts], 1.0 / l[ts:])
        o_ref[p] = (acc * linv).astype(BF16)


def _band_sample(q_s, kv_s, cache_kt, cache_vt, bias_c, bias_n, n_streams, ts):
    d, win = cache_kt.shape[1], cache_kt.shape[2]
    return pl.pallas_call(
        _band_sample_kernel,
        grid=(n_streams,),
        in_specs=[
            pl.BlockSpec((NP, ts, LANES), lambda s: (0, s, 0)),
            pl.BlockSpec((2 * NP, ts, LANES), lambda s: (0, s, 0)),
            pl.BlockSpec((1, d, win), lambda s: (s, 0, 0)),
            pl.BlockSpec((1, d, win), lambda s: (s, 0, 0)),
            pl.BlockSpec(bias_c.shape, lambda s: (0, 0, 0)),
            pl.BlockSpec(bias_n.shape, lambda s: (0, 0, 0)),
        ],
        out_specs=pl.BlockSpec((NP, ts, LANES), lambda s: (0, s, 0)),
        out_shape=jax.ShapeDtypeStruct(q_s.shape, BF16),
        compiler_params=_cparams(("arbitrary",)),
        name="band_attention_sample",
    )(q_s, kv_s, cache_kt, cache_vt, bias_c, bias_n)


def _sb_weights(z, upper, carry, mask):
    sp = jnp.maximum(z, 0.0) + jnp.log(1.0 + jnp.exp(-jnp.abs(z)))
    lk = -sp
    if mask is not None:
        lk = jnp.where(mask, lk, 0.0)
    hi, lo = _split_hi_lo(lk)
    after = _dot(hi, upper) + _dot(lo, upper) + carry
    a = jnp.exp(z - sp + after)
    if mask is not None:
        a = jnp.where(mask, a, 0.0)
    return a.astype(BF16), carry + lk.sum(axis=1, keepdims=True)


def _sb_weights_multi(zs, upper_ones, carries, masks):
    rows, c = zs[0].shape
    chunks = [slice(r0, r0 + SB_ROWS) for r0 in range(0, rows, SB_ROWS)]
    logits, halves = [], []
    for z, mask in zip(zs, masks):
        zl, hl = [], []
        for ch in chunks:
            zc = z[ch]
            nz = -zc
            lk = jnp.minimum(nz, 0.0) - jnp.log(1.0 + jnp.exp(jnp.minimum(zc, nz)))
            if mask is not None:
                lk = jnp.where(mask[ch], lk, 0.0)
            hi, lo = _split_hi_lo(lk)
            zl.append(zc + lk)
            hl.append(jnp.concatenate([hi, lo], axis=1))
        logits.append(zl)
        halves.append(jnp.concatenate(hl, axis=0))
    sums = [_dot(hl, upper_ones) for hl in halves]
    weights, new_carries = [], []
    for zl, sm, carry, mask in zip(logits, sums, carries, masks):
        ws, cs = [], []
        for i, ch in enumerate(chunks):
            a = jnp.exp(zl[i] + (sm[ch, :c] + carry[ch]))
            if mask is not None:
                a = jnp.where(mask[ch], a, 0.0)
            ws.append(a.astype(BF16))
            cs.append(carry[ch] + sm[ch, c:])
        weights.append(jnp.concatenate(ws, axis=0))
        new_carries.append(jnp.concatenate(cs, axis=0))
    return weights, new_carries


def _upper_tri(n):
    return (lax.broadcasted_iota(jnp.int32, (n, n), 0)
            > lax.broadcasted_iota(jnp.int32, (n, n), 1)).astype(BF16)


def _upper_tri_ones(n):
    half = jnp.concatenate([_upper_tri(n), jnp.ones((n, n), BF16)], axis=1)
    return jnp.concatenate([half, half], axis=0)


def _causal_mask(rows, cols):
    r = lax.broadcasted_iota(jnp.int32, (2 * rows, cols), 0)
    r = jnp.where(r >= rows, r - rows, r)
    c = lax.broadcasted_iota(jnp.int32, (2 * rows, cols), 1)
    return c < r


def _any_live(carries):
    m = carries[0].max()
    for c in carries[1:]:
        m = jnp.maximum(m, c.max())
    return (m >= SB_STOP).astype(jnp.int32)


def _sb_prompt_kernel(q_ref, kt_ref, vt_ref, o_ref):
    npg, s_len = q_ref.shape[0], q_ref.shape[1]
    upper_ones = _upper_tri_ones(TK)
    diag_mask = _causal_mask(TQ, TK)
    all_true = diag_mask | True

    units = [(g, t) for t in range(SB_TILES) for g in range(npg)]

    def blocks(qms, blks, carries, accs, masks):
        cols = [pl.ds(pl.multiple_of(b * TK, TK), TK) for b in blks]
        rows = [slice(g * LANES, (g + 1) * LANES) for g, _ in units]
        zs = [_dot(qms[u], kt_ref[0, rows[u], cols[u]]) for u in range(len(units))]
        ws, carries = _sb_weights_multi(zs, upper_ones, carries, masks)
        accs = [accs[u] + _pv_feature_major(ws[u], vt_ref[0, rows[u], cols[u]]) for u in range(len(units))]
        return carries, accs

    def tile_group(i2, c0):
        first = SB_TILES * i2
        qms = [_stack_heads(q_ref[g, pl.ds(pl.multiple_of((first + t) * TQ, TQ), TQ), :]) for g, t in units]
        carries, accs = blocks(qms, [first + t for _, t in units],
                               [jnp.zeros((2 * TQ, TK), F32)] * len(units),
                               [jnp.zeros((TQ, LANES), F32)] * len(units), [diag_mask] * len(units))

        def cond(st):
            return jnp.logical_and(first + SB_TILES - 1 - st[0] >= 0, st[1] > 0)

        def body(st):
            k = st[0]
            blks = [first + t - k for _, t in units]
            masks = [None if t == SB_TILES - 1 else jnp.logical_and(all_true, blks[u] >= 0)
                     for u, (_, t) in enumerate(units)]
            cs, as_ = blocks(qms, [jnp.maximum(b, 0) for b in blks], list(st[2]), list(st[3]), masks)
            return (k + 1, _any_live(cs), tuple(cs), tuple(as_))

        st = lax.while_loop(cond, body, (1, _any_live(carries), tuple(carries), tuple(accs)))
        for u, (g, t) in enumerate(units):
            row0 = pl.multiple_of((first + t) * TQ, TQ)
            o_ref[g, pl.ds(row0, TQ), :] = st[3][u].astype(BF16)
        return c0

    lax.fori_loop(0, s_len // (SB_TILES * TQ), tile_group, 0)


def _sb_prompt(q_pm, ktb, vtb, n_batch, s_len):
    ng = NP // SB_PG
    return pl.pallas_call(
        _sb_prompt_kernel,
        grid=(n_batch, ng),
        in_specs=[
            pl.BlockSpec((SB_PG, s_len, LANES), lambda b, g: (g, b, 0)),
            pl.BlockSpec((1, SB_PG * LANES, s_len), lambda b, g: (b, g, 0)),
            pl.BlockSpec((1, SB_PG * LANES, s_len), lambda b, g: (b, g, 0)),
        ],
        out_specs=pl.BlockSpec((SB_PG, s_len, LANES), lambda b, g: (g, b, 0)),
        out_shape=jax.ShapeDtypeStruct((NP, n_batch * s_len, LANES), BF16),
        compiler_params=_cparams(("arbitrary", "arbitrary")),
        name="stick_breaking_prompt",
    )(q_pm, ktb, vtb)


def _sb_sample_kernel(q_ref, kv_ref, kc_hbm, vc_hbm, o_ref, kbuf, vbuf, sem):
    s = pl.program_id(0)
    ts = q_ref.shape[1]
    nblk = kc_hbm.shape[2] // TK
    upper_ones = _upper_tri_ones(TK)

    def copies(j):
        slot = j % 2
        cols = pl.ds(pl.multiple_of(j * TK, TK), TK)
        return (pltpu.make_async_copy(kc_hbm.at[s, :, cols], kbuf.at[slot], sem.at[0, slot]),
                pltpu.make_async_copy(vc_hbm.at[s, :, cols], vbuf.at[slot], sem.at[1, slot]))

    def start(j):
        for c in copies(j):
            c.start()

    def wait(j):
        for c in copies(j):
            c.wait()

    start(nblk - 1)

    qms, carries, accs = [], [], []
    mask = _causal_mask(ts, ts)
    upper_n = _upper_tri(ts)
    for p in range(NP):
        qm = _stack_heads(q_ref[p])
        ab, c = _sb_weights(_dot_nt(qm, kv_ref[p]), upper_n, jnp.zeros((2 * ts, 1), F32), mask)
        qms.append(qm)
        carries.append(jnp.broadcast_to(c, (2 * ts, TK)))
        accs.append(_pv_row_major(ab, kv_ref[NP + p]))

    def cond(st):
        return jnp.logical_and(st[0] >= 0, st[1] > 0)

    def body(st):
        j = st[0]
        wait(j)

        @pl.when(j > 0)
        def _():
            start(j - 1)

        slot = j % 2
        rows = [slice(p * LANES, (p + 1) * LANES) for p in range(NP)]
        zs = [_dot(qms[p], kbuf[slot, rows[p], :].astype(BF16)) for p in range(NP)]
        ws, cs = _sb_weights_multi(zs, upper_ones, list(st[2]), [None] * NP)
        as_ = [st[3][p] + _pv_feature_major(ws[p], vbuf[slot, rows[p], :].astype(BF16)) for p in range(NP)]
        return (j - 1, _any_live(cs), tuple(cs), tuple(as_))

    st = lax.while_loop(cond, body, (nblk - 1, _any_live(carries), tuple(carries), tuple(accs)))

    @pl.when(st[0] >= 0)
    def _():
        wait(st[0])

    for p in range(NP):
        o_ref[p] = st[3][p].astype(BF16)


def _sb_sample(q_s, kv_s, cache_kt, cache_vt, n_streams, ts):
    d = cache_kt.shape[1]
    return pl.pallas_call(
        _sb_sample_kernel,
        grid=(n_streams,),
        in_specs=[
            pl.BlockSpec((NP, ts, LANES), lambda s: (0, s, 0)),
            pl.BlockSpec((2 * NP, ts, LANES), lambda s: (0, s, 0)),
            pl.BlockSpec(memory_space=pl.ANY),
            pl.BlockSpec(memory_space=pl.ANY),
        ],
        out_specs=pl.BlockSpec((NP, ts, LANES), lambda s: (0, s, 0)),
        out_shape=jax.ShapeDtypeStruct(q_s.shape, BF16),
        scratch_shapes=[pltpu.VMEM((2, d, TK), F32), pltpu.VMEM((2, d, TK), F32),
                        pltpu.SemaphoreType.DMA((2, 2))],
        compiler_params=_cparams(("arbitrary",)),
        name="stick_breaking_sample",
    )(q_s, kv_s, cache_kt, cache_vt)


def _pick(is_sample, prompt_ref, sample_ref):
    return jnp.where(is_sample, sample_ref[...], prompt_ref[...])


def _attn_out(x, o, wo_ref, g1_ref, n2_ref, sh_ref, sc_ref):
    rows = x.shape[0]
    attn = jnp.concatenate([o[p] for p in range(NP)], axis=1)
    x1 = x + _group_rows(g1_ref[...], rows) * _dot(attn, wo_ref[...])
    h2 = _mod_norm(x1, n2_ref[...], sh_ref[...], sc_ref[...])
    return x1, h2


def _post_dense_kernel(xp_ref, xs_ref, op_ref, os_ref, wo_ref, g1_ref, n2_ref, sh_ref, sc_ref, g2_ref,
                       wg_ref, wu_ref, wd_ref, out_ref, x1_sc, h2_sc, acc_sc):
    j = pl.program_id(1)

    @pl.when(j == 0)
    def _():
        is_sample = pl.program_id(0) == pl.num_programs(0) - 1
        x1, h2 = _attn_out(_pick(is_sample, xp_ref, xs_ref), _pick(is_sample, op_ref, os_ref),
                           wo_ref, g1_ref, n2_ref, sh_ref, sc_ref)
        x1_sc[...] = x1
        h2_sc[...] = h2.astype(BF16)
        acc_sc[...] = jnp.zeros_like(acc_sc)

    h2 = h2_sc[...]
    act = (_silu(_dot(h2, wg_ref[...])) * _dot(h2, wu_ref[...])).astype(BF16)
    acc_sc[...] += _dot(act, wd_ref[...])

    @pl.when(j == pl.num_programs(1) - 1)
    def _():
        out_ref[...] = x1_sc[...] + _group_rows(g2_ref[...], x1_sc.shape[0]) * acc_sc[...]


def _post_dense(x_p, x_s, o_p, o_s, wo, g1_g, n2, sh_g, sc_g, g2_g, wg, wu, wd):
    tp, d = x_p.shape
    ntp = tp // TM
    f = wg.shape[1]
    gpt = TM // GROUP
    row = lambda i, j: (i, 0)
    prow = lambda i, j: (jnp.minimum(i, ntp - 1), 0)
    const = lambda i, j: (0, 0)
    return pl.pallas_call(
        _post_dense_kernel,
        grid=(ntp + 1, f // FC),
        in_specs=[
            pl.BlockSpec((TM, d), prow),
            pl.BlockSpec((TM, d), const),
            pl.BlockSpec((NP, TM, LANES), lambda i, j: (0, jnp.minimum(i, ntp - 1), 0)),
            pl.BlockSpec((NP, TM, LANES), lambda i, j: (0, 0, 0)),
            pl.BlockSpec((d, d), const),
            pl.BlockSpec((gpt, d), row),
            pl.BlockSpec((1, d), const),
            pl.BlockSpec((gpt, d), row),
            pl.BlockSpec((gpt, d), row),
            pl.BlockSpec((gpt, d), row),
            pl.BlockSpec((d, FC), lambda i, j: (0, j)),
            pl.BlockSpec((d, FC), lambda i, j: (0, j)),
            pl.BlockSpec((FC, d), lambda i, j: (j, 0)),
        ],
        out_specs=pl.BlockSpec((TM, d), row),
        out_shape=jax.ShapeDtypeStruct((tp + TM, d), F32),
        scratch_shapes=[pltpu.VMEM((TM, d), F32), pltpu.VMEM((TM, d), BF16), pltpu.VMEM((TM, d), F32)],
        compiler_params=_cparams(("arbitrary", "arbitrary")),
        name="attn_out_dense_ffn",
    )(x_p, x_s, o_p, o_s, wo, g1_g, n2, sh_g, sc_g, g2_g, wg, wu, wd)


def _post_router_kernel(x_ref, op_ref, os_ref, wo_ref, g1_ref, n2_ref, sh_ref, sc_ref, wr_ref,
                        x1_ref, h2_ref, route_ref, cnt_ref, cnt_sc):
    i = pl.program_id(0)
    rows = x_ref.shape[0]

    @pl.when(i == 0)
    def _():
        cnt_sc[...] = jnp.zeros_like(cnt_sc)

    is_sample = i == pl.num_programs(0) - 1
    x1, h2 = _attn_out(x_ref[...], _pick(is_sample, op_ref, os_ref), wo_ref, g1_ref, n2_ref, sh_ref, sc_ref)
    x1_ref[...] = x1
    h2_ref[...] = h2

    hh, hl = _split_hi_lo(h2)
    wh, wl = _split_hi_lo(wr_ref[...])
    logits = _dot(hh, wh) + _dot(hl, wh) + _dot(hh, wl)
    lane = lax.broadcasted_iota(jnp.int32, logits.shape, 1).astype(F32)
    logits = jnp.where(lane < N_EXPERTS, logits, -jnp.inf)
    v1 = logits.max(axis=1, keepdims=True)
    i1 = jnp.where(logits == v1, lane, float(LANES)).min(axis=1, keepdims=True)
    rest = jnp.where(lane == i1, -jnp.inf, logits)
    v2 = rest.max(axis=1, keepdims=True)
    i2 = jnp.where(rest == v2, lane, float(LANES)).min(axis=1, keepdims=True)
    w1 = 1.0 / (1.0 + jnp.exp(v2 - v1))
    w2 = 1.0 - w1

    oh1 = (lane == i1).astype(F32)
    oh2 = (lane == i2).astype(F32)
    cnt = oh1 + oh2
    lower = (lax.broadcasted_iota(jnp.int32, (rows, rows), 0)
             > lax.broadcasted_iota(jnp.int32, (rows, rows), 1)).astype(BF16)
    rank = _dot(lower, cnt.astype(BF16)) + cnt_sc[...]
    r1 = (oh1 * rank).sum(axis=1, keepdims=True)
    r2 = (oh2 * rank).sum(axis=1, keepdims=True)
    cnt_sc[...] += cnt.sum(axis=0, keepdims=True)
    cnt_ref[...] = cnt_sc[...]

    out = jnp.where(lane == 0, i1, 0.0)
    out = jnp.where(lane == 1, i2, out)
    out = jnp.where(lane == 2, w1, out)
    out = jnp.where(lane == 3, w2, out)
    out = jnp.where(lane == 4, r1, out)
    out = jnp.where(lane == 5, r2, out)
    route_ref[...] = out[:, :route_ref.shape[1]]


def _post_router(x_all, o_p, o_s, wo, g1_g, n2, sh_g, sc_g, wr_pad):
    t, d = x_all.shape
    ntp = t // TM - 1
    gpt = TM // GROUP
    row = lambda i: (i, 0)
    const = lambda i: (0, 0)
    return pl.pallas_call(
        _post_router_kernel,
        grid=(t // TM,),
        in_specs=[
            pl.BlockSpec((TM, d), row),
            pl.BlockSpec((NP, TM, LANES), lambda i: (0, jnp.minimum(i, ntp - 1), 0)),
            pl.BlockSpec((NP, TM, LANES), lambda i: (0, 0, 0)),
            pl.BlockSpec((d, d), const),
            pl.BlockSpec((gpt, d), row),
            pl.BlockSpec((1, d), const),
            pl.BlockSpec((gpt, d), row),
            pl.BlockSpec((gpt, d), row),
            pl.BlockSpec((d, LANES), const),
        ],
        out_specs=[
            pl.BlockSpec((TM, d), row),
            pl.BlockSpec((TM, d), row),
            pl.BlockSpec((TM, 8), row),
            pl.BlockSpec((1, LANES), const),
        ],
        out_shape=[
            jax.ShapeDtypeStruct((t, d), F32),
            jax.ShapeDtypeStruct((t, d), F32),
            jax.ShapeDtypeStruct((t, 8), F32),
            jax.ShapeDtypeStruct((1, LANES), F32),
        ],
        scratch_shapes=[pltpu.VMEM((1, LANES), F32)],
        compiler_params=_cparams(("arbitrary",)),
        name="attn_out_router",
    )(x_all, o_p, o_s, wo, g1_g, n2, sh_g, sc_g, wr_pad)


def _dispatch_kernel(pos_ref, h2_ref, init_ref, xs_ref, sem):
    del init_ref
    rows = h2_ref.shape[0]

    def issue(r, c):
        src = h2_ref.at[pl.ds(r, 1)]
        pltpu.make_async_copy(src, xs_ref.at[pl.ds(pos_ref[0, 0, r], 1)], sem).start()
        pltpu.make_async_copy(src, xs_ref.at[pl.ds(pos_ref[0, 0, rows + r], 1)], sem).start()
        return c

    lax.fori_loop(0, rows, issue, 0, unroll=8)
    for _ in range(2):
        pltpu.make_async_copy(h2_ref, xs_ref.at[pl.ds(0, rows)], sem).wait()


def _dispatch(pos_tiles, h2, xs_init):
    t, d = h2.shape
    return pl.pallas_call(
        _dispatch_kernel,
        grid=(t // TM,),
        in_specs=[
            pl.BlockSpec((1, 1, 2 * TM), lambda i: (i, 0, 0), memory_space=pltpu.SMEM),
            pl.BlockSpec((TM, d), lambda i: (i, 0)),
            pl.BlockSpec(memory_space=pl.ANY),
        ],
        out_specs=pl.BlockSpec(memory_space=pl.ANY),
        out_shape=jax.ShapeDtypeStruct(xs_init.shape, xs_init.dtype),
        scratch_shapes=[pltpu.SemaphoreType.DMA],
        input_output_aliases={2: 0},
        compiler_params=_cparams(("arbitrary",)),
        name="moe_dispatch",
    )(pos_tiles, h2, xs_init)


def _expert_ffn_kernel(te_ref, tv_ref, xs_ref, wg_ref, wu_ref, wd_ref, y_ref, xb_sc, acc_sc):
    r = pl.program_id(0)
    j = pl.program_id(1)
    valid = tv_ref[r] > 0

    @pl.when(j == 0)
    def _():
        xb_sc[...] = xs_ref[...].astype(BF16)
        acc_sc[...] = jnp.zeros_like(acc_sc)

    @pl.when(valid)
    def _():
        x = xb_sc[...]
        act = (_silu(_dot(x, wg_ref[0])) * _dot(x, wu_ref[0])).astype(BF16)
        acc_sc[...] += _dot(act, wd_ref[0])

    @pl.when(j == pl.num_programs(1) - 1)
    def _():
        y_ref[...] = acc_sc[...]


def _expert_ffn(tile_expert, tile_valid, xs, wg, wu, wd):
    r_pad, d = xs.shape
    f = wg.shape[2]
    grid_spec = pltpu.PrefetchScalarGridSpec(
        num_scalar_prefetch=2,
        grid=(r_pad // TR, f // FC),
        in_specs=[
            pl.BlockSpec((TR, d), lambda r, j, te, tv: (r, 0)),
            pl.BlockSpec((1, d, FC), lambda r, j, te, tv: (te[r], 0, j)),
            pl.BlockSpec((1, d, FC), lambda r, j, te, tv: (te[r], 0, j)),
            pl.BlockSpec((1, FC, d), lambda r, j, te, tv: (te[r], j, 0)),
        ],
        out_specs=pl.BlockSpec((TR, d), lambda r, j, te, tv: (r, 0)),
        scratch_shapes=[pltpu.VMEM((TR, d), BF16), pltpu.VMEM((TR, d), F32)],
    )
    return pl.pallas_call(
        _expert_ffn_kernel,
        grid_spec=grid_spec,
        out_shape=jax.ShapeDtypeStruct((r_pad, d), F32),
        compiler_params=_cparams(("arbitrary", "arbitrary")),
        name="moe_expert_ffn",
    )(tile_expert, tile_valid, xs, wg, wu, wd)


def _combine_kernel(pos_ref, x1_ref, g2_ref, route_ref, y_ref, outp_ref, outs_ref, buf, sem, *, ntp):
    i = pl.program_id(0)
    rows = x1_ref.shape[0]

    def issue(r, c):
        pltpu.make_async_copy(y_ref.at[pl.ds(pos_ref[0, 0, r], 1)], buf.at[0, pl.ds(r, 1)], sem).start()
        pltpu.make_async_copy(y_ref.at[pl.ds(pos_ref[0, 0, rows + r], 1)], buf.at[1, pl.ds(r, 1)], sem).start()
        return c

    lax.fori_loop(0, rows, issue, 0, unroll=8)
    for k in range(2):
        pltpu.make_async_copy(y_ref.at[pl.ds(0, rows)], buf.at[k], sem).wait()
    route = route_ref[...]
    moe = route[:, 2:3] * buf[0] + route[:, 3:4] * buf[1]
    res = x1_ref[...] + _group_rows(g2_ref[...], rows) * moe

    @pl.when(i < ntp)
    def _():
        outp_ref[...] = res

    @pl.when(i >= ntp)
    def _():
        outs_ref[...] = res


def _combine(pos_tiles, x1, g2_g, route, y, tp):
    t, d = x1.shape
    ntp = tp // TMC
    gpt = TMC // GROUP
    row = lambda i: (i, 0)
    return pl.pallas_call(
        functools.partial(_combine_kernel, ntp=ntp),
        grid=(t // TMC,),
        in_specs=[
            pl.BlockSpec((1, 1, 2 * TMC), lambda i: (i, 0, 0), memory_space=pltpu.SMEM),
            pl.BlockSpec((TMC, d), row),
            pl.BlockSpec((gpt, d), row),
            pl.BlockSpec((TMC, 8), row),
            pl.BlockSpec(memory_space=pl.ANY),
        ],
        out_specs=[
            pl.BlockSpec((TMC, d), lambda i: (jnp.minimum(i, ntp - 1), 0)),
            pl.BlockSpec((TMC, d), lambda i: (jnp.maximum(i - ntp, 0), 0)),
        ],
        out_shape=[jax.ShapeDtypeStruct((tp, d), F32), jax.ShapeDtypeStruct((t - tp, d), F32)],
        scratch_shapes=[pltpu.VMEM((2, TMC, d), F32), pltpu.SemaphoreType.DMA],
        compiler_params=_cparams(("arbitrary",)),
        name="moe_combine",
    )(pos_tiles, x1, g2_g, route, y)


def _band_valid(q_pos, k_pos):
    qc = q_pos // CHUNK
    kc = k_pos // CHUNK
    return (k_pos[None, :] >= 0) & (kc[None, :] <= qc[:, None]) & (kc[None, :] >= qc[:, None] - N_PAST_CHUNKS)


def _band_bias(rel_bias, q_pos, k_pos):
    nq, nk = len(q_pos), len(k_pos)
    assert np.all(np.diff(q_pos) == 1) and np.all(np.diff(k_pos) == 1)
    period = nq + nk
    dist = (q_pos[0] - k_pos[0]) + (nq - 1) - np.arange(period)
    vec = rel_bias.astype(F32)[:, np.clip(dist, -REL_CLIP, REL_CLIP) + REL_CLIP]
    flat = jnp.tile(jnp.roll(vec, -(nq - 1), axis=1), (1, nq))[:, :nq * (period - 1)]
    tab = flat.reshape(-1, nq, period - 1)[:, :, :nk]
    return jnp.where(jnp.asarray(_band_valid(q_pos, k_pos))[None], tab, NEG_INF)


def _tile_pos(pos2, tile):
    t = pos2.shape[1]
    return pos2.reshape(2, t // tile, tile).transpose(1, 0, 2).reshape(t // tile, 1, 2 * tile)


def _feature_major(cache):
    n, length = cache.shape[0], cache.shape[1]
    return cache.transpose(0, 2, 3, 1).reshape(n, N_HEADS * HEAD_DIM, length)


def _token_major(xt):
    n, _, length = xt.shape
    return xt.reshape(n, N_HEADS, HEAD_DIM, length).transpose(0, 3, 1, 2)


def kernel(x_prompt, x_sample, cache_a_k, cache_a_v, cache_b_k, cache_b_v, c_prompt, c_sample, w_qkv, w_o, norm1_g, norm2_g, w_ada, b_ada, q_norm_g, k_norm_g, rel_bias, w_gate_d, w_up_d, w_down_d, w_router, w_gate_e, w_up_e, w_down_e):
    bp, sp, d = x_prompt.shape
    bs, ts, _ = x_sample.shape
    past_len = cache_b_k.shape[2]
    win = cache_a_k.shape[2]
    depth = w_qkv.shape[0]
    assert depth == 2 and d == N_HEADS * HEAD_DIM and ts == GROUP
    tp, tsm = bp * sp, bs * ts
    t_all = tp + tsm
    assert sp % TM == 0 and tsm == TM and sp % (BAND_UNROLL * TQ) == 0 and t_all % TMC == 0 and tp % TMC == 0
    assert win == BAND_PAST == TM and sp >= BAND_KEYS and past_len % TK == 0 and past_len >= win

    x_p = x_prompt.reshape(tp, d)
    x_s = x_sample.reshape(tsm, d)

    mod = _ada(jnp.concatenate([c_prompt, c_sample], axis=0), w_ada, b_ada)

    def groups(v):
        return jnp.concatenate([jnp.repeat(v[:bp], sp // GROUP, axis=0), v[bp:]], axis=0)

    bd = (np.arange(256)[:, None] // HEAD_DIM == np.arange(256)[None, :] // HEAD_DIM)
    bd = jnp.asarray(bd.astype(np.float32) / HEAD_DIM, BF16)
    ones_row = jnp.ones((1, d), F32)
    ones_col = jnp.ones((d, 1), F32)
    gps = tp // GROUP

    for i in range(depth):
        sh1, sc1, g1, sh2, sc2, g2 = [groups(m) for m in jnp.split(mod[i], 6, axis=-1)]
        band = i % 2 == 0
        wq = w_qkv[i][:, :d].astype(BF16)
        wkvt = w_qkv[i][:, d:].T.astype(BF16)
        qg = jnp.tile(q_norm_g[0], N_HEADS)[None] if band else ones_row
        kg = jnp.tile(k_norm_g[0], N_HEADS)[:, None] if band else ones_col
        n1 = norm1_g[i][None]
        if band:
            q_p, ktb, vtb, ktf, vtf = _qkv_prompt(x_p, n1, sh1, sc1, wq, wkvt, qg, kg, bd, bp, sp, True)
            q_s, kv_s, ksf, vsf = _qkv_sample(x_s, 0, n1, sh1[gps:], sc1[gps:], wq, wkvt, qg, kg, bd, True)
            q_pos = np.arange(TQ) + BAND_PAST
            bias_p = _band_bias(rel_bias[0], q_pos, np.arange(BAND_KEYS))
            bias_p = jnp.pad(bias_p, ((0, 0), (0, 0), (0, BAND_PAST)), constant_values=NEG_INF)
            o_p = _band_prompt(q_p, ktb, vtb, bias_p, bp, sp)
            pos_s = past_len + np.arange(ts)
            pos_win = past_len - win + np.arange(win)
            ca_k, ca_v = _feature_major(cache_a_k[0]), _feature_major(cache_a_v[0])
            o_s = _band_sample(q_s, kv_s, ca_k, ca_v, _band_bias(rel_bias[0], pos_s, pos_win),
                               _band_bias(rel_bias[0], pos_s, pos_s), bs, ts)
            a_k_p = _token_major(ktf)[None]
            a_v_p = _token_major(vtf)[None]
            new_k = ksf.reshape(d, bs, ts).transpose(1, 0, 2)
            new_v = vsf.reshape(d, bs, ts).transpose(1, 0, 2)
            a_k_s = _token_major(jnp.concatenate([ca_k[:, :, ts:], new_k], axis=2))[None]
            a_v_s = _token_major(jnp.concatenate([ca_v[:, :, ts:], new_v], axis=2))[None]
            x_all = _post_dense(x_p, x_s, o_p, o_s, w_o[i].astype(BF16), g1, norm2_g[i][None], sh2, sc2, g2,
                                w_gate_d[0].astype(BF16), w_up_d[0].astype(BF16), w_down_d[0].astype(BF16))
        else:
            q_p, ktb, vtb, ktf, vtf = _qkv_prompt(x_all, n1, sh1, sc1, wq, wkvt, qg, kg, bd, bp, sp, False)
            q_s, kv_s, ksf, vsf = _qkv_sample(x_all, tp // TM, n1, sh1[gps:], sc1[gps:], wq, wkvt, qg, kg, bd,
                                              False)
            o_p = _sb_prompt(q_p, ktb, vtb, bp, sp)
            o_s = _sb_sample(q_s, kv_s, _feature_major(cache_b_k[0]), _feature_major(cache_b_v[0]), bs, ts)
            b_k_p = _token_major(ktf)[None]
            b_v_p = _token_major(vtf)[None]
            b_k_s = ksf.reshape(1, bs, ts, N_HEADS, HEAD_DIM)
            b_v_s = vsf.reshape(1, bs, ts, N_HEADS, HEAD_DIM)
            wr_pad = jnp.pad(w_router[0], ((0, 0), (0, LANES - N_EXPERTS)))
            x1, h2, route, cnt = _post_router(x_all, o_p, o_s, w_o[i].astype(BF16), g1, norm2_g[i][None],
                                              sh2, sc2, wr_pad)
            counts = cnt[0, :N_EXPERTS].astype(jnp.int32)
            sizes = (counts + TR - 1) // TR * TR
            ends = jnp.cumsum(sizes)
            offs = ends - sizes
            e12 = route[:, 0:2].astype(jnp.int32)
            off12 = jnp.sum(jnp.where(e12[:, :, None] == jnp.arange(N_EXPERTS), offs, 0), axis=-1)
            pos2 = (off12 + route[:, 4:6].astype(jnp.int32)).T
            r_pad = (2 * t_all // TR + N_EXPERTS) * TR
            tile_start = jnp.arange(r_pad // TR, dtype=jnp.int32) * TR
            tile_expert = jnp.minimum(jnp.sum(tile_start[:, None] >= ends[None, :], axis=1),
                                      N_EXPERTS - 1).astype(jnp.int32)
            tile_valid = (tile_start < ends[-1]).astype(jnp.int32)
            xs = _dispatch(_tile_pos(pos2, TM), h2, jnp.zeros((r_pad, d), F32))
            y = _expert_ffn(tile_expert, tile_valid, xs, w_gate_e[0].astype(BF16),
                            w_up_e[0].astype(BF16), w_down_e[0].astype(BF16))
            y_p, y_s = _combine(_tile_pos(pos2, TMC), x1, g2, route, y, tp)

    return (y_p.reshape(bp, sp, d), y_s.reshape(bs, ts, d), a_k_p, a_v_p, a_k_s, a_v_s,
            b_k_p, b_v_p, b_k_s, b_v_s)
```

```python
import functools

import numpy as np
import jax
import jax.numpy as jnp
from jax import lax
from jax.experimental import pallas as pl
from jax.experimental.pallas import tpu as pltpu

F32 = jnp.float32
BF16 = jnp.bfloat16

CHUNK = 64
N_HEADS = 16
HEAD_DIM = 64
N_PAST_CHUNKS = 8
BAND_PAST = N_PAST_CHUNKS * CHUNK
REL_CLIP = 128
N_EXPERTS = 8
EPS = 1e-6
NEG_INF = -1e30

LANES = 128
VMEM_LIMIT = 56 * 1024 * 1024

GROUP = 32
TM = 512
FC = 1792
TQ = 128
TK = 128
NP = N_HEADS // 2
BAND_KEYS = BAND_PAST + TQ
BAND_UNROLL = 8
BAND_ROWS = 32
SB_PG = 2
SB_TILES = 4
SB_STOP = -90.0
TR = 512
TMC = 256


def _cparams(sem):
    return pltpu.CompilerParams(dimension_semantics=sem, vmem_limit_bytes=VMEM_LIMIT)


def _dot(a, b):
    return jnp.dot(a, b, preferred_element_type=F32)


def _dot_nt(a, b):
    return lax.dot_general(a, b, (((1,), (1,)), ((), ())), preferred_element_type=F32)


def _split_hi_lo(x):
    hi = x.astype(BF16)
    lo = (x - hi.astype(F32)).astype(BF16)
    return hi, lo


def _silu(x):
    return x * (1.0 / (1.0 + jnp.exp(-x)))


def _lane_lt64(shape):
    return lax.broadcasted_iota(jnp.int32, shape, len(shape) - 1) < HEAD_DIM


def _group_rows(vec_rows, rows):
    g, d = vec_rows.shape
    return jnp.broadcast_to(vec_rows[:, None, :], (g, rows // g, d)).reshape(rows, d)


def _mod_norm(x, g, shift_g, scale_g):
    rows = x.shape[0]
    ms = jnp.mean(x * x, axis=-1, keepdims=True)
    y = x * lax.rsqrt(ms + EPS) * g
    return y * (1.0 + _group_rows(scale_g, rows)) + _group_rows(shift_g, rows)


def _stack_heads(q2):
    m = _lane_lt64(q2.shape)
    z = jnp.zeros_like(q2)
    return jnp.concatenate([jnp.where(m, q2, z), jnp.where(m, z, q2)], axis=0)


def _pv_feature_major(pb, vt):
    r = pb.shape[0] // 2
    both = _dot_nt(pb, vt)
    return jnp.where(_lane_lt64((r, LANES)), both[:r], both[r:])


def _pv_row_major(pb, v2):
    r = pb.shape[0] // 2
    both = _dot(pb, v2)
    return jnp.where(_lane_lt64((r, LANES)), both[:r], both[r:])


def _ada_kernel(c_ref, w_ref, b_ref, o_ref):
    s = _silu(c_ref[...]).astype(BF16)
    o_ref[0] = _dot(s, w_ref[0].astype(BF16)) + b_ref[0]


def _ada(c_all, w_ada, b_ada):
    depth, d, n = w_ada.shape
    nb = c_all.shape[0]
    tn = 1536
    return pl.pallas_call(
        _ada_kernel,
        grid=(depth, n // tn),
        in_specs=[
            pl.BlockSpec((nb, d), lambda l, j: (0, 0)),
            pl.BlockSpec((1, d, tn), lambda l, j: (l, 0, j)),
            pl.BlockSpec((1, 1, tn), lambda l, j: (l, 0, j)),
        ],
        out_specs=pl.BlockSpec((1, nb, tn), lambda l, j: (l, 0, j)),
        out_shape=jax.ShapeDtypeStruct((depth, nb, n), F32),
        compiler_params=_cparams(("arbitrary", "arbitrary")),
        name="ada_modulation",
    )(c_all, w_ada, b_ada.reshape(depth, 1, n))


def _head_mean_sq_rows(x, bd_ref):
    hi, lo = _split_hi_lo(x * x)
    w = bd_ref.shape[0]
    cols = []
    for c in range(x.shape[1] // w):
        sl = slice(c * w, (c + 1) * w)
        cols.append(_dot(hi[:, sl], bd_ref[...]) + _dot(lo[:, sl], bd_ref[...]))
    return jnp.concatenate(cols, axis=1)


def _head_norm_cols(xt, gain_col):
    d, c = xt.shape
    x3 = xt.reshape(N_HEADS, HEAD_DIM, c)
    ms = jnp.mean(x3 * x3, axis=1, keepdims=True)
    return (x3 * lax.rsqrt(ms + EPS)).reshape(d, c) * gain_col


def _project(x, g_ref, sh_ref, sc_ref, wq_ref, wkvt_ref, qg_ref, kg_ref, bd_ref, qk_norm):
    d = x.shape[1]
    h = _mod_norm(x, g_ref[...], sh_ref[...], sc_ref[...]).astype(BF16)
    q = _dot(h, wq_ref[...])
    kvt = _dot_nt(wkvt_ref[...], h)
    kt, vt = kvt[:d], kvt[d:]
    if qk_norm:
        q = q * lax.rsqrt(_head_mean_sq_rows(q, bd_ref) + EPS) * qg_ref[...]
        kt = _head_norm_cols(kt, kg_ref[...])
    return (q * (HEAD_DIM ** -0.5)).astype(BF16), kt, vt


def _qkv_prompt_kernel(x_ref, g_ref, sh_ref, sc_ref, wq_ref, wkvt_ref, qg_ref, kg_ref, bd_ref,
                       q_ref, ktb_ref, vtb_ref, ktf_ref, vtf_ref, *, qk_norm, tail_only):
    qs, kt, vt = _project(x_ref[...], g_ref, sh_ref, sc_ref, wq_ref, wkvt_ref, qg_ref, kg_ref, bd_ref,
                          qk_norm)
    for p in range(NP):
        q_ref[p] = qs[:, p * LANES:(p + 1) * LANES]
    ktb_ref[0] = kt.astype(BF16)
    vtb_ref[0] = vt.astype(BF16)
    if tail_only:
        @pl.when((pl.program_id(0) + 1) % tail_only == 0)
        def _():
            ktf_ref[0] = kt
            vtf_ref[0] = vt
    else:
        ktf_ref[0] = kt
        vtf_ref[0] = vt


def _qkv_prompt(x, g, shift_g, scale_g, wq, wkvt, qg_row, kg_col, bd, n_batch, s_len, qk_norm):
    d = x.shape[1]
    tpb = s_len // TM
    gpt = TM // GROUP
    row = lambda i: (i, 0)
    const = lambda i: (0, 0)
    fmaj = lambda i: (i // tpb, 0, i % tpb)
    if qk_norm:
        f_spec = pl.BlockSpec((1, d, TM), lambda i: (i // tpb, 0, 0))
        f_shape = jax.ShapeDtypeStruct((n_batch, d, TM), F32)
    else:
        f_spec = pl.BlockSpec((1, d, TM), fmaj)
        f_shape = jax.ShapeDtypeStruct((n_batch, d, s_len), F32)
    return pl.pallas_call(
        functools.partial(_qkv_prompt_kernel, qk_norm=qk_norm, tail_only=tpb if qk_norm else 0),
        grid=(n_batch * tpb,),
        in_specs=[
            pl.BlockSpec((TM, d), row),
            pl.BlockSpec((1, d), const),
            pl.BlockSpec((gpt, d), row),
            pl.BlockSpec((gpt, d), row),
            pl.BlockSpec((d, d), const),
            pl.BlockSpec((2 * d, d), const),
            pl.BlockSpec((1, d), const),
            pl.BlockSpec((d, 1), const),
            pl.BlockSpec(bd.shape, const),
        ],
        out_specs=[
            pl.BlockSpec((NP, TM, LANES), lambda i: (0, i, 0)),
            pl.BlockSpec((1, d, TM), fmaj),
            pl.BlockSpec((1, d, TM), fmaj),
            f_spec, f_spec,
        ],
        out_shape=[
            jax.ShapeDtypeStruct((NP, n_batch * s_len, LANES), BF16),
            jax.ShapeDtypeStruct((n_batch, d, s_len), BF16),
            jax.ShapeDtypeStruct((n_batch, d, s_len), BF16),
            f_shape, f_shape,
        ],
        compiler_params=_cparams(("arbitrary",)),
        name="qkv_prompt_qknorm" if qk_norm else "qkv_prompt",
    )(x, g, shift_g, scale_g, wq, wkvt, qg_row, kg_col, bd)


def _qkv_sample_kernel(x_ref, g_ref, sh_ref, sc_ref, wq_ref, wkvt_ref, qg_ref, kg_ref, bd_ref,
                       q_ref, kv_ref, kf_ref, vf_ref, *, qk_norm):
    qs, kt, vt = _project(x_ref[...], g_ref, sh_ref, sc_ref, wq_ref, wkvt_ref, qg_ref, kg_ref, bd_ref,
                          qk_norm)
    k, v = kt.T, vt.T
    if qk_norm:
        kf_ref[...] = kt
        vf_ref[...] = vt
    else:
        kf_ref[...] = k
        vf_ref[...] = v
    kb, vb = k.astype(BF16), v.astype(BF16)
    for p in range(NP):
        sl = slice(p * LANES, (p + 1) * LANES)
        q_ref[p] = qs[:, sl]
        kv_ref[p] = kb[:, sl]
        kv_ref[NP + p] = vb[:, sl]


def _qkv_sample(x, row_block, g, shift_g, scale_g, wq, wkvt, qg_row, kg_col, bd, qk_norm):
    d = x.shape[1]
    gpt = TM // GROUP
    row = lambda i: (row_block, 0)
    const = lambda i: (0, 0)
    f_shape = jax.ShapeDtypeStruct((d, TM) if qk_norm else (TM, d), F32)
    return pl.pallas_call(
        functools.partial(_qkv_sample_kernel, qk_norm=qk_norm),
        grid=(1,),
        in_specs=[
            pl.BlockSpec((TM, d), row),
            pl.BlockSpec((1, d), const),
            pl.BlockSpec((gpt, d), const),
            pl.BlockSpec((gpt, d), const),
            pl.BlockSpec((d, d), const),
            pl.BlockSpec((2 * d, d), const),
            pl.BlockSpec((1, d), const),
            pl.BlockSpec((d, 1), const),
            pl.BlockSpec(bd.shape, const),
        ],
        out_specs=[
            pl.BlockSpec((NP, TM, LANES), lambda i: (0, 0, 0)),
            pl.BlockSpec((2 * NP, TM, LANES), lambda i: (0, 0, 0)),
            pl.BlockSpec(f_shape.shape, const),
            pl.BlockSpec(f_shape.shape, const),
        ],
        out_shape=[
            jax.ShapeDtypeStruct((NP, TM, LANES), BF16),
            jax.ShapeDtypeStruct((2 * NP, TM, LANES), BF16),
            f_shape, f_shape,
        ],
        compiler_params=_cparams(("arbitrary",)),
        name="qkv_sample_qknorm" if qk_norm else "qkv_sample",
    )(x, g, shift_g, scale_g, wq, wkvt, qg_row, kg_col, bd)


def _band_prompt_kernel(q_ref, kt_ref, vt_ref, bias_ref, o_ref):
    s_len = q_ref.shape[1]
    back = BAND_PAST // TK

    def tiles(n, carry):
        idx = [n * BAND_UNROLL + u for u in range(BAND_UNROLL)]
        row0 = [pl.multiple_of(i * TQ, TQ) for i in idx]
        key0 = [pl.multiple_of(jnp.maximum(i - back, 0) * TK, TK) for i in idx]
        col0 = [pl.multiple_of(jnp.maximum(back - i, 0) * TK, TK) for i in idx]
        scores = [_dot(_stack_heads(q_ref[0, pl.ds(row0[u], TQ), :]), kt_ref[0, :, pl.ds(key0[u], BAND_KEYS)])
                  for u in range(BAND_UNROLL)]
        for u in range(BAND_UNROLL):
            ps, ls = [], []
            for h in range(2):
                for r0 in range(0, TQ, BAND_ROWS):
                    s = (scores[u][h * TQ + r0:h * TQ + r0 + BAND_ROWS]
                         + bias_ref[h, r0:r0 + BAND_ROWS, pl.ds(col0[u], BAND_KEYS)])
                    p = jnp.exp(s - s.max(axis=1, keepdims=True))
                    ls.append(p.sum(axis=1, keepdims=True))
                    ps.append(p.astype(BF16))
            l = jnp.concatenate(ls, axis=0)
            acc = _pv_feature_major(jnp.concatenate(ps, axis=0), vt_ref[0, :, pl.ds(key0[u], BAND_KEYS)])
            linv = jnp.where(_lane_lt64((TQ, LANES)), 1.0 / l[:TQ], 1.0 / l[TQ:])
            o_ref[0, pl.ds(row0[u], TQ), :] = (acc * linv).astype(BF16)
        return carry

    lax.fori_loop(0, s_len // (BAND_UNROLL * TQ), tiles, 0)


def _band_prompt(q_pm, ktb, vtb, bias_p, n_batch, s_len):
    return pl.pallas_call(
        _band_prompt_kernel,
        grid=(n_batch, NP),
        in_specs=[
            pl.BlockSpec((1, s_len, LANES), lambda b, p: (p, b, 0)),
            pl.BlockSpec((1, LANES, s_len), lambda b, p: (b, p, 0)),
            pl.BlockSpec((1, LANES, s_len), lambda b, p: (b, p, 0)),
            pl.BlockSpec((2, TQ, bias_p.shape[2]), lambda b, p: (p, 0, 0)),
        ],
        out_specs=pl.BlockSpec((1, s_len, LANES), lambda b, p: (p, b, 0)),
        out_shape=jax.ShapeDtypeStruct((NP, n_batch * s_len, LANES), BF16),
        compiler_params=_cparams(("arbitrary", "arbitrary")),
        name="band_attention_prompt",
    )(q_pm, ktb, vtb, bias_p)


def _band_sample_kernel(q_ref, kv_ref, kc_ref, vc_ref, bc_ref, bn_ref, o_ref):
    ts = q_ref.shape[1]
    for p in range(NP):
        rows = slice(p * LANES, (p + 1) * LANES)
        qm = _stack_heads(q_ref[p])
        kn, vn = kv_ref[p], kv_ref[NP + p]
        s1 = _dot(qm, kc_ref[0, rows, :].astype(BF16)) + bc_ref[2 * p:2 * p + 2].reshape(2 * ts, -1)
        s2 = _dot_nt(qm, kn) + bn_ref[2 * p:2 * p + 2].reshape(2 * ts, ts)
        m = jnp.maximum(s1.max(axis=1, keepdims=True), s2.max(axis=1, keepdims=True))
        p1 = jnp.exp(s1 - m)
        p2 = jnp.exp(s2 - m)
        l = p1.sum(axis=1, keepdims=True) + p2.sum(axis=1, keepdims=True)
        acc = (_pv_feature_major(p1.astype(BF16), vc_ref[0, rows, :].astype(BF16))
               + _pv_row_major(p2.astype(BF16), vn))
        linv = jnp.where(_lane_lt64((ts, LANES)), 1.0 / l[:ts], 1.0 / l[ts:])
        o_ref[p] = (acc * linv).astype(BF16)


def _band_sample(q_s, kv_s, cache_kt, cache_vt, bias_c, bias_n, n_streams, ts):
    d, win = cache_kt.shape[1], cache_kt.shape[2]
    return pl.pallas_call(
        _band_sample_kernel,
        grid=(n_streams,),
        in_specs=[
            pl.BlockSpec((NP, ts, LANES), lambda s: (0, s, 0)),
            pl.BlockSpec((2 * NP, ts, LANES), lambda s: (0, s, 0)),
            pl.BlockSpec((1, d, win), lambda s: (s, 0, 0)),
            pl.BlockSpec((1, d, win), lambda s: (s, 0, 0)),
            pl.BlockSpec(bias_c.shape, lambda s: (0, 0, 0)),
            pl.BlockSpec(bias_n.shape, lambda s: (0, 0, 0)),
        ],
        out_specs=pl.BlockSpec((NP, ts, LANES), lambda s: (0, s, 0)),
        out_shape=jax.ShapeDtypeStruct(q_s.shape, BF16),
        compiler_params=_cparams(("arbitrary",)),
        name="band_attention_sample",
    )(q_s, kv_s, cache_kt, cache_vt, bias_c, bias_n)


def _sb_weights(z, upper, carry, mask):
    sp = jnp.maximum(z, 0.0) + jnp.log(1.0 + jnp.exp(-jnp.abs(z)))
    lk = -sp
    if mask is not None:
        lk = jnp.where(mask, lk, 0.0)
    hi, lo = _split_hi_lo(lk)
    after = _dot(hi, upper) + _dot(lo, upper) + carry
    a = jnp.exp(z - sp + after)
    if mask is not None:
        a = jnp.where(mask, a, 0.0)
    return a.astype(BF16), carry + lk.sum(axis=1, keepdims=True)


def _sb_weights_multi(zs, upper_ones, carries, masks):
    c = zs[0].shape[1]
    halves = []
    for z, mask in zip(zs, masks):
        nz = -z
        lk = jnp.minimum(nz, 0.0) - jnp.log(1.0 + jnp.exp(jnp.minimum(z, nz)))
        if mask is not None:
            lk = jnp.where(mask, lk, 0.0)
        halves.append(jnp.concatenate(_split_hi_lo(lk), axis=1))
    sums = [_dot(hl, upper_ones) for hl in halves]
    weights, new_carries = [], []
    for z, sm, carry, mask in zip(zs, sums, carries, masks):
        a = jnp.exp(z + (sm[:, :c] + carry))
        if mask is not None:
            a = jnp.where(mask, a, 0.0)
        weights.append(a.astype(BF16))
        new_carries.append(carry + sm[:, c:])
    return weights, new_carries


def _upper_tri(n):
    return (lax.broadcasted_iota(jnp.int32, (n, n), 0)
            > lax.broadcasted_iota(jnp.int32, (n, n), 1)).astype(BF16)


def _upper_tri_ones(n):
    incl = (lax.broadcasted_iota(jnp.int32, (n, n), 0)
            >= lax.broadcasted_iota(jnp.int32, (n, n), 1)).astype(BF16)
    half = jnp.concatenate([incl, jnp.ones((n, n), BF16)], axis=1)
    return jnp.concatenate([half, half], axis=0)


def _causal_mask(rows, cols):
    r = lax.broadcasted_iota(jnp.int32, (2 * rows, cols), 0)
    r = jnp.where(r >= rows, r - rows, r)
    c = lax.broadcasted_iota(jnp.int32, (2 * rows, cols), 1)
    return c < r


def _any_live(carries):
    m = carries[0].max()
    for c in carries[1:]:
        m = jnp.maximum(m, c.max())
    return (m >= SB_STOP).astype(jnp.int32)


def _sb_prompt_kernel(q_ref, kt_ref, vt_ref, o_ref):
    npg, s_len = q_ref.shape[0], q_ref.shape[1]
    upper_ones = _upper_tri_ones(TK)
    diag_mask = _causal_mask(TQ, TK)
    all_true = diag_mask | True

    units = [(g, t) for t in range(SB_TILES) for g in range(npg)]

    def blocks(qms, blks, carries, accs, masks):
        cols = [pl.ds(pl.multiple_of(b * TK, TK), TK) for b in blks]
        rows = [slice(g * LANES, (g + 1) * LANES) for g, _ in units]
        zs = [_dot(qms[u], kt_ref[0, rows[u], cols[u]]) for u in range(len(units))]
        ws, carries = _sb_weights_multi(zs, upper_ones, carries, masks)
        accs = [accs[u] + _pv_feature_major(ws[u], vt_ref[0, rows[u], cols[u]]) for u in range(len(units))]
        return carries, accs

    def tile_group(i2, c0):
        first = SB_TILES * i2
        qms = [_stack_heads(q_ref[g, pl.ds(pl.multiple_of((first + t) * TQ, TQ), TQ), :]) for g, t in units]
        carries, accs = blocks(qms, [first + t for _, t in units],
                               [jnp.zeros((2 * TQ, TK), F32)] * len(units),
                               [jnp.zeros((TQ, LANES), F32)] * len(units), [diag_mask] * len(units))

        def cond(st):
            return jnp.logical_and(first + SB_TILES - 1 - st[0] >= 0, st[1] > 0)

        def body(st):
            k = st[0]
            blks = [first + t - k for _, t in units]
            masks = [None if t == SB_TILES - 1 else jnp.logical_and(all_true, blks[u] >= 0)
                     for u, (_, t) in enumerate(units)]
            cs, as_ = blocks(qms, [jnp.maximum(b, 0) for b in blks], list(st[2]), list(st[3]), masks)
            return (k + 1, _any_live(cs), tuple(cs), tuple(as_))

        st = lax.while_loop(cond, body, (1, _any_live(carries), tuple(carries), tuple(accs)))
        for u, (g, t) in enumerate(units):
            row0 = pl.multiple_of((first + t) * TQ, TQ)
            o_ref[g, pl.ds(row0, TQ), :] = st[3][u].astype(BF16)
        return c0

    lax.fori_loop(0, s_len // (SB_TILES * TQ), tile_group, 0)


def _sb_prompt(q_pm, ktb, vtb, n_batch, s_len):
    ng = NP // SB_PG
    return pl.pallas_call(
        _sb_prompt_kernel,
        grid=(n_batch, ng),
        in_specs=[
            pl.BlockSpec((SB_PG, s_len, LANES), lambda b, g: (g, b, 0)),
            pl.BlockSpec((1, SB_PG * LANES, s_len), lambda b, g: (b, g, 0)),
            pl.BlockSpec((1, SB_PG * LANES, s_len), lambda b, g: (b, g, 0)),
        ],
        out_specs=pl.BlockSpec((SB_PG, s_len, LANES), lambda b, g: (g, b, 0)),
        out_shape=jax.ShapeDtypeStruct((NP, n_batch * s_len, LANES), BF16),
        compiler_params=_cparams(("arbitrary", "arbitrary")),
        name="stick_breaking_prompt",
    )(q_pm, ktb, vtb)


def _sb_sample_kernel(q_ref, kv_ref, kc_hbm, vc_hbm, o_ref, kbuf, vbuf, sem):
    s = pl.program_id(0)
    ts = q_ref.shape[1]
    nblk = kc_hbm.shape[2] // TK
    upper_ones = _upper_tri_ones(TK)

    def copies(j):
        slot = j % 2
        cols = pl.ds(pl.multiple_of(j * TK, TK), TK)
        return (pltpu.make_async_copy(kc_hbm.at[s, :, cols], kbuf.at[slot], sem.at[0, slot]),
                pltpu.make_async_copy(vc_hbm.at[s, :, cols], vbuf.at[slot], sem.at[1, slot]))

    def start(j):
        for c in copies(j):
            c.start()

    def wait(j):
        for c in copies(j):
            c.wait()

    start(nblk - 1)

    qms, carries, accs = [], [], []
    mask = _causal_mask(ts, ts)
    upper_n = _upper_tri(ts)
    for p in range(NP):
        qm = _stack_heads(q_ref[p])
        ab, c = _sb_weights(_dot_nt(qm, kv_ref[p]), upper_n, jnp.zeros((2 * ts, 1), F32), mask)
        qms.append(qm)
        carries.append(jnp.broadcast_to(c, (2 * ts, TK)))
        accs.append(_pv_row_major(ab, kv_ref[NP + p]))

    def cond(st):
        return jnp.logical_and(st[0] >= 0, st[1] > 0)

    def body(st):
        j = st[0]
        wait(j)

        @pl.when(j > 0)
        def _():
            start(j - 1)

        slot = j % 2
        rows = [slice(p * LANES, (p + 1) * LANES) for p in range(NP)]
        zs = [_dot(qms[p], kbuf[slot, rows[p], :].astype(BF16)) for p in range(NP)]
        ws, cs = _sb_weights_multi(zs, upper_ones, list(st[2]), [None] * NP)
        as_ = [st[3][p] + _pv_feature_major(ws[p], vbuf[slot, rows[p], :].astype(BF16)) for p in range(NP)]
        return (j - 1, _any_live(cs), tuple(cs), tuple(as_))

    st = lax.while_loop(cond, body, (nblk - 1, _any_live(carries), tuple(carries), tuple(accs)))

    @pl.when(st[0] >= 0)
    def _():
        wait(st[0])

    for p in range(NP):
        o_ref[p] = st[3][p].astype(BF16)


def _sb_sample(q_s, kv_s, cache_kt, cache_vt, n_streams, ts):
    d = cache_kt.shape[1]
    return pl.pallas_call(
        _sb_sample_kernel,
        grid=(n_streams,),
        in_specs=[
            pl.BlockSpec((NP, ts, LANES), lambda s: (0, s, 0)),
            pl.BlockSpec((2 * NP, ts, LANES), lambda s: (0, s, 0)),
            pl.BlockSpec(memory_space=pl.ANY),
            pl.BlockSpec(memory_space=pl.ANY),
        ],
        out_specs=pl.BlockSpec((NP, ts, LANES), lambda s: (0, s, 0)),
        out_shape=jax.ShapeDtypeStruct(q_s.shape, BF16),
        scratch_shapes=[pltpu.VMEM((2, d, TK), F32), pltpu.VMEM((2, d, TK), F32),
                        pltpu.SemaphoreType.DMA((2, 2))],
        compiler_params=_cparams(("arbitrary",)),
        name="stick_breaking_sample",
    )(q_s, kv_s, cache_kt, cache_vt)


def _pick(is_sample, prompt_ref, sample_ref):
    return jnp.where(is_sample, sample_ref[...], prompt_ref[...])


def _attn_out(x, o, wo_ref, g1_ref, n2_ref, sh_ref, sc_ref):
    rows = x.shape[0]
    attn = jnp.concatenate([o[p] for p in range(NP)], axis=1)
    x1 = x + _group_rows(g1_ref[...], rows) * _dot(attn, wo_ref[...])
    h2 = _mod_norm(x1, n2_ref[...], sh_ref[...], sc_ref[...])
    return x1, h2


def _post_dense_kernel(xp_ref, xs_ref, op_ref, os_ref, wo_ref, g1_ref, n2_ref, sh_ref, sc_ref, g2_ref,
                       wg_ref, wu_ref, wd_ref, out_ref, x1_sc, h2_sc, acc_sc):
    j = pl.program_id(1)

    @pl.when(j == 0)
    def _():
        is_sample = pl.program_id(0) == pl.num_programs(0) - 1
        x1, h2 = _attn_out(_pick(is_sample, xp_ref, xs_ref), _pick(is_sample, op_ref, os_ref),
                           wo_ref, g1_ref, n2_ref, sh_ref, sc_ref)
        x1_sc[...] = x1
        h2_sc[...] = h2.astype(BF16)
        acc_sc[...] = jnp.zeros_like(acc_sc)

    h2 = h2_sc[...]
    act = (_silu(_dot(h2, wg_ref[...])) * _dot(h2, wu_ref[...])).astype(BF16)
    acc_sc[...] += _dot(act, wd_ref[...])

    @pl.when(j == pl.num_programs(1) - 1)
    def _():
        out_ref[...] = x1_sc[...] + _group_rows(g2_ref[...], x1_sc.shape[0]) * acc_sc[...]


def _post_dense(x_p, x_s, o_p, o_s, wo, g1_g, n2, sh_g, sc_g, g2_g, wg, wu, wd):
    tp, d = x_p.shape
    ntp = tp // TM
    f = wg.shape[1]
    gpt = TM // GROUP
    row = lambda i, j: (i, 0)
    prow = lambda i, j: (jnp.minimum(i, ntp - 1), 0)
    const = lambda i, j: (0, 0)
    return pl.pallas_call(
        _post_dense_kernel,
        grid=(ntp + 1, f // FC),
        in_specs=[
            pl.BlockSpec((TM, d), prow),
            pl.BlockSpec((TM, d), const),
            pl.BlockSpec((NP, TM, LANES), lambda i, j: (0, jnp.minimum(i, ntp - 1), 0)),
            pl.BlockSpec((NP, TM, LANES), lambda i, j: (0, 0, 0)),
            pl.BlockSpec((d, d), const),
            pl.BlockSpec((gpt, d), row),
            pl.BlockSpec((1, d), const),
            pl.BlockSpec((gpt, d), row),
            pl.BlockSpec((gpt, d), row),
            pl.BlockSpec((gpt, d), row),
            pl.BlockSpec((d, FC), lambda i, j: (0, j)),
            pl.BlockSpec((d, FC), lambda i, j: (0, j)),
            pl.BlockSpec((FC, d), lambda i, j: (j, 0)),
        ],
        out_specs=pl.BlockSpec((TM, d), row),
        out_shape=jax.ShapeDtypeStruct((tp + TM, d), F32),
        scratch_shapes=[pltpu.VMEM((TM, d), F32), pltpu.VMEM((TM, d), BF16), pltpu.VMEM((TM, d), F32)],
        compiler_params=_cparams(("arbitrary", "arbitrary")),
        name="attn_out_dense_ffn",
    )(x_p, x_s, o_p, o_s, wo, g1_g, n2, sh_g, sc_g, g2_g, wg, wu, wd)


def _post_router_kernel(x_ref, op_ref, os_ref, wo_ref, g1_ref, n2_ref, sh_ref, sc_ref, wr_ref,
                        x1_ref, h2_ref, route_ref, cnt_ref, cnt_sc):
    i = pl.program_id(0)
    rows = x_ref.shape[0]

    @pl.when(i == 0)
    def _():
        cnt_sc[...] = jnp.zeros_like(cnt_sc)

    is_sample = i == pl.num_programs(0) - 1
    x1, h2 = _attn_out(x_ref[...], _pick(is_sample, op_ref, os_ref), wo_ref, g1_ref, n2_ref, sh_ref, sc_ref)
    x1_ref[...] = x1
    h2_ref[...] = h2

    hh, hl = _split_hi_lo(h2)
    wh, wl = _split_hi_lo(wr_ref[...])
    logits = _dot(hh, wh) + _dot(hl, wh) + _dot(hh, wl)
    lane = lax.broadcasted_iota(jnp.int32, logits.shape, 1).astype(F32)
    logits = jnp.where(lane < N_EXPERTS, logits, -jnp.inf)
    v1 = logits.max(axis=1, keepdims=True)
    i1 = jnp.where(logits == v1, lane, float(LANES)).min(axis=1, keepdims=True)
    rest = jnp.where(lane == i1, -jnp.inf, logits)
    v2 = rest.max(axis=1, keepdims=True)
    i2 = jnp.where(rest == v2, lane, float(LANES)).min(axis=1, keepdims=True)
    w1 = 1.0 / (1.0 + jnp.exp(v2 - v1))
    w2 = 1.0 - w1

    oh1 = (lane == i1).astype(F32)
    oh2 = (lane == i2).astype(F32)
    cnt = oh1 + oh2
    lower = (lax.broadcasted_iota(jnp.int32, (rows, rows), 0)
             > lax.broadcasted_iota(jnp.int32, (rows, rows), 1)).astype(BF16)
    rank = _dot(lower, cnt.astype(BF16)) + cnt_sc[...]
    r1 = (oh1 * rank).sum(axis=1, keepdims=True)
    r2 = (oh2 * rank).sum(axis=1, keepdims=True)
    cnt_sc[...] += cnt.sum(axis=0, keepdims=True)
    cnt_ref[...] = cnt_sc[...]

    out = jnp.where(lane == 0, i1, 0.0)
    out = jnp.where(lane == 1, i2, out)
    out = jnp.where(lane == 2, w1, out)
    out = jnp.where(lane == 3, w2, out)
    out = jnp.where(lane == 4, r1, out)
    out = jnp.where(lane == 5, r2, out)
    route_ref[...] = out[:, :route_ref.shape[1]]


def _post_router(x_all, o_p, o_s, wo, g1_g, n2, sh_g, sc_g, wr_pad):
    t, d = x_all.shape
    ntp = t // TM - 1
    gpt = TM // GROUP
    row = lambda i: (i, 0)
    const = lambda i: (0, 0)
    return pl.pallas_call(
        _post_router_kernel,
        grid=(t // TM,),
        in_specs=[
            pl.BlockSpec((TM, d), row),
            pl.BlockSpec((NP, TM, LANES), lambda i: (0, jnp.minimum(i, ntp - 1), 0)),
            pl.BlockSpec((NP, TM, LANES), lambda i: (0, 0, 0)),
            pl.BlockSpec((d, d), const),
            pl.BlockSpec((gpt, d), row),
            pl.BlockSpec((1, d), const),
            pl.BlockSpec((gpt, d), row),
            pl.BlockSpec((gpt, d), row),
            pl.BlockSpec((d, LANES), const),
        ],
        out_specs=[
            pl.BlockSpec((TM, d), row),
            pl.BlockSpec((TM, d), row),
            pl.BlockSpec((TM, 8), row),
            pl.BlockSpec((1, LANES), const),
        ],
        out_shape=[
            jax.ShapeDtypeStruct((t, d), F32),
            jax.ShapeDtypeStruct((t, d), F32),
            jax.ShapeDtypeStruct((t, 8), F32),
            jax.ShapeDtypeStruct((1, LANES), F32),
        ],
        scratch_shapes=[pltpu.VMEM((1, LANES), F32)],
        compiler_params=_cparams(("arbitrary",)),
        name="attn_out_router",
    )(x_all, o_p, o_s, wo, g1_g, n2, sh_g, sc_g, wr_pad)


def _dispatch_kernel(fill_start_ref, fill_on_ref, pos_ref, h2_ref, xs_ref, zero_buf, sem):
    rows = h2_ref.shape[0]

    @pl.when(pl.program_id(0) == 0)
    def _():
        zero_buf[...] = jnp.zeros_like(zero_buf)

        def fill(k):
            return pltpu.make_async_copy(
                zero_buf, xs_ref.at[pl.ds(pl.multiple_of(fill_start_ref[k], TR), TR)], sem)

        for k in range(fill_on_ref.shape[0]):
            @pl.when(fill_on_ref[k] > 0)
            def _():
                fill(k).start()
        for k in range(fill_on_ref.shape[0]):
            @pl.when(fill_on_ref[k] > 0)
            def _():
                fill(k).wait()

    def issue(r, c):
        src = h2_ref.at[pl.ds(r, 1)]
        pltpu.make_async_copy(src, xs_ref.at[pl.ds(pos_ref[0, 0, r], 1)], sem).start()
        pltpu.make_async_copy(src, xs_ref.at[pl.ds(pos_ref[0, 0, rows + r], 1)], sem).start()
        return c

    lax.fori_loop(0, rows, issue, 0, unroll=8)
    for _ in range(2):
        pltpu.make_async_copy(h2_ref, xs_ref.at[pl.ds(0, rows)], sem).wait()


def _dispatch(fill_start, fill_on, pos_tiles, h2, r_pad):
    t, d = h2.shape
    assert TM == TR
    grid_spec = pltpu.PrefetchScalarGridSpec(
        num_scalar_prefetch=2,
        grid=(t // TM,),
        in_specs=[
            pl.BlockSpec((1, 1, 2 * TM), lambda i, fs, fo: (i, 0, 0), memory_space=pltpu.SMEM),
            pl.BlockSpec((TM, d), lambda i, fs, fo: (i, 0)),
        ],
        out_specs=pl.BlockSpec(memory_space=pl.ANY),
        scratch_shapes=[pltpu.VMEM((TR, d), F32), pltpu.SemaphoreType.DMA],
    )
    return pl.pallas_call(
        _dispatch_kernel,
        grid_spec=grid_spec,
        out_shape=jax.ShapeDtypeStruct((r_pad, d), F32),
        compiler_params=_cparams(("arbitrary",)),
        name="moe_dispatch",
    )(fill_start, fill_on, pos_tiles, h2)


def _expert_ffn_kernel(te_ref, tv_ref, xs_ref, wg_ref, wu_ref, wd_ref, y_ref, xb_sc, acc_sc):
    r = pl.program_id(0)
    j = pl.program_id(1)
    valid = tv_ref[r] > 0

    @pl.when(j == 0)
    def _():
        xb_sc[...] = xs_ref[...].astype(BF16)
        acc_sc[...] = jnp.zeros_like(acc_sc)

    @pl.when(valid)
    def _():
        x = xb_sc[...]
        act = (_silu(_dot(x, wg_ref[0])) * _dot(x, wu_ref[0])).astype(BF16)
        acc_sc[...] += _dot(act, wd_ref[0])

    @pl.when(j == pl.num_programs(1) - 1)
    def _():
        y_ref[...] = acc_sc[...]


def _expert_ffn(tile_expert, tile_valid, xs, wg, wu, wd):
    r_pad, d = xs.shape
    f = wg.shape[2]
    grid_spec = pltpu.PrefetchScalarGridSpec(
        num_scalar_prefetch=2,
        grid=(r_pad // TR, f // FC),
        in_specs=[
            pl.BlockSpec((TR, d), lambda r, j, te, tv: (r, 0)),
            pl.BlockSpec((1, d, FC), lambda r, j, te, tv: (te[r], 0, j)),
            pl.BlockSpec((1, d, FC), lambda r, j, te, tv: (te[r], 0, j)),
            pl.BlockSpec((1, FC, d), lambda r, j, te, tv: (te[r], j, 0)),
        ],
        out_specs=pl.BlockSpec((TR, d), lambda r, j, te, tv: (r, 0)),
        scratch_shapes=[pltpu.VMEM((TR, d), BF16), pltpu.VMEM((TR, d), F32)],
    )
    return pl.pallas_call(
        _expert_ffn_kernel,
        grid_spec=grid_spec,
        out_shape=jax.ShapeDtypeStruct((r_pad, d), F32),
        compiler_params=_cparams(("arbitrary", "arbitrary")),
        name="moe_expert_ffn",
    )(tile_expert, tile_valid, xs, wg, wu, wd)


def _combine_kernel(pos_ref, posn_ref, x1_ref, g2_ref, route_ref, y_ref, outp_ref, outs_ref, buf, sem, *,
                    ntp):
    i = pl.program_id(0)
    rows = x1_ref.shape[0]
    slot = i % 2

    def gather(p_ref, s):
        def issue(r, c):
            pltpu.make_async_copy(y_ref.at[pl.ds(p_ref[0, 0, r], 1)], buf.at[s, 0, pl.ds(r, 1)],
                                  sem.at[s]).start()
            pltpu.make_async_copy(y_ref.at[pl.ds(p_ref[0, 0, rows + r], 1)], buf.at[s, 1, pl.ds(r, 1)],
                                  sem.at[s]).start()
            return c
        lax.fori_loop(0, rows, issue, 0, unroll=8)

    @pl.when(i == 0)
    def _():
        gather(pos_ref, 0)

    @pl.when(i + 1 < pl.num_programs(0))
    def _():
        gather(posn_ref, 1 - slot)

    for k in range(2):
        pltpu.make_async_copy(y_ref.at[pl.ds(0, rows)], buf.at[slot, k], sem.at[slot]).wait()
    route = route_ref[...]
    moe = route[:, 2:3] * buf[slot, 0] + route[:, 3:4] * buf[slot, 1]
    res = x1_ref[...] + _group_rows(g2_ref[...], rows) * moe

    @pl.when(i < ntp)
    def _():
        outp_ref[...] = res

    @pl.when(i >= ntp)
    def _():
        outs_ref[...] = res


def _combine(pos_tiles, x1, g2_g, route, y, tp):
    t, d = x1.shape
    ntp = tp // TMC
    gpt = TMC // GROUP
    row = lambda i: (i, 0)
    last = t // TMC - 1
    return pl.pallas_call(
        functools.partial(_combine_kernel, ntp=ntp),
        grid=(t // TMC,),
        in_specs=[
            pl.BlockSpec((1, 1, 2 * TMC), lambda i: (i, 0, 0), memory_space=pltpu.SMEM),
            pl.BlockSpec((1, 1, 2 * TMC), lambda i: (jnp.minimum(i + 1, last), 0, 0), memory_space=pltpu.SMEM),
            pl.BlockSpec((TMC, d), row),
            pl.BlockSpec((gpt, d), row),
            pl.BlockSpec((TMC, 8), row),
            pl.BlockSpec(memory_space=pl.ANY),
        ],
        out_specs=[
            pl.BlockSpec((TMC, d), lambda i: (jnp.minimum(i, ntp - 1), 0)),
            pl.BlockSpec((TMC, d), lambda i: (jnp.maximum(i - ntp, 0), 0)),
        ],
        out_shape=[jax.ShapeDtypeStruct((tp, d), F32), jax.ShapeDtypeStruct((t - tp, d), F32)],
        scratch_shapes=[pltpu.VMEM((2, 2, TMC, d), F32), pltpu.SemaphoreType.DMA((2,))],
        compiler_params=_cparams(("arbitrary",)),
        name="moe_combine",
    )(pos_tiles, pos_tiles, x1, g2_g, route, y)


def _band_valid(q_pos, k_pos):
    qc = q_pos // CHUNK
    kc = k_pos // CHUNK
    return (k_pos[None, :] >= 0) & (kc[None, :] <= qc[:, None]) & (kc[None, :] >= qc[:, None] - N_PAST_CHUNKS)


def _band_bias(rel_bias, q_pos, k_pos):
    nq, nk = len(q_pos), len(k_pos)
    assert np.all(np.diff(q_pos) == 1) and np.all(np.diff(k_pos) == 1)
    period = nq + nk
    dist = (q_pos[0] - k_pos[0]) + (nq - 1) - np.arange(period)
    vec = rel_bias.astype(F32)[:, np.clip(dist, -REL_CLIP, REL_CLIP) + REL_CLIP]
    flat = jnp.tile(jnp.roll(vec, -(nq - 1), axis=1), (1, nq))[:, :nq * (period - 1)]
    tab = flat.reshape(-1, nq, period - 1)[:, :, :nk]
    return jnp.where(jnp.asarray(_band_valid(q_pos, k_pos))[None], tab, NEG_INF)


def _tile_pos(pos2, tile):
    t = pos2.shape[1]
    return pos2.reshape(2, t // tile, tile).transpose(1, 0, 2).reshape(t // tile, 1, 2 * tile)


def _feature_major(cache):
    n, length = cache.shape[0], cache.shape[1]
    return cache.transpose(0, 2, 3, 1).reshape(n, N_HEADS * HEAD_DIM, length)


def _token_major(xt):
    n, _, length = xt.shape
    return xt.reshape(n, N_HEADS, HEAD_DIM, length).transpose(0, 3, 1, 2)


def kernel(x_prompt, x_sample, cache_a_k, cache_a_v, cache_b_k, cache_b_v, c_prompt, c_sample, w_qkv, w_o, norm1_g, norm2_g, w_ada, b_ada, q_norm_g, k_norm_g, rel_bias, w_gate_d, w_up_d, w_down_d, w_router, w_gate_e, w_up_e, w_down_e):
    bp, sp, d = x_prompt.shape
    bs, ts, _ = x_sample.shape
    past_len = cache_b_k.shape[2]
    win = cache_a_k.shape[2]
    depth = w_qkv.shape[0]
    assert depth == 2 and d == N_HEADS * HEAD_DIM and ts == GROUP
    tp, tsm = bp * sp, bs * ts
    t_all = tp + tsm
    assert sp % TM == 0 and tsm == TM and sp % (BAND_UNROLL * TQ) == 0 and t_all % TMC == 0 and tp % TMC == 0
    assert win == BAND_PAST == TM and sp >= BAND_KEYS and past_len % TK == 0 and past_len >= win

    x_p = x_prompt.reshape(tp, d)
    x_s = x_sample.reshape(tsm, d)

    mod = _ada(jnp.concatenate([c_prompt, c_sample], axis=0), w_ada, b_ada)

    def groups(v):
        return jnp.concatenate([jnp.repeat(v[:bp], sp // GROUP, axis=0), v[bp:]], axis=0)

    bd = (np.arange(256)[:, None] // HEAD_DIM == np.arange(256)[None, :] // HEAD_DIM)
    bd = jnp.asarray(bd.astype(np.float32) / HEAD_DIM, BF16)
    ones_row = jnp.ones((1, d), F32)
    ones_col = jnp.ones((d, 1), F32)
    gps = tp // GROUP

    for i in range(depth):
        sh1, sc1, g1, sh2, sc2, g2 = [groups(m) for m in jnp.split(mod[i], 6, axis=-1)]
        band = i % 2 == 0
        wq = w_qkv[i][:, :d].astype(BF16)
        wkvt = w_qkv[i][:, d:].T.astype(BF16)
        qg = jnp.tile(q_norm_g[0], N_HEADS)[None] if band else ones_row
        kg = jnp.tile(k_norm_g[0], N_HEADS)[:, None] if band else ones_col
        n1 = norm1_g[i][None]
        if band:
            q_p, ktb, vtb, ktf, vtf = _qkv_prompt(x_p, n1, sh1, sc1, wq, wkvt, qg, kg, bd, bp, sp, True)
            q_s, kv_s, ksf, vsf = _qkv_sample(x_s, 0, n1, sh1[gps:], sc1[gps:], wq, wkvt, qg, kg, bd, True)
            q_pos = np.arange(TQ) + BAND_PAST
            bias_p = _band_bias(rel_bias[0], q_pos, np.arange(BAND_KEYS))
            bias_p = jnp.pad(bias_p, ((0, 0), (0, 0), (0, BAND_PAST)), constant_values=NEG_INF)
            o_p = _band_prompt(q_p, ktb, vtb, bias_p, bp, sp)
            pos_s = past_len + np.arange(ts)
            pos_win = past_len - win + np.arange(win)
            ca_k, ca_v = _feature_major(cache_a_k[0]), _feature_major(cache_a_v[0])
            o_s = _band_sample(q_s, kv_s, ca_k, ca_v, _band_bias(rel_bias[0], pos_s, pos_win),
                               _band_bias(rel_bias[0], pos_s, pos_s), bs, ts)
            a_k_p = _token_major(ktf)[None]
            a_v_p = _token_major(vtf)[None]
            new_k = ksf.reshape(d, bs, ts).transpose(1, 0, 2)
            new_v = vsf.reshape(d, bs, ts).transpose(1, 0, 2)
            a_k_s = _token_major(jnp.concatenate([ca_k[:, :, ts:], new_k], axis=2))[None]
            a_v_s = _token_major(jnp.concatenate([ca_v[:, :, ts:], new_v], axis=2))[None]
            x_all = _post_dense(x_p, x_s, o_p, o_s, w_o[i].astype(BF16), g1, norm2_g[i][None], sh2, sc2, g2,
                                w_gate_d[0].astype(BF16), w_up_d[0].astype(BF16), w_down_d[0].astype(BF16))
        else:
            q_p, ktb, vtb, ktf, vtf = _qkv_prompt(x_all, n1, sh1, sc1, wq, wkvt, qg, kg, bd, bp, sp, False)
            q_s, kv_s, ksf, vsf = _qkv_sample(x_all, tp // TM, n1, sh1[gps:], sc1[gps:], wq, wkvt, qg, kg, bd,
                                              False)
            o_p = _sb_prompt(q_p, ktb, vtb, bp, sp)
            o_s = _sb_sample(q_s, kv_s, _feature_major(cache_b_k[0]), _feature_major(cache_b_v[0]), bs, ts)
            b_k_p = _token_major(ktf)[None]
            b_v_p = _token_major(vtf)[None]
            b_k_s = ksf.reshape(1, bs, ts, N_HEADS, HEAD_DIM)
            b_v_s = vsf.reshape(1, bs, ts, N_HEADS, HEAD_DIM)
            wr_pad = jnp.pad(w_router[0], ((0, 0), (0, LANES - N_EXPERTS)))
            x1, h2, route, cnt = _post_router(x_all, o_p, o_s, w_o[i].astype(BF16), g1, norm2_g[i][None],
                                              sh2, sc2, wr_pad)
            counts = cnt[0, :N_EXPERTS].astype(jnp.int32)
            sizes = (counts + TR - 1) // TR * TR
            ends = jnp.cumsum(sizes)
            offs = ends - sizes
            route_t = route.T
            e12 = route_t[0:2].astype(jnp.int32)
            off12 = sum(jnp.where(e12 == e, offs[e], 0) for e in range(N_EXPERTS))
            pos2 = off12 + route_t[4:6].astype(jnp.int32)
            r_pad = (2 * t_all // TR + N_EXPERTS) * TR
            tile_start = jnp.arange(r_pad // TR, dtype=jnp.int32) * TR
            tile_expert = jnp.minimum(jnp.sum(tile_start[:, None] >= ends[None, :], axis=1),
                                      N_EXPERTS - 1).astype(jnp.int32)
            tile_valid = (tile_start < ends[-1]).astype(jnp.int32)
            tail_start = ends[-1] + jnp.arange(N_EXPERTS, dtype=jnp.int32) * TR
            fill_start = jnp.concatenate([ends - TR, tail_start]).astype(jnp.int32)
            fill_on = jnp.concatenate([sizes > 0, tail_start < r_pad]).astype(jnp.int32)
            xs = _dispatch(jnp.where(fill_on > 0, fill_start, 0), fill_on, _tile_pos(pos2, TM), h2, r_pad)
            y = _expert_ffn(tile_expert, tile_valid, xs, w_gate_e[0].astype(BF16),
                            w_up_e[0].astype(BF16), w_down_e[0].astype(BF16))
            y_p, y_s = _combine(_tile_pos(pos2, TMC), x1, g2, route, y, tp)

    return (y_p.reshape(bp, sp, d), y_s.reshape(bs, ts, d), a_k_p, a_v_p, a_k_s, a_v_s,
            b_k_p, b_v_p, b_k_s, b_v_s)
```

```python
import functools

import numpy as np
import jax
import jax.numpy as jnp
from jax import lax
from jax.experimental import pallas as pl
from jax.experimental.pallas import tpu as pltpu

F32 = jnp.float32
BF16 = jnp.bfloat16

CHUNK = 64
N_HEADS = 16
HEAD_DIM = 64
N_PAST_CHUNKS = 8
BAND_PAST = N_PAST_CHUNKS * CHUNK
REL_CLIP = 128
N_EXPERTS = 8
EPS = 1e-6
NEG_INF = -1e30

LANES = 128
VMEM_LIMIT = 56 * 1024 * 1024

MOD_SHIFT1, MOD_SCALE1, MOD_GATE1, MOD_SHIFT2, MOD_SCALE2, MOD_GATE2 = range(6)

GROUP = 32
TM = 512
FC = 1792
TQ = 128
TK = 128
NP = N_HEADS // 2
BAND_KEYS = BAND_PAST + TQ
BAND_UNROLL = 8
BAND_ROWS = 32
SB_PG = 2
SB_TILES = 4
SB_STOP = -90.0
TR = 1024
FCE = 512
TMC = 256


def _cparams(sem):
    return pltpu.CompilerParams(dimension_semantics=sem, vmem_limit_bytes=VMEM_LIMIT)


def _dot(a, b):
    return jnp.dot(a, b, preferred_element_type=F32)


def _dot_nt(a, b):
    return lax.dot_general(a, b, (((1,), (1,)), ((), ())), preferred_element_type=F32)


def _split_hi_lo(x):
    hi = x.astype(BF16)
    lo = (x - hi.astype(F32)).astype(BF16)
    return hi, lo


def _silu(x):
    return x * (1.0 / (1.0 + jnp.exp(-x)))


def _lane_lt64(shape):
    return lax.broadcasted_iota(jnp.int32, shape, len(shape) - 1) < HEAD_DIM


def _group_rows(vec_rows, rows):
    g, d = vec_rows.shape
    return jnp.broadcast_to(vec_rows[:, None, :], (g, rows // g, d)).reshape(rows, d)


def _mod_norm(x, g, shift_g, scale_g):
    rows = x.shape[0]
    ms = jnp.mean(x * x, axis=-1, keepdims=True)
    y = x * lax.rsqrt(ms + EPS) * g
    return y * (1.0 + _group_rows(scale_g, rows)) + _group_rows(shift_g, rows)


def _stack_heads(q2):
    m = _lane_lt64(q2.shape)
    z = jnp.zeros_like(q2)
    return jnp.concatenate([jnp.where(m, q2, z), jnp.where(m, z, q2)], axis=0)


def _pv_feature_major(pb, vt):
    r = pb.shape[0] // 2
    both = _dot_nt(pb, vt)
    return jnp.where(_lane_lt64((r, LANES)), both[:r], both[r:])


def _pv_row_major(pb, v2):
    r = pb.shape[0] // 2
    both = _dot(pb, v2)
    return jnp.where(_lane_lt64((r, LANES)), both[:r], both[r:])


def _ada_kernel(c_ref, w_ref, b_ref, o_ref):
    s = _silu(c_ref[...]).astype(BF16)
    o_ref[0] = _dot(s, w_ref[0].astype(BF16)) + b_ref[0]


def _ada(c_all, w_ada, b_ada):
    depth, d, n = w_ada.shape
    nb = c_all.shape[0]
    tn = 1536
    return pl.pallas_call(
        _ada_kernel,
        grid=(depth, n // tn),
        in_specs=[
            pl.BlockSpec((nb, d), lambda l, j: (0, 0)),
            pl.BlockSpec((1, d, tn), lambda l, j: (l, 0, j)),
            pl.BlockSpec((1, 1, tn), lambda l, j: (l, 0, j)),
        ],
        out_specs=pl.BlockSpec((1, nb, tn), lambda l, j: (l, 0, j)),
        out_shape=jax.ShapeDtypeStruct((depth, nb, n), F32),
        compiler_params=_cparams(("arbitrary", "arbitrary")),
        name="ada_modulation",
    )(c_all, w_ada, b_ada.reshape(depth, 1, n))


def _head_mean_sq_rows(x, bd_ref):
    hi, lo = _split_hi_lo(x * x)
    w = bd_ref.shape[0]
    cols = []
    for c in range(x.shape[1] // w):
        sl = slice(c * w, (c + 1) * w)
        cols.append(_dot(hi[:, sl], bd_ref[...]) + _dot(lo[:, sl], bd_ref[...]))
    return jnp.concatenate(cols, axis=1)


def _head_norm_cols(xt, gain_col):
    d, c = xt.shape
    x3 = xt.reshape(N_HEADS, HEAD_DIM, c)
    ms = jnp.mean(x3 * x3, axis=1, keepdims=True)
    return (x3 * lax.rsqrt(ms + EPS)).reshape(d, c) * gain_col


def _project(x, g_ref, sh_ref, sc_ref, wq_ref, wkvt_ref, qg_ref, kg_ref, bd_ref, qk_norm):
    d = x.shape[1]
    h = _mod_norm(x, g_ref[...], sh_ref[...], sc_ref[...]).astype(BF16)
    q = _dot(h, wq_ref[...])
    kvt = _dot_nt(wkvt_ref[...], h)
    kt, vt = kvt[:d], kvt[d:]
    if qk_norm:
        q = q * lax.rsqrt(_head_mean_sq_rows(q, bd_ref) + EPS) * qg_ref[...]
        kt = _head_norm_cols(kt, kg_ref[...])
    return (q * (HEAD_DIM ** -0.5)).astype(BF16), kt, vt


def _qkv_prompt_kernel(x_ref, g_ref, sh_ref, sc_ref, wq_ref, wkvt_ref, qg_ref, kg_ref, bd_ref,
                       q_ref, ktb_ref, vtb_ref, ktf_ref, vtf_ref, *, qk_norm, tail_only):
    qs, kt, vt = _project(x_ref[...], g_ref, sh_ref, sc_ref, wq_ref, wkvt_ref, qg_ref, kg_ref, bd_ref,
                          qk_norm)
    for p in range(NP):
        q_ref[p] = qs[:, p * LANES:(p + 1) * LANES]
    ktb_ref[0] = kt.astype(BF16)
    vtb_ref[0] = vt.astype(BF16)
    if tail_only:
        @pl.when((pl.program_id(0) + 1) % tail_only == 0)
        def _():
            ktf_ref[0] = kt
            vtf_ref[0] = vt
    else:
        ktf_ref[0] = kt
        vtf_ref[0] = vt


def _qkv_prompt(x, g, mod_g, wq, wkvt, qg_row, kg_col, bd, n_batch, s_len, qk_norm):
    d = x.shape[1]
    tpb = s_len // TM
    gpt = TM // GROUP
    row = lambda i: (i, 0)
    const = lambda i: (0, 0)
    shift_g = scale_g = mod_g
    fmaj = lambda i: (i // tpb, 0, i % tpb)
    if qk_norm:
        f_spec = pl.BlockSpec((1, d, TM), lambda i: (i // tpb, 0, 0))
        f_shape = jax.ShapeDtypeStruct((n_batch, d, TM), F32)
    else:
        f_spec = pl.BlockSpec((1, d, TM), fmaj)
        f_shape = jax.ShapeDtypeStruct((n_batch, d, s_len), F32)
    return pl.pallas_call(
        functools.partial(_qkv_prompt_kernel, qk_norm=qk_norm, tail_only=tpb if qk_norm else 0),
        grid=(n_batch * tpb,),
        in_specs=[
            pl.BlockSpec((TM, d), row),
            pl.BlockSpec((1, d), const),
            pl.BlockSpec((gpt, d), lambda i: (i, MOD_SHIFT1)),
            pl.BlockSpec((gpt, d), lambda i: (i, MOD_SCALE1)),
            pl.BlockSpec((d, d), const),
            pl.BlockSpec((2 * d, d), const),
            pl.BlockSpec((1, d), const),
            pl.BlockSpec((d, 1), const),
            pl.BlockSpec(bd.shape, const),
        ],
        out_specs=[
            pl.BlockSpec((NP, TM, LANES), lambda i: (0, i, 0)),
            pl.BlockSpec((1, d, TM), fmaj),
            pl.BlockSpec((1, d, TM), fmaj),
            f_spec, f_spec,
        ],
        out_shape=[
            jax.ShapeDtypeStruct((NP, n_batch * s_len, LANES), BF16),
            jax.ShapeDtypeStruct((n_batch, d, s_len), BF16),
            jax.ShapeDtypeStruct((n_batch, d, s_len), BF16),
            f_shape, f_shape,
        ],
        compiler_params=_cparams(("arbitrary",)),
        name="qkv_prompt_qknorm" if qk_norm else "qkv_prompt",
    )(x, g, shift_g, scale_g, wq, wkvt, qg_row, kg_col, bd)


def _qkv_sample_kernel(x_ref, g_ref, sh_ref, sc_ref, wq_ref, wkvt_ref, qg_ref, kg_ref, bd_ref,
                       q_ref, kv_ref, kf_ref, vf_ref, *, qk_norm):
    qs, kt, vt = _project(x_ref[...], g_ref, sh_ref, sc_ref, wq_ref, wkvt_ref, qg_ref, kg_ref, bd_ref,
                          qk_norm)
    k, v = kt.T, vt.T
    if qk_norm:
        kf_ref[...] = kt
        vf_ref[...] = vt
    else:
        kf_ref[...] = k
        vf_ref[...] = v
    kb, vb = k.astype(BF16), v.astype(BF16)
    for p in range(NP):
        sl = slice(p * LANES, (p + 1) * LANES)
        q_ref[p] = qs[:, sl]
        kv_ref[p] = kb[:, sl]
        kv_ref[NP + p] = vb[:, sl]


def _qkv_sample(x, row_block, g, mod_g, wq, wkvt, qg_row, kg_col, bd, qk_norm):
    d = x.shape[1]
    gpt = TM // GROUP
    last_g = mod_g.shape[0] // gpt - 1
    row = lambda i: (row_block, 0)
    const = lambda i: (0, 0)
    shift_g = scale_g = mod_g
    f_shape = jax.ShapeDtypeStruct((d, TM) if qk_norm else (TM, d), F32)
    return pl.pallas_call(
        functools.partial(_qkv_sample_kernel, qk_norm=qk_norm),
        grid=(1,),
        in_specs=[
            pl.BlockSpec((TM, d), row),
            pl.BlockSpec((1, d), const),
            pl.BlockSpec((gpt, d), lambda i: (last_g, MOD_SHIFT1)),
            pl.BlockSpec((gpt, d), lambda i: (last_g, MOD_SCALE1)),
            pl.BlockSpec((d, d), const),
            pl.BlockSpec((2 * d, d), const),
            pl.BlockSpec((1, d), const),
            pl.BlockSpec((d, 1), const),
            pl.BlockSpec(bd.shape, const),
        ],
        out_specs=[
            pl.BlockSpec((NP, TM, LANES), lambda i: (0, 0, 0)),
            pl.BlockSpec((2 * NP, TM, LANES), lambda i: (0, 0, 0)),
            pl.BlockSpec(f_shape.shape, const),
            pl.BlockSpec(f_shape.shape, const),
        ],
        out_shape=[
            jax.ShapeDtypeStruct((NP, TM, LANES), BF16),
            jax.ShapeDtypeStruct((2 * NP, TM, LANES), BF16),
            f_shape, f_shape,
        ],
        compiler_params=_cparams(("arbitrary",)),
        name="qkv_sample_qknorm" if qk_norm else "qkv_sample",
    )(x, g, shift_g, scale_g, wq, wkvt, qg_row, kg_col, bd)


def _band_prompt_kernel(q_ref, kt_ref, vt_ref, bias_ref, o_ref):
    s_len = q_ref.shape[1]
    back = BAND_PAST // TK

    def tiles(n, carry):
        idx = [n * BAND_UNROLL + u for u in range(BAND_UNROLL)]
        row0 = [pl.multiple_of(i * TQ, TQ) for i in idx]
        key0 = [pl.multiple_of(jnp.maximum(i - back, 0) * TK, TK) for i in idx]
        col0 = [pl.multiple_of(jnp.maximum(back - i, 0) * TK, TK) for i in idx]
        scores = [_dot(_stack_heads(q_ref[0, pl.ds(row0[u], TQ), :]), kt_ref[0, :, pl.ds(key0[u], BAND_KEYS)])
                  for u in range(BAND_UNROLL)]
        for u in range(BAND_UNROLL):
            ps, ls = [], []
            for h in range(2):
                for r0 in range(0, TQ, BAND_ROWS):
                    s = (scores[u][h * TQ + r0:h * TQ + r0 + BAND_ROWS]
                         + bias_ref[h, r0:r0 + BAND_ROWS, pl.ds(col0[u], BAND_KEYS)])
                    p = jnp.exp(s - s.max(axis=1, keepdims=True))
                    ls.append(p.sum(axis=1, keepdims=True))
                    ps.append(p.astype(BF16))
            l = jnp.concatenate(ls, axis=0)
            acc = _pv_feature_major(jnp.concatenate(ps, axis=0), vt_ref[0, :, pl.ds(key0[u], BAND_KEYS)])
            linv = jnp.where(_lane_lt64((TQ, LANES)), 1.0 / l[:TQ], 1.0 / l[TQ:])
            o_ref[0, pl.ds(row0[u], TQ), :] = (acc * linv).astype(BF16)
        return carry

    lax.fori_loop(0, s_len // (BAND_UNROLL * TQ), tiles, 0)


def _band_prompt(q_pm, ktb, vtb, bias_p, n_batch, s_len):
    return pl.pallas_call(
        _band_prompt_kernel,
        grid=(n_batch, NP),
        in_specs=[
            pl.BlockSpec((1, s_len, LANES), lambda b, p: (p, b, 0)),
            pl.BlockSpec((1, LANES, s_len), lambda b, p: (b, p, 0)),
            pl.BlockSpec((1, LANES, s_len), lambda b, p: (b, p, 0)),
            pl.BlockSpec((2, TQ, bias_p.shape[2]), lambda b, p: (p, 0, 0)),
        ],
        out_specs=pl.BlockSpec((1, s_len, LANES), lambda b, p: (p, b, 0)),
        out_shape=jax.ShapeDtypeStruct((NP, n_batch * s_len, LANES), BF16),
        compiler_params=_cparams(("arbitrary", "arbitrary")),
        name="band_attention_prompt",
    )(q_pm, ktb, vtb, bias_p)


def _band_sample_kernel(q_ref, kv_ref, kc_ref, vc_ref, bc_ref, bn_ref, knew_ref, vnew_ref,
                        o_ref, kwin_ref, vwin_ref):
    ts = q_ref.shape[1]
    win = kc_ref.shape[2]
    new_lanes = lax.broadcasted_iota(jnp.int32, knew_ref.shape[1:], 1) >= LANES - ts
    for c_ref, n_ref, w_ref in ((kc_ref, knew_ref, kwin_ref), (vc_ref, vnew_ref, vwin_ref)):
        rolled = pltpu.roll(c_ref[0], win - ts, 1)
        w_ref[0, :, :win - LANES] = rolled[:, :win - LANES]
        w_ref[0, :, win - LANES:] = jnp.where(new_lanes, n_ref[0], rolled[:, win - LANES:])
    for p in range(NP):
        rows = slice(p * LANES, (p + 1) * LANES)
        qm = _stack_heads(q_ref[p])
        kn, vn = kv_ref[p], kv_ref[NP + p]
        s1 = _dot(qm, kc_ref[0, rows, :].astype(BF16)) + bc_ref[2 * p:2 * p + 2].reshape(2 * ts, -1)
        s2 = _dot_nt(qm, kn) + bn_ref[2 * p:2 * p + 2].reshape(2 * ts, ts)
        m = jnp.maximum(s1.max(axis=1, keepdims=True), s2.max(axis=1, keepdims=True))
        p1 = jnp.exp(s1 - m)
        p2 = jnp.exp(s2 - m)
        l = p1.sum(axis=1, keepdims=True) + p2.sum(axis=1, keepdims=True)
        acc = (_pv_feature_major(p1.astype(BF16), vc_ref[0, rows, :].astype(BF16))
               + _pv_row_major(p2.astype(BF16), vn))
        linv = jnp.where(_lane_lt64((ts, LANES)), 1.0 / l[:ts], 1.0 / l[ts:])
        o_ref[p] = (acc * linv).astype(BF16)


def _band_sample(q_s, kv_s, cache_kt, cache_vt, bias_c, bias_n, new_kt, new_vt, n_streams, ts):
    d, win = cache_kt.shape[1], cache_kt.shape[2]
    stream = lambda s: (s, 0, 0)
    return pl.pallas_call(
        _band_sample_kernel,
        grid=(n_streams,),
        in_specs=[
            pl.BlockSpec((NP, ts, LANES), lambda s: (0, s, 0)),
            pl.BlockSpec((2 * NP, ts, LANES), lambda s: (0, s, 0)),
            pl.BlockSpec((1, d, win), stream),
            pl.BlockSpec((1, d, win), stream),
            pl.BlockSpec(bias_c.shape, lambda s: (0, 0, 0)),
            pl.BlockSpec(bias_n.shape, lambda s: (0, 0, 0)),
            pl.BlockSpec((1, d, LANES), stream),
            pl.BlockSpec((1, d, LANES), stream),
        ],
        out_specs=[
            pl.BlockSpec((NP, ts, LANES), lambda s: (0, s, 0)),
            pl.BlockSpec((1, d, win), stream),
            pl.BlockSpec((1, d, win), stream),
        ],
        out_shape=[
            jax.ShapeDtypeStruct(q_s.shape, BF16),
            jax.ShapeDtypeStruct(cache_kt.shape, F32),
            jax.ShapeDtypeStruct(cache_vt.shape, F32),
        ],
        compiler_params=_cparams(("arbitrary",)),
        name="band_attention_sample",
    )(q_s, kv_s, cache_kt, cache_vt, bias_c, bias_n, new_kt, new_vt)


def _sb_weights(z, upper, carry, mask):
    sp = jnp.maximum(z, 0.0) + jnp.log(1.0 + jnp.exp(-jnp.abs(z)))
    lk = -sp
    if mask is not None:
        lk = jnp.where(mask, lk, 0.0)
    hi, lo = _split_hi_lo(lk)
    after = _dot(hi, upper) + _dot(lo, upper) + carry
    a = jnp.exp(z - sp + after)
    if mask is not None:
        a = jnp.where(mask, a, 0.0)
    return a.astype(BF16), carry + lk.sum(axis=1, keepdims=True)


def _sb_weights_multi(zs, upper_ones, carries, masks):
    c = zs[0].shape[1]
    halves = []
    for z, mask in zip(zs, masks):
        nz = -z
        lk = jnp.minimum(nz, 0.0) - jnp.log(1.0 + jnp.exp(jnp.minimum(z, nz)))
        if mask is not None:
            lk = jnp.where(mask, lk, 0.0)
        halves.append(jnp.concatenate(_split_hi_lo(lk), axis=1))
    sums = [_dot(hl, upper_ones) for hl in halves]
    weights, new_carries = [], []
    for z, sm, carry, mask in zip(zs, sums, carries, masks):
        a = jnp.exp(z + (sm[:, :c] + carry))
        if mask is not None:
            a = jnp.where(mask, a, 0.0)
        weights.append(a.astype(BF16))
        new_carries.append(carry + sm[:, c:])
    return weights, new_carries


def _upper_tri(n):
    return (lax.broadcasted_iota(jnp.int32, (n, n), 0)
            > lax.broadcasted_iota(jnp.int32, (n, n), 1)).astype(BF16)


def _upper_tri_ones(n):
    incl = (lax.broadcasted_iota(jnp.int32, (n, n), 0)
            >= lax.broadcasted_iota(jnp.int32, (n, n), 1)).astype(BF16)
    half = jnp.concatenate([incl, jnp.ones((n, n), BF16)], axis=1)
    return jnp.concatenate([half, half], axis=0)


def _causal_mask(rows, cols):
    r = lax.broadcasted_iota(jnp.int32, (2 * rows, cols), 0)
    r = jnp.where(r >= rows, r - rows, r)
    c = lax.broadcasted_iota(jnp.int32, (2 * rows, cols), 1)
    return c < r


def _any_live(carries):
    m = carries[0].max()
    for c in carries[1:]:
        m = jnp.maximum(m, c.max())
    return (m >= SB_STOP).astype(jnp.int32)


def _sb_prompt_kernel(q_ref, kt_ref, vt_ref, o_ref):
    npg, s_len = q_ref.shape[0], q_ref.shape[1]
    upper_ones = _upper_tri_ones(TK)
    diag_mask = _causal_mask(TQ, TK)
    all_true = diag_mask | True

    units = [(g, t) for t in range(SB_TILES) for g in range(npg)]

    def blocks(qms, blks, carries, accs, masks):
        cols = [pl.ds(pl.multiple_of(b * TK, TK), TK) for b in blks]
        rows = [slice(g * LANES, (g + 1) * LANES) for g, _ in units]
        zs = [_dot(qms[u], kt_ref[0, rows[u], cols[u]]) for u in range(len(units))]
        ws, carries = _sb_weights_multi(zs, upper_ones, carries, masks)
        accs = [accs[u] + _pv_feature_major(ws[u], vt_ref[0, rows[u], cols[u]]) for u in range(len(units))]
        return carries, accs

    def tile_group(i2, c0):
        first = SB_TILES * i2
        qms = [_stack_heads(q_ref[g, pl.ds(pl.multiple_of((first + t) * TQ, TQ), TQ), :]) for g, t in units]
        carries, accs = blocks(qms, [first + t for _, t in units],
                               [jnp.zeros((2 * TQ, TK), F32)] * len(units),
                               [jnp.zeros((TQ, LANES), F32)] * len(units), [diag_mask] * len(units))

        def cond(st):
            return jnp.logical_and(first + SB_TILES - 1 - st[0] >= 0, st[1] > 0)

        def body(st):
            k = st[0]
            blks = [first + t - k for _, t in units]
            masks = [None if t == SB_TILES - 1 else jnp.logical_and(all_true, blks[u] >= 0)
                     for u, (_, t) in enumerate(units)]
            cs, as_ = blocks(qms, [jnp.maximum(b, 0) for b in blks], list(st[2]), list(st[3]), masks)
            return (k + 1, _any_live(cs), tuple(cs), tuple(as_))

        st = lax.while_loop(cond, body, (1, _any_live(carries), tuple(carries), tuple(accs)))
        for u, (g, t) in enumerate(units):
            row0 = pl.multiple_of((first + t) * TQ, TQ)
            o_ref[g, pl.ds(row0, TQ), :] = st[3][u].astype(BF16)
        return c0

    lax.fori_loop(0, s_len // (SB_TILES * TQ), tile_group, 0)


def _sb_prompt(q_pm, ktb, vtb, n_batch, s_len):
    ng = NP // SB_PG
    return pl.pallas_call(
        _sb_prompt_kernel,
        grid=(n_batch, ng),
        in_specs=[
            pl.BlockSpec((SB_PG, s_len, LANES), lambda b, g: (g, b, 0)),
            pl.BlockSpec((1, SB_PG * LANES, s_len), lambda b, g: (b, g, 0)),
            pl.BlockSpec((1, SB_PG * LANES, s_len), lambda b, g: (b, g, 0)),
        ],
        out_specs=pl.BlockSpec((SB_PG, s_len, LANES), lambda b, g: (g, b, 0)),
        out_shape=jax.ShapeDtypeStruct((NP, n_batch * s_len, LANES), BF16),
        compiler_params=_cparams(("arbitrary", "arbitrary")),
        name="stick_breaking_prompt",
    )(q_pm, ktb, vtb)


def _sb_sample_kernel(q_ref, kv_ref, kc_hbm, vc_hbm, o_ref, kbuf, vbuf, sem):
    s = pl.program_id(0)
    ts = q_ref.shape[1]
    nblk = kc_hbm.shape[2] // TK
    upper_ones = _upper_tri_ones(TK)

    def copies(j):
        slot = j % 2
        cols = pl.ds(pl.multiple_of(j * TK, TK), TK)
        return (pltpu.make_async_copy(kc_hbm.at[s, :, cols], kbuf.at[slot], sem.at[0, slot]),
                pltpu.make_async_copy(vc_hbm.at[s, :, cols], vbuf.at[slot], sem.at[1, slot]))

    def start(j):
        for c in copies(j):
            c.start()

    def wait(j):
        for c in copies(j):
            c.wait()

    start(nblk - 1)

    qms, carries, accs = [], [], []
    mask = _causal_mask(ts, ts)
    upper_n = _upper_tri(ts)
    for p in range(NP):
        qm = _stack_heads(q_ref[p])
        ab, c = _sb_weights(_dot_nt(qm, kv_ref[p]), upper_n, jnp.zeros((2 * ts, 1), F32), mask)
        qms.append(qm)
        carries.append(jnp.broadcast_to(c, (2 * ts, TK)))
        accs.append(_pv_row_major(ab, kv_ref[NP + p]))

    def cond(st):
        return jnp.logical_and(st[0] >= 0, st[1] > 0)

    def body(st):
        j = st[0]
        wait(j)

        @pl.when(j > 0)
        def _():
            start(j - 1)

        slot = j % 2
        rows = [slice(p * LANES, (p + 1) * LANES) for p in range(NP)]
        zs = [_dot(qms[p], kbuf[slot, rows[p], :].astype(BF16)) for p in range(NP)]
        ws, cs = _sb_weights_multi(zs, upper_ones, list(st[2]), [None] * NP)
        as_ = [st[3][p] + _pv_feature_major(ws[p], vbuf[slot, rows[p], :].astype(BF16)) for p in range(NP)]
        return (j - 1, _any_live(cs), tuple(cs), tuple(as_))

    st = lax.while_loop(cond, body, (nblk - 1, _any_live(carries), tuple(carries), tuple(accs)))

    @pl.when(st[0] >= 0)
    def _():
        wait(st[0])

    for p in range(NP):
        o_ref[p] = st[3][p].astype(BF16)


def _sb_sample(q_s, kv_s, cache_kt, cache_vt, n_streams, ts):
    d = cache_kt.shape[1]
    return pl.pallas_call(
        _sb_sample_kernel,
        grid=(n_streams,),
        in_specs=[
            pl.BlockSpec((NP, ts, LANES), lambda s: (0, s, 0)),
            pl.BlockSpec((2 * NP, ts, LANES), lambda s: (0, s, 0)),
            pl.BlockSpec(memory_space=pl.ANY),
            pl.BlockSpec(memory_space=pl.ANY),
        ],
        out_specs=pl.BlockSpec((NP, ts, LANES), lambda s: (0, s, 0)),
        out_shape=jax.ShapeDtypeStruct(q_s.shape, BF16),
        scratch_shapes=[pltpu.VMEM((2, d, TK), F32), pltpu.VMEM((2, d, TK), F32),
                        pltpu.SemaphoreType.DMA((2, 2))],
        compiler_params=_cparams(("arbitrary",)),
        name="stick_breaking_sample",
    )(q_s, kv_s, cache_kt, cache_vt)


def _pick(is_sample, prompt_ref, sample_ref):
    return jnp.where(is_sample, sample_ref[...], prompt_ref[...])


def _attn_out(x, o, wo_ref, g1_ref, n2_ref, sh_ref, sc_ref):
    rows = x.shape[0]
    attn = jnp.concatenate([o[p] for p in range(NP)], axis=1)
    x1 = x + _group_rows(g1_ref[...], rows) * _dot(attn, wo_ref[...])
    h2 = _mod_norm(x1, n2_ref[...], sh_ref[...], sc_ref[...])
    return x1, h2


def _post_dense_kernel(xp_ref, xs_ref, op_ref, os_ref, wo_ref, g1_ref, n2_ref, sh_ref, sc_ref, g2_ref,
                       wg_ref, wu_ref, wd_ref, out_ref, x1_sc, h2_sc, acc_sc):
    j = pl.program_id(1)

    @pl.when(j == 0)
    def _():
        is_sample = pl.program_id(0) == pl.num_programs(0) - 1
        x1, h2 = _attn_out(_pick(is_sample, xp_ref, xs_ref), _pick(is_sample, op_ref, os_ref),
                           wo_ref, g1_ref, n2_ref, sh_ref, sc_ref)
        x1_sc[...] = x1
        h2_sc[...] = h2.astype(BF16)
        acc_sc[...] = jnp.zeros_like(acc_sc)

    h2 = h2_sc[...]
    act = (_silu(_dot(h2, wg_ref[...])) * _dot(h2, wu_ref[...])).astype(BF16)
    acc_sc[...] += _dot(act, wd_ref[...])

    @pl.when(j == pl.num_programs(1) - 1)
    def _():
        out_ref[...] = x1_sc[...] + _group_rows(g2_ref[...], x1_sc.shape[0]) * acc_sc[...]


def _post_dense(x_p, x_s, o_p, o_s, wo, mod_g, n2, wg, wu, wd):
    tp, d = x_p.shape
    ntp = tp // TM
    f = wg.shape[1]
    gpt = TM // GROUP
    row = lambda i, j: (i, 0)
    prow = lambda i, j: (jnp.minimum(i, ntp - 1), 0)
    const = lambda i, j: (0, 0)
    g1_g = sh_g = sc_g = g2_g = mod_g
    return pl.pallas_call(
        _post_dense_kernel,
        grid=(ntp + 1, f // FC),
        in_specs=[
            pl.BlockSpec((TM, d), prow),
            pl.BlockSpec((TM, d), const),
            pl.BlockSpec((NP, TM, LANES), lambda i, j: (0, jnp.minimum(i, ntp - 1), 0)),
            pl.BlockSpec((NP, TM, LANES), lambda i, j: (0, 0, 0)),
            pl.BlockSpec((d, d), const),
            pl.BlockSpec((gpt, d), lambda i, j: (i, MOD_GATE1)),
            pl.BlockSpec((1, d), const),
            pl.BlockSpec((gpt, d), lambda i, j: (i, MOD_SHIFT2)),
            pl.BlockSpec((gpt, d), lambda i, j: (i, MOD_SCALE2)),
            pl.BlockSpec((gpt, d), lambda i, j: (i, MOD_GATE2)),
            pl.BlockSpec((d, FC), lambda i, j: (0, j)),
            pl.BlockSpec((d, FC), lambda i, j: (0, j)),
            pl.BlockSpec((FC, d), lambda i, j: (j, 0)),
        ],
        out_specs=pl.BlockSpec((TM, d), row),
        out_shape=jax.ShapeDtypeStruct((tp + TM, d), F32),
        scratch_shapes=[pltpu.VMEM((TM, d), F32), pltpu.VMEM((TM, d), BF16), pltpu.VMEM((TM, d), F32)],
        compiler_params=_cparams(("arbitrary", "arbitrary")),
        name="attn_out_dense_ffn",
    )(x_p, x_s, o_p, o_s, wo, g1_g, n2, sh_g, sc_g, g2_g, wg, wu, wd)


def _post_router_kernel(x_ref, op_ref, os_ref, wo_ref, g1_ref, n2_ref, sh_ref, sc_ref, wr_ref,
                        x1_ref, h2_ref, route_ref, cnt_ref, cnt_sc):
    i = pl.program_id(0)
    rows = x_ref.shape[0]

    @pl.when(i == 0)
    def _():
        cnt_sc[...] = jnp.zeros_like(cnt_sc)

    is_sample = i == pl.num_programs(0) - 1
    x1, h2 = _attn_out(x_ref[...], _pick(is_sample, op_ref, os_ref), wo_ref, g1_ref, n2_ref, sh_ref, sc_ref)
    x1_ref[...] = x1
    h2_ref[...] = h2

    hh, hl = _split_hi_lo(h2)
    wh, wl = _split_hi_lo(wr_ref[...])
    logits = _dot(hh, wh) + _dot(hl, wh) + _dot(hh, wl)
    lane = lax.broadcasted_iota(jnp.int32, logits.shape, 1).astype(F32)
    logits = jnp.where(lane < N_EXPERTS, logits, -jnp.inf)
    v1 = logits.max(axis=1, keepdims=True)
    i1 = jnp.where(logits == v1, lane, float(LANES)).min(axis=1, keepdims=True)
    rest = jnp.where(lane == i1, -jnp.inf, logits)
    v2 = rest.max(axis=1, keepdims=True)
    i2 = jnp.where(rest == v2, lane, float(LANES)).min(axis=1, keepdims=True)
    w1 = 1.0 / (1.0 + jnp.exp(v2 - v1))
    w2 = 1.0 - w1

    oh1 = (lane == i1).astype(F32)
    oh2 = (lane == i2).astype(F32)
    cnt = oh1 + oh2
    lower = (lax.broadcasted_iota(jnp.int32, (rows, rows), 0)
             > lax.broadcasted_iota(jnp.int32, (rows, rows), 1)).astype(BF16)
    rank = _dot(lower, cnt.astype(BF16)) + cnt_sc[...]
    r1 = (oh1 * rank).sum(axis=1, keepdims=True)
    r2 = (oh2 * rank).sum(axis=1, keepdims=True)
    cnt_sc[...] += cnt.sum(axis=0, keepdims=True)
    cnt_ref[...] = cnt_sc[...]

    out = jnp.where(lane == 0, i1, 0.0)
    out = jnp.where(lane == 1, i2, out)
    out = jnp.where(lane == 2, w1, out)
    out = jnp.where(lane == 3, w2, out)
    out = jnp.where(lane == 4, r1, out)
    out = jnp.where(lane == 5, r2, out)
    route_ref[...] = out[:, :route_ref.shape[1]]


def _post_router(x_all, o_p, o_s, wo, mod_g, n2, wr_pad):
    t, d = x_all.shape
    ntp = t // TM - 1
    gpt = TM // GROUP
    row = lambda i: (i, 0)
    const = lambda i: (0, 0)
    g1_g = sh_g = sc_g = mod_g
    return pl.pallas_call(
        _post_router_kernel,
        grid=(t // TM,),
        in_specs=[
            pl.BlockSpec((TM, d), row),
            pl.BlockSpec((NP, TM, LANES), lambda i: (0, jnp.minimum(i, ntp - 1), 0)),
            pl.BlockSpec((NP, TM, LANES), lambda i: (0, 0, 0)),
            pl.BlockSpec((d, d), const),
            pl.BlockSpec((gpt, d), lambda i: (i, MOD_GATE1)),
            pl.BlockSpec((1, d), const),
            pl.BlockSpec((gpt, d), lambda i: (i, MOD_SHIFT2)),
            pl.BlockSpec((gpt, d), lambda i: (i, MOD_SCALE2)),
            pl.BlockSpec((d, LANES), const),
        ],
        out_specs=[
            pl.BlockSpec((TM, d), row),
            pl.BlockSpec((TM, d), row),
            pl.BlockSpec((TM, 8), row),
            pl.BlockSpec((1, LANES), const),
        ],
        out_shape=[
            jax.ShapeDtypeStruct((t, d), F32),
            jax.ShapeDtypeStruct((t, d), F32),
            jax.ShapeDtypeStruct((t, 8), F32),
            jax.ShapeDtypeStruct((1, LANES), F32),
        ],
        scratch_shapes=[pltpu.VMEM((1, LANES), F32)],
        compiler_params=_cparams(("arbitrary",)),
        name="attn_out_router",
    )(x_all, o_p, o_s, wo, g1_g, n2, sh_g, sc_g, wr_pad)


def _dispatch_kernel(fill_start_ref, fill_on_ref, pos_ref, h2_ref, xs_ref, zero_buf, sem):
    rows = h2_ref.shape[0]

    @pl.when(pl.program_id(0) == 0)
    def _():
        zero_buf[...] = jnp.zeros_like(zero_buf)

        def fill(k):
            return pltpu.make_async_copy(
                zero_buf, xs_ref.at[pl.ds(pl.multiple_of(fill_start_ref[k], TR), TR)], sem)

        for k in range(fill_on_ref.shape[0]):
            @pl.when(fill_on_ref[k] > 0)
            def _():
                fill(k).start()
        for k in range(fill_on_ref.shape[0]):
            @pl.when(fill_on_ref[k] > 0)
            def _():
                fill(k).wait()

    def issue(r, c):
        src = h2_ref.at[pl.ds(r, 1)]
        pltpu.make_async_copy(src, xs_ref.at[pl.ds(pos_ref[0, 0, r], 1)], sem).start()
        pltpu.make_async_copy(src, xs_ref.at[pl.ds(pos_ref[0, 0, rows + r], 1)], sem).start()
        return c

    lax.fori_loop(0, rows, issue, 0, unroll=8)
    for _ in range(2):
        pltpu.make_async_copy(h2_ref, xs_ref.at[pl.ds(0, rows)], sem).wait()


def _dispatch(fill_start, fill_on, pos_tiles, h2, r_pad):
    t, d = h2.shape
    grid_spec = pltpu.PrefetchScalarGridSpec(
        num_scalar_prefetch=2,
        grid=(t // TM,),
        in_specs=[
            pl.BlockSpec((1, 1, 2 * TM), lambda i, fs, fo: (i, 0, 0), memory_space=pltpu.SMEM),
            pl.BlockSpec((TM, d), lambda i, fs, fo: (i, 0)),
        ],
        out_specs=pl.BlockSpec(memory_space=pl.ANY),
        scratch_shapes=[pltpu.VMEM((TR, d), F32), pltpu.SemaphoreType.DMA],
    )
    return pl.pallas_call(
        _dispatch_kernel,
        grid_spec=grid_spec,
        out_shape=jax.ShapeDtypeStruct((r_pad, d), F32),
        compiler_params=_cparams(("arbitrary",)),
        name="moe_dispatch",
    )(fill_start, fill_on, pos_tiles, h2)


def _expert_ffn_kernel(te_ref, tv_ref, xs_ref, wg_ref, wu_ref, wd_ref, y_ref, xb_sc, acc_sc):
    r = pl.program_id(0)
    j = pl.program_id(1)
    valid = tv_ref[r] > 0

    @pl.when(j == 0)
    def _():
        xb_sc[...] = xs_ref[...].astype(BF16)
        acc_sc[...] = jnp.zeros_like(acc_sc)

    @pl.when(valid)
    def _():
        x = xb_sc[...]
        act = (_silu(_dot(x, wg_ref[0].astype(BF16))) * _dot(x, wu_ref[0].astype(BF16))).astype(BF16)
        acc_sc[...] += _dot(act, wd_ref[0].astype(BF16))

    @pl.when(j == pl.num_programs(1) - 1)
    def _():
        y_ref[...] = acc_sc[...]


def _expert_ffn(tile_expert, tile_valid, xs, wg, wu, wd):
    r_pad, d = xs.shape
    f = wg.shape[2]
    grid_spec = pltpu.PrefetchScalarGridSpec(
        num_scalar_prefetch=2,
        grid=(r_pad // TR, f // FCE),
        in_specs=[
            pl.BlockSpec((TR, d), lambda r, j, te, tv: (r, 0)),
            pl.BlockSpec((1, d, FCE), lambda r, j, te, tv: (te[r], 0, j)),
            pl.BlockSpec((1, d, FCE), lambda r, j, te, tv: (te[r], 0, j)),
            pl.BlockSpec((1, FCE, d), lambda r, j, te, tv: (te[r], j, 0)),
        ],
        out_specs=pl.BlockSpec((TR, d), lambda r, j, te, tv: (r, 0)),
        scratch_shapes=[pltpu.VMEM((TR, d), BF16), pltpu.VMEM((TR, d), F32)],
    )
    return pl.pallas_call(
        _expert_ffn_kernel,
        grid_spec=grid_spec,
        out_shape=jax.ShapeDtypeStruct((r_pad, d), F32),
        compiler_params=_cparams(("arbitrary", "arbitrary")),
        name="moe_expert_ffn",
    )(tile_expert, tile_valid, xs, wg, wu, wd)


def _combine_kernel(pos_ref, posn_ref, x1_ref, g2_ref, route_ref, y_ref, outp_ref, outs_ref, buf, sem, *,
                    ntp):
    i = pl.program_id(0)
    rows = x1_ref.shape[0]
    slot = i % 2

    def gather(p_ref, s):
        def issue(r, c):
            pltpu.make_async_copy(y_ref.at[pl.ds(p_ref[0, 0, r], 1)], buf.at[s, 0, pl.ds(r, 1)],
                                  sem.at[s]).start()
            pltpu.make_async_copy(y_ref.at[pl.ds(p_ref[0, 0, rows + r], 1)], buf.at[s, 1, pl.ds(r, 1)],
                                  sem.at[s]).start()
            return c
        lax.fori_loop(0, rows, issue, 0, unroll=8)

    @pl.when(i == 0)
    def _():
        gather(pos_ref, 0)

    @pl.when(i + 1 < pl.num_programs(0))
    def _():
        gather(posn_ref, 1 - slot)

    for k in range(2):
        pltpu.make_async_copy(y_ref.at[pl.ds(0, rows)], buf.at[slot, k], sem.at[slot]).wait()
    route = route_ref[...]
    moe = route[:, 2:3] * buf[slot, 0] + route[:, 3:4] * buf[slot, 1]
    res = x1_ref[...] + _group_rows(g2_ref[...], rows) * moe

    @pl.when(i < ntp)
    def _():
        outp_ref[...] = res

    @pl.when(i >= ntp)
    def _():
        outs_ref[...] = res


def _combine(pos_tiles, x1, mod_g, route, y, tp):
    t, d = x1.shape
    ntp = tp // TMC
    gpt = TMC // GROUP
    row = lambda i: (i, 0)
    last = t // TMC - 1
    g2_g = mod_g
    return pl.pallas_call(
        functools.partial(_combine_kernel, ntp=ntp),
        grid=(t // TMC,),
        in_specs=[
            pl.BlockSpec((1, 1, 2 * TMC), lambda i: (i, 0, 0), memory_space=pltpu.SMEM),
            pl.BlockSpec((1, 1, 2 * TMC), lambda i: (jnp.minimum(i + 1, last), 0, 0), memory_space=pltpu.SMEM),
            pl.BlockSpec((TMC, d), row),
            pl.BlockSpec((gpt, d), lambda i: (i, MOD_GATE2)),
            pl.BlockSpec((TMC, 8), row),
            pl.BlockSpec(memory_space=pl.ANY),
        ],
        out_specs=[
            pl.BlockSpec((TMC, d), lambda i: (jnp.minimum(i, ntp - 1), 0)),
            pl.BlockSpec((TMC, d), lambda i: (jnp.maximum(i - ntp, 0), 0)),
        ],
        out_shape=[jax.ShapeDtypeStruct((tp, d), F32), jax.ShapeDtypeStruct((t - tp, d), F32)],
        scratch_shapes=[pltpu.VMEM((2, 2, TMC, d), F32), pltpu.SemaphoreType.DMA((2,))],
        compiler_params=_cparams(("arbitrary",)),
        name="moe_combine",
    )(pos_tiles, pos_tiles, x1, g2_g, route, y)


def _band_valid(q_pos, k_pos):
    qc = q_pos // CHUNK
    kc = k_pos // CHUNK
    return (k_pos[None, :] >= 0) & (kc[None, :] <= qc[:, None]) & (kc[None, :] >= qc[:, None] - N_PAST_CHUNKS)


def _band_bias(rel_bias, q_pos, k_pos):
    nq, nk = len(q_pos), len(k_pos)
    assert np.all(np.diff(q_pos) == 1) and np.all(np.diff(k_pos) == 1)
    period = nq + nk
    dist = (q_pos[0] - k_pos[0]) + (nq - 1) - np.arange(period)
    vec = rel_bias.astype(F32)[:, np.clip(dist, -REL_CLIP, REL_CLIP) + REL_CLIP]
    flat = jnp.tile(jnp.roll(vec, -(nq - 1), axis=1), (1, nq))[:, :nq * (period - 1)]
    tab = flat.reshape(-1, nq, period - 1)[:, :, :nk]
    return jnp.where(jnp.asarray(_band_valid(q_pos, k_pos))[None], tab, NEG_INF)


def _tile_pos(pos2, tile):
    t = pos2.shape[1]
    return pos2.reshape(2, t // tile, tile).transpose(1, 0, 2).reshape(t // tile, 1, 2 * tile)


def _feature_major(cache):
    n, length = cache.shape[0], cache.shape[1]
    return cache.transpose(0, 2, 3, 1).reshape(n, N_HEADS * HEAD_DIM, length)


def _token_major(xt):
    n, _, length = xt.shape
    return xt.reshape(n, N_HEADS, HEAD_DIM, length).transpose(0, 3, 1, 2)


def kernel(x_prompt, x_sample, cache_a_k, cache_a_v, cache_b_k, cache_b_v, c_prompt, c_sample, w_qkv, w_o, norm1_g, norm2_g, w_ada, b_ada, q_norm_g, k_norm_g, rel_bias, w_gate_d, w_up_d, w_down_d, w_router, w_gate_e, w_up_e, w_down_e):
    bp, sp, d = x_prompt.shape
    bs, ts, _ = x_sample.shape
    past_len = cache_b_k.shape[2]
    win = cache_a_k.shape[2]
    depth = w_qkv.shape[0]
    assert depth == 2 and d == N_HEADS * HEAD_DIM and ts == GROUP
    tp, tsm = bp * sp, bs * ts
    t_all = tp + tsm
    assert sp % TM == 0 and tsm == TM and sp % (BAND_UNROLL * TQ) == 0 and t_all % TMC == 0 and tp % TMC == 0
    assert win == BAND_PAST == TM and sp >= BAND_KEYS and past_len % TK == 0 and past_len >= win

    x_p = x_prompt.reshape(tp, d)
    x_s = x_sample.reshape(tsm, d)

    mod = _ada(jnp.concatenate([c_prompt, c_sample], axis=0), w_ada, b_ada)
    mod = jnp.concatenate([jnp.repeat(mod[:, :bp], sp // GROUP, axis=1), mod[:, bp:]], axis=1)

    bd = (np.arange(256)[:, None] // HEAD_DIM == np.arange(256)[None, :] // HEAD_DIM)
    bd = jnp.asarray(bd.astype(np.float32) / HEAD_DIM, BF16)
    ones_row = jnp.ones((1, d), F32)
    ones_col = jnp.ones((d, 1), F32)

    for i in range(depth):
        mod_g = mod[i]
        band = i % 2 == 0
        wq = w_qkv[i][:, :d].astype(BF16)
        wkvt = w_qkv[i][:, d:].T.astype(BF16)
        qg = jnp.tile(q_norm_g[0], N_HEADS)[None] if band else ones_row
        kg = jnp.tile(k_norm_g[0], N_HEADS)[:, None] if band else ones_col
        n1 = norm1_g[i][None]
        if band:
            q_p, ktb, vtb, ktf, vtf = _qkv_prompt(x_p, n1, mod_g, wq, wkvt, qg, kg, bd, bp, sp, True)
            q_s, kv_s, ksf, vsf = _qkv_sample(x_s, 0, n1, mod_g, wq, wkvt, qg, kg, bd, True)
            q_pos = np.arange(TQ) + BAND_PAST
            bias_p = _band_bias(rel_bias[0], q_pos, np.arange(BAND_KEYS))
            bias_p = jnp.pad(bias_p, ((0, 0), (0, 0), (0, BAND_PAST)), constant_values=NEG_INF)
            o_p = _band_prompt(q_p, ktb, vtb, bias_p, bp, sp)
            pos_s = past_len + np.arange(ts)
            pos_win = past_len - win + np.arange(win)
            ca_k, ca_v = _feature_major(cache_a_k[0]), _feature_major(cache_a_v[0])
            right = ((0, 0), (0, 0), (LANES - ts, 0))
            new_k = jnp.pad(ksf.reshape(d, bs, ts).transpose(1, 0, 2), right)
            new_v = jnp.pad(vsf.reshape(d, bs, ts).transpose(1, 0, 2), right)
            o_s, k_win, v_win = _band_sample(q_s, kv_s, ca_k, ca_v, _band_bias(rel_bias[0], pos_s, pos_win),
                                             _band_bias(rel_bias[0], pos_s, pos_s), new_k, new_v, bs, ts)
            a_k_p = _token_major(ktf)[None]
            a_v_p = _token_major(vtf)[None]
            a_k_s = _token_major(k_win)[None]
            a_v_s = _token_major(v_win)[None]
            x_all = _post_dense(x_p, x_s, o_p, o_s, w_o[i].astype(BF16), mod_g, norm2_g[i][None],
                                w_gate_d[0].astype(BF16), w_up_d[0].astype(BF16), w_down_d[0].astype(BF16))
        else:
            q_p, ktb, vtb, ktf, vtf = _qkv_prompt(x_all, n1, mod_g, wq, wkvt, qg, kg, bd, bp, sp, False)
            q_s, kv_s, ksf, vsf = _qkv_sample(x_all, tp // TM, n1, mod_g, wq, wkvt, qg, kg, bd, False)
            o_p = _sb_prompt(q_p, ktb, vtb, bp, sp)
            o_s = _sb_sample(q_s, kv_s, _feature_major(cache_b_k[0]), _feature_major(cache_b_v[0]), bs, ts)
            b_k_p = _token_major(ktf)[None]
            b_v_p = _token_major(vtf)[None]
            b_k_s = ksf.reshape(1, bs, ts, N_HEADS, HEAD_DIM)
            b_v_s = vsf.reshape(1, bs, ts, N_HEADS, HEAD_DIM)
            wr_pad = jnp.pad(w_router[0], ((0, 0), (0, LANES - N_EXPERTS)))
            x1, h2, route, cnt = _post_router(x_all, o_p, o_s, w_o[i].astype(BF16), mod_g, norm2_g[i][None],
                                              wr_pad)
            counts = cnt[0, :N_EXPERTS].astype(jnp.int32)
            sizes = (counts + TR - 1) // TR * TR
            ends = jnp.cumsum(sizes)
            offs = ends - sizes
            route_t = route.T
            e12 = route_t[0:2].astype(jnp.int32)
            off12 = sum(jnp.where(e12 == e, offs[e], 0) for e in range(N_EXPERTS))
            pos2 = off12 + route_t[4:6].astype(jnp.int32)
            r_pad = (2 * t_all // TR + N_EXPERTS) * TR
            tile_start = jnp.arange(r_pad // TR, dtype=jnp.int32) * TR
            tile_expert = jnp.minimum(jnp.sum(tile_start[:, None] >= ends[None, :], axis=1),
                                      N_EXPERTS - 1).astype(jnp.int32)
            tile_valid = (tile_start < ends[-1]).astype(jnp.int32)
            tail_start = ends[-1] + jnp.arange(N_EXPERTS, dtype=jnp.int32) * TR
            fill_start = jnp.concatenate([ends - TR, tail_start]).astype(jnp.int32)
            fill_on = jnp.concatenate([sizes > 0, tail_start < r_pad]).astype(jnp.int32)
            xs = _dispatch(jnp.where(fill_on > 0, fill_start, 0), fill_on, _tile_pos(pos2, TM), h2, r_pad)
            y = _expert_ffn(tile_expert, tile_valid, xs, w_gate_e[0], w_up_e[0], w_down_e[0])
            y_p, y_s = _combine(_tile_pos(pos2, TMC), x1, mod_g, route, y, tp)

    return (y_p.reshape(bp, sp, d), y_s.reshape(bs, ts, d), a_k_p, a_v_p, a_k_s, a_v_s,
            b_k_p, b_v_p, b_k_s, b_v_s)
```

```python
import functools

import numpy as np
import jax
import jax.numpy as jnp
from jax import lax
from jax.experimental import pallas as pl
from jax.experimental.pallas import tpu as pltpu

F32 = jnp.float32
BF16 = jnp.bfloat16

CHUNK = 64
N_HEADS = 16
HEAD_DIM = 64
N_PAST_CHUNKS = 8
BAND_PAST = N_PAST_CHUNKS * CHUNK
REL_CLIP = 128
N_EXPERTS = 8
EPS = 1e-6
NEG_INF = -1e30

LANES = 128
VMEM_LIMIT = 56 * 1024 * 1024

MOD_SHIFT1, MOD_SCALE1, MOD_GATE1, MOD_SHIFT2, MOD_SCALE2, MOD_GATE2 = range(6)

GROUP = 32
TM = 512
FC = 1792
TQ = 128
TK = 128
NP = N_HEADS // 2
BAND_KEYS = BAND_PAST + TQ
BAND_UNROLL = 8
BAND_ROWS = 32
SB_PG = 2
SB_TILES = 4
SB_STOP = -90.0
TR = 512
TMC = 512


def _cparams(sem):
    return pltpu.CompilerParams(dimension_semantics=sem, vmem_limit_bytes=VMEM_LIMIT)


def _dot(a, b):
    return jnp.dot(a, b, preferred_element_type=F32)


def _dot_nt(a, b):
    return lax.dot_general(a, b, (((1,), (1,)), ((), ())), preferred_element_type=F32)


def _split_hi_lo(x):
    hi = x.astype(BF16)
    lo = (x - hi.astype(F32)).astype(BF16)
    return hi, lo


def _silu(x):
    return x * (1.0 / (1.0 + jnp.exp(-x)))


def _lane_lt64(shape):
    return lax.broadcasted_iota(jnp.int32, shape, len(shape) - 1) < HEAD_DIM


def _group_rows(vec_rows, rows):
    g, d = vec_rows.shape
    return jnp.broadcast_to(vec_rows[:, None, :], (g, rows // g, d)).reshape(rows, d)


def _mod_block(i, n_prompt_tiles, tiles_per_batch):
    return jnp.where(i >= n_prompt_tiles, 0, 1 + i // tiles_per_batch)


def _mod_norm(x, g, shift_g, scale_g):
    rows = x.shape[0]
    ms = jnp.mean(x * x, axis=-1, keepdims=True)
    y = x * lax.rsqrt(ms + EPS) * g
    return y * (1.0 + _group_rows(scale_g, rows)) + _group_rows(shift_g, rows)


def _stack_heads(q2):
    m = _lane_lt64(q2.shape)
    z = jnp.zeros_like(q2)
    return jnp.concatenate([jnp.where(m, q2, z), jnp.where(m, z, q2)], axis=0)


def _pv_feature_major(pb, vt):
    r = pb.shape[0] // 2
    both = _dot_nt(pb, vt)
    return jnp.where(_lane_lt64((r, LANES)), both[:r], both[r:])


def _pv_row_major(pb, v2):
    r = pb.shape[0] // 2
    both = _dot(pb, v2)
    return jnp.where(_lane_lt64((r, LANES)), both[:r], both[r:])


def _ada_kernel(c_ref, w_ref, b_ref, o_ref):
    s = _silu(c_ref[...]).astype(BF16)
    o_ref[0] = _dot(s, w_ref[0].astype(BF16)) + b_ref[0]


def _ada(c_all, w_ada, b_ada):
    depth, d, n = w_ada.shape
    nb = c_all.shape[0]
    tn = 1536
    return pl.pallas_call(
        _ada_kernel,
        grid=(depth, n // tn),
        in_specs=[
            pl.BlockSpec((nb, d), lambda l, j: (0, 0)),
            pl.BlockSpec((1, d, tn), lambda l, j: (l, 0, j)),
            pl.BlockSpec((1, 1, tn), lambda l, j: (l, 0, j)),
        ],
        out_specs=pl.BlockSpec((1, nb, tn), lambda l, j: (l, 0, j)),
        out_shape=jax.ShapeDtypeStruct((depth, nb, n), F32),
        compiler_params=_cparams(("arbitrary", "arbitrary")),
        name="ada_modulation",
    )(c_all, w_ada, b_ada.reshape(depth, 1, n))


def _head_mean_sq_rows(x, bd_ref):
    hi, lo = _split_hi_lo(x * x)
    w = bd_ref.shape[0]
    cols = []
    for c in range(x.shape[1] // w):
        sl = slice(c * w, (c + 1) * w)
        cols.append(_dot(hi[:, sl], bd_ref[...]) + _dot(lo[:, sl], bd_ref[...]))
    return jnp.concatenate(cols, axis=1)


def _head_norm_cols(xt, gain_col):
    d, c = xt.shape
    x3 = xt.reshape(N_HEADS, HEAD_DIM, c)
    ms = jnp.mean(x3 * x3, axis=1, keepdims=True)
    return (x3 * lax.rsqrt(ms + EPS)).reshape(d, c) * gain_col


def _project(x, g_ref, sh_ref, sc_ref, wq_ref, wkvt_ref, qg_ref, kg_ref, bd_ref, qk_norm):
    d = x.shape[1]
    h = _mod_norm(x, g_ref[...], sh_ref[...], sc_ref[...]).astype(BF16)
    q = _dot(h, wq_ref[...])
    kvt = _dot_nt(wkvt_ref[...], h)
    kt, vt = kvt[:d], kvt[d:]
    if qk_norm:
        q = q * lax.rsqrt(_head_mean_sq_rows(q, bd_ref) + EPS) * qg_ref[...]
        kt = _head_norm_cols(kt, kg_ref[...])
    return (q * (HEAD_DIM ** -0.5)).astype(BF16), kt, vt


def _qkv_prompt_kernel(x_ref, g_ref, sh_ref, sc_ref, wq_ref, wkvt_ref, qg_ref, kg_ref, bd_ref,
                       q_ref, ktb_ref, vtb_ref, ktf_ref, vtf_ref, *, qk_norm, tail_only):
    qs, kt, vt = _project(x_ref[...], g_ref, sh_ref, sc_ref, wq_ref, wkvt_ref, qg_ref, kg_ref, bd_ref,
                          qk_norm)
    for p in range(NP):
        q_ref[p] = qs[:, p * LANES:(p + 1) * LANES]
    ktb_ref[0] = kt.astype(BF16)
    vtb_ref[0] = vt.astype(BF16)
    if tail_only:
        @pl.when((pl.program_id(0) + 1) % tail_only == 0)
        def _():
            ktf_ref[0] = kt
            vtf_ref[0] = vt
    else:
        ktf_ref[0] = kt
        vtf_ref[0] = vt


def _qkv_prompt(x, g, mod_g, wq, wkvt, qg_row, kg_col, bd, n_batch, s_len, qk_norm):
    d = x.shape[1]
    tpb = s_len // TM
    gpt = TM // GROUP
    row = lambda i: (i, 0)
    const = lambda i: (0, 0)
    shift_g = scale_g = mod_g
    fmaj = lambda i: (i // tpb, 0, i % tpb)
    if qk_norm:
        f_spec = pl.BlockSpec((1, d, TM), lambda i: (i // tpb, 0, 0))
        f_shape = jax.ShapeDtypeStruct((n_batch, d, TM), F32)
    else:
        f_spec = pl.BlockSpec((1, d, TM), fmaj)
        f_shape = jax.ShapeDtypeStruct((n_batch, d, s_len), F32)
    return pl.pallas_call(
        functools.partial(_qkv_prompt_kernel, qk_norm=qk_norm, tail_only=tpb if qk_norm else 0),
        grid=(n_batch * tpb,),
        in_specs=[
            pl.BlockSpec((TM, d), row),
            pl.BlockSpec((1, d), const),
            pl.BlockSpec((gpt, d), lambda i: (_mod_block(i, n_batch * tpb, tpb), MOD_SHIFT1)),
            pl.BlockSpec((gpt, d), lambda i: (_mod_block(i, n_batch * tpb, tpb), MOD_SCALE1)),
            pl.BlockSpec((d, d), const),
            pl.BlockSpec((2 * d, d), const),
            pl.BlockSpec((1, d), const),
            pl.BlockSpec((d, 1), const),
            pl.BlockSpec(bd.shape, const),
        ],
        out_specs=[
            pl.BlockSpec((NP, TM, LANES), lambda i: (0, i, 0)),
            pl.BlockSpec((1, d, TM), fmaj),
            pl.BlockSpec((1, d, TM), fmaj),
            f_spec, f_spec,
        ],
        out_shape=[
            jax.ShapeDtypeStruct((NP, n_batch * s_len, LANES), BF16),
            jax.ShapeDtypeStruct((n_batch, d, s_len), BF16),
            jax.ShapeDtypeStruct((n_batch, d, s_len), BF16),
            f_shape, f_shape,
        ],
        compiler_params=_cparams(("arbitrary",)),
        name="qkv_prompt_qknorm" if qk_norm else "qkv_prompt",
    )(x, g, shift_g, scale_g, wq, wkvt, qg_row, kg_col, bd)


def _qkv_sample_kernel(x_ref, g_ref, sh_ref, sc_ref, wq_ref, wkvt_ref, qg_ref, kg_ref, bd_ref,
                       q_ref, kv_ref, kf_ref, vf_ref, *, qk_norm):
    qs, kt, vt = _project(x_ref[...], g_ref, sh_ref, sc_ref, wq_ref, wkvt_ref, qg_ref, kg_ref, bd_ref,
                          qk_norm)
    k, v = kt.T, vt.T
    if qk_norm:
        kf_ref[...] = kt
        vf_ref[...] = vt
    else:
        kf_ref[...] = k
        vf_ref[...] = v
    kb, vb = k.astype(BF16), v.astype(BF16)
    for p in range(NP):
        sl = slice(p * LANES, (p + 1) * LANES)
        q_ref[p] = qs[:, sl]
        kv_ref[p] = kb[:, sl]
        kv_ref[NP + p] = vb[:, sl]


def _qkv_sample(x, row_block, g, mod_g, wq, wkvt, qg_row, kg_col, bd, qk_norm):
    d = x.shape[1]
    gpt = TM // GROUP
    row = lambda i: (row_block, 0)
    const = lambda i: (0, 0)
    shift_g = scale_g = mod_g
    f_shape = jax.ShapeDtypeStruct((d, TM) if qk_norm else (TM, d), F32)
    return pl.pallas_call(
        functools.partial(_qkv_sample_kernel, qk_norm=qk_norm),
        grid=(1,),
        in_specs=[
            pl.BlockSpec((TM, d), row),
            pl.BlockSpec((1, d), const),
            pl.BlockSpec((gpt, d), lambda i: (0, MOD_SHIFT1)),
            pl.BlockSpec((gpt, d), lambda i: (0, MOD_SCALE1)),
            pl.BlockSpec((d, d), const),
            pl.BlockSpec((2 * d, d), const),
            pl.BlockSpec((1, d), const),
            pl.BlockSpec((d, 1), const),
            pl.BlockSpec(bd.shape, const),
        ],
        out_specs=[
            pl.BlockSpec((NP, TM, LANES), lambda i: (0, 0, 0)),
            pl.BlockSpec((2 * NP, TM, LANES), lambda i: (0, 0, 0)),
            pl.BlockSpec(f_shape.shape, const),
            pl.BlockSpec(f_shape.shape, const),
        ],
        out_shape=[
            jax.ShapeDtypeStruct((NP, TM, LANES), BF16),
            jax.ShapeDtypeStruct((2 * NP, TM, LANES), BF16),
            f_shape, f_shape,
        ],
        compiler_params=_cparams(("arbitrary",)),
        name="qkv_sample_qknorm" if qk_norm else "qkv_sample",
    )(x, g, shift_g, scale_g, wq, wkvt, qg_row, kg_col, bd)


def _band_prompt_kernel(q_ref, kt_ref, vt_ref, bias_ref, o_ref):
    s_len = q_ref.shape[1]
    back = BAND_PAST // TK

    def tiles(n, carry):
        idx = [n * BAND_UNROLL + u for u in range(BAND_UNROLL)]
        row0 = [pl.multiple_of(i * TQ, TQ) for i in idx]
        key0 = [pl.multiple_of(jnp.maximum(i - back, 0) * TK, TK) for i in idx]
        col0 = [pl.multiple_of(jnp.maximum(back - i, 0) * TK, TK) for i in idx]
        scores = [_dot(_stack_heads(q_ref[0, pl.ds(row0[u], TQ), :]), kt_ref[0, :, pl.ds(key0[u], BAND_KEYS)])
                  for u in range(BAND_UNROLL)]
        for u in range(BAND_UNROLL):
            ps, ls = [], []
            for h in range(2):
                for r0 in range(0, TQ, BAND_ROWS):
                    s = (scores[u][h * TQ + r0:h * TQ + r0 + BAND_ROWS]
                         + bias_ref[h, r0:r0 + BAND_ROWS, pl.ds(col0[u], BAND_KEYS)])
                    p = jnp.exp(s - s.max(axis=1, keepdims=True))
                    ls.append(p.sum(axis=1, keepdims=True))
                    ps.append(p.astype(BF16))
            l = jnp.concatenate(ls, axis=0)
            acc = _pv_feature_major(jnp.concatenate(ps, axis=0), vt_ref[0, :, pl.ds(key0[u], BAND_KEYS)])
            linv = jnp.where(_lane_lt64((TQ, LANES)), 1.0 / l[:TQ], 1.0 / l[TQ:])
            o_ref[0, pl.ds(row0[u], TQ), :] = (acc * linv).astype(BF16)
        return carry

    lax.fori_loop(0, s_len // (BAND_UNROLL * TQ), tiles, 0)


def _band_prompt(q_pm, ktb, vtb, bias_p, n_batch, s_len):
    return pl.pallas_call(
        _band_prompt_kernel,
        grid=(n_batch, NP),
        in_specs=[
            pl.BlockSpec((1, s_len, LANES), lambda b, p: (p, b, 0)),
            pl.BlockSpec((1, LANES, s_len), lambda b, p: (b, p, 0)),
            pl.BlockSpec((1, LANES, s_len), lambda b, p: (b, p, 0)),
            pl.BlockSpec((2, TQ, bias_p.shape[2]), lambda b, p: (p, 0, 0)),
        ],
        out_specs=pl.BlockSpec((1, s_len, LANES), lambda b, p: (p, b, 0)),
        out_shape=jax.ShapeDtypeStruct((NP, n_batch * s_len, LANES), BF16),
        compiler_params=_cparams(("arbitrary", "arbitrary")),
        name="band_attention_prompt",
    )(q_pm, ktb, vtb, bias_p)


def _band_sample_kernel(q_ref, kv_ref, kc_ref, vc_ref, bc_ref, bn_ref, knew_ref, vnew_ref,
                        o_ref, kwin_ref, vwin_ref):
    ts = q_ref.shape[1]
    win = kc_ref.shape[2]
    new_lanes = lax.broadcasted_iota(jnp.int32, knew_ref.shape[1:], 1) >= LANES - ts
    for c_ref, n_ref, w_ref in ((kc_ref, knew_ref, kwin_ref), (vc_ref, vnew_ref, vwin_ref)):
        rolled = pltpu.roll(c_ref[0], win - ts, 1)
        w_ref[0, :, :win - LANES] = rolled[:, :win - LANES]
        w_ref[0, :, win - LANES:] = jnp.where(new_lanes, n_ref[0], rolled[:, win - LANES:])
    for p in range(NP):
        rows = slice(p * LANES, (p + 1) * LANES)
        qm = _stack_heads(q_ref[p])
        kn, vn = kv_ref[p], kv_ref[NP + p]
        s1 = _dot(qm, kc_ref[0, rows, :].astype(BF16)) + bc_ref[2 * p:2 * p + 2].reshape(2 * ts, -1)
        s2 = _dot_nt(qm, kn) + bn_ref[2 * p:2 * p + 2].reshape(2 * ts, ts)
        m = jnp.maximum(s1.max(axis=1, keepdims=True), s2.max(axis=1, keepdims=True))
        p1 = jnp.exp(s1 - m)
        p2 = jnp.exp(s2 - m)
        l = p1.sum(axis=1, keepdims=True) + p2.sum(axis=1, keepdims=True)
        acc = (_pv_feature_major(p1.astype(BF16), vc_ref[0, rows, :].astype(BF16))
               + _pv_row_major(p2.astype(BF16), vn))
        linv = jnp.where(_lane_lt64((ts, LANES)), 1.0 / l[:ts], 1.0 / l[ts:])
        o_ref[p] = (acc * linv).astype(BF16)


def _band_sample(q_s, kv_s, cache_kt, cache_vt, bias_c, bias_n, new_kt, new_vt, n_streams, ts):
    d, win = cache_kt.shape[1], cache_kt.shape[2]
    stream = lambda s: (s, 0, 0)
    return pl.pallas_call(
        _band_sample_kernel,
        grid=(n_streams,),
        in_specs=[
            pl.BlockSpec((NP, ts, LANES), lambda s: (0, s, 0)),
            pl.BlockSpec((2 * NP, ts, LANES), lambda s: (0, s, 0)),
            pl.BlockSpec((1, d, win), stream),
            pl.BlockSpec((1, d, win), stream),
            pl.BlockSpec(bias_c.shape, lambda s: (0, 0, 0)),
            pl.BlockSpec(bias_n.shape, lambda s: (0, 0, 0)),
            pl.BlockSpec((1, d, LANES), stream),
            pl.BlockSpec((1, d, LANES), stream),
        ],
        out_specs=[
            pl.BlockSpec((NP, ts, LANES), lambda s: (0, s, 0)),
            pl.BlockSpec((1, d, win), stream),
            pl.BlockSpec((1, d, win), stream),
        ],
        out_shape=[
            jax.ShapeDtypeStruct(q_s.shape, BF16),
            jax.ShapeDtypeStruct(cache_kt.shape, F32),
            jax.ShapeDtypeStruct(cache_vt.shape, F32),
        ],
        compiler_params=_cparams(("arbitrary",)),
        name="band_attention_sample",
    )(q_s, kv_s, cache_kt, cache_vt, bias_c, bias_n, new_kt, new_vt)


def _sb_weights(z, upper, carry, mask):
    sp = jnp.maximum(z, 0.0) + jnp.log(1.0 + jnp.exp(-jnp.abs(z)))
    lk = -sp
    if mask is not None:
        lk = jnp.where(mask, lk, 0.0)
    hi, lo = _split_hi_lo(lk)
    after = _dot(hi, upper) + _dot(lo, upper) + carry
    a = jnp.exp(z - sp + after)
    if mask is not None:
        a = jnp.where(mask, a, 0.0)
    return a.astype(BF16), carry + lk.sum(axis=1, keepdims=True)


def _sb_weights_multi(zs, upper_ones, carries, masks):
    c = zs[0].shape[1]
    halves = []
    for z, mask in zip(zs, masks):
        nz = -z
        lk = jnp.minimum(nz, 0.0) - jnp.log(1.0 + jnp.exp(jnp.minimum(z, nz)))
        if mask is not None:
            lk = jnp.where(mask, lk, 0.0)
        halves.append(jnp.concatenate(_split_hi_lo(lk), axis=1))
    sums = [_dot(hl, upper_ones) for hl in halves]
    weights, new_carries = [], []
    for z, sm, carry, mask in zip(zs, sums, carries, masks):
        a = jnp.exp(z + (sm[:, :c] + carry))
        if mask is not None:
            a = jnp.where(mask, a, 0.0)
        weights.append(a.astype(BF16))
        new_carries.append(carry + sm[:, c:])
    return weights, new_carries


def _upper_tri(n):
    return (lax.broadcasted_iota(jnp.int32, (n, n), 0)
            > lax.broadcasted_iota(jnp.int32, (n, n), 1)).astype(BF16)


def _upper_tri_ones(n):
    incl = (lax.broadcasted_iota(jnp.int32, (n, n), 0)
            >= lax.broadcasted_iota(jnp.int32, (n, n), 1)).astype(BF16)
    half = jnp.concatenate([incl, jnp.ones((n, n), BF16)], axis=1)
    return jnp.concatenate([half, half], axis=0)


def _causal_mask(rows, cols):
    r = lax.broadcasted_iota(jnp.int32, (2 * rows, cols), 0)
    r = jnp.where(r >= rows, r - rows, r)
    c = lax.broadcasted_iota(jnp.int32, (2 * rows, cols), 1)
    return c < r


def _any_live(carries):
    m = carries[0].max()
    for c in carries[1:]:
        m = jnp.maximum(m, c.max())
    return (m >= SB_STOP).astype(jnp.int32)


def _sb_prompt_kernel(q_ref, kt_ref, vt_ref, o_ref):
    npg, s_len = q_ref.shape[0], q_ref.shape[1]
    upper_ones = _upper_tri_ones(TK)
    diag_mask = _causal_mask(TQ, TK)
    all_true = diag_mask | True

    units = [(g, t) for t in range(SB_TILES) for g in range(npg)]

    def blocks(qms, blks, carries, accs, masks):
        cols = [pl.ds(pl.multiple_of(b * TK, TK), TK) for b in blks]
        rows = [slice(g * LANES, (g + 1) * LANES) for g, _ in units]
        zs = [_dot(qms[u], kt_ref[0, rows[u], cols[u]]) for u in range(len(units))]
        ws, carries = _sb_weights_multi(zs, upper_ones, carries, masks)
        accs = [accs[u] + _pv_feature_major(ws[u], vt_ref[0, rows[u], cols[u]]) for u in range(len(units))]
        return carries, accs

    def tile_group(i2, c0):
        first = SB_TILES * i2
        qms = [_stack_heads(q_ref[g, pl.ds(pl.multiple_of((first + t) * TQ, TQ), TQ), :]) for g, t in units]
        carries, accs = blocks(qms, [first + t for _, t in units],
                               [jnp.zeros((2 * TQ, TK), F32)] * len(units),
                               [jnp.zeros((TQ, LANES), F32)] * len(units), [diag_mask] * len(units))

        def cond(st):
            return jnp.logical_and(first + SB_TILES - 1 - st[0] >= 0, st[1] > 0)

        def body(st):
            k = st[0]
            blks = [first + t - k for _, t in units]
            masks = [None if t == SB_TILES - 1 else jnp.logical_and(all_true, blks[u] >= 0)
                     for u, (_, t) in enumerate(units)]
            cs, as_ = blocks(qms, [jnp.maximum(b, 0) for b in blks], list(st[2]), list(st[3]), masks)
            return (k + 1, _any_live(cs), tuple(cs), tuple(as_))

        st = lax.while_loop(cond, body, (1, _any_live(carries), tuple(carries), tuple(accs)))
        for u, (g, t) in enumerate(units):
            row0 = pl.multiple_of((first + t) * TQ, TQ)
            o_ref[g, pl.ds(row0, TQ), :] = st[3][u].astype(BF16)
        return c0

    lax.fori_loop(0, s_len // (SB_TILES * TQ), tile_group, 0)


def _sb_prompt(q_pm, ktb, vtb, n_batch, s_len):
    ng = NP // SB_PG
    return pl.pallas_call(
        _sb_prompt_kernel,
        grid=(n_batch, ng),
        in_specs=[
            pl.BlockSpec((SB_PG, s_len, LANES), lambda b, g: (g, b, 0)),
            pl.BlockSpec((1, SB_PG * LANES, s_len), lambda b, g: (b, g, 0)),
            pl.BlockSpec((1, SB_PG * LANES, s_len), lambda b, g: (b, g, 0)),
        ],
        out_specs=pl.BlockSpec((SB_PG, s_len, LANES), lambda b, g: (g, b, 0)),
        out_shape=jax.ShapeDtypeStruct((NP, n_batch * s_len, LANES), BF16),
        compiler_params=_cparams(("arbitrary", "arbitrary")),
        name="stick_breaking_prompt",
    )(q_pm, ktb, vtb)


def _sb_sample_kernel(q_ref, kv_ref, kc_hbm, vc_hbm, o_ref, kbuf, vbuf, sem):
    s = pl.program_id(0)
    ts = q_ref.shape[1]
    nblk = kc_hbm.shape[2] // TK
    upper_ones = _upper_tri_ones(TK)

    def copies(j):
        slot = j % 2
        cols = pl.ds(pl.multiple_of(j * TK, TK), TK)
        return (pltpu.make_async_copy(kc_hbm.at[s, :, cols], kbuf.at[slot], sem.at[0, slot]),
                pltpu.make_async_copy(vc_hbm.at[s, :, cols], vbuf.at[slot], sem.at[1, slot]))

    def start(j):
        for c in copies(j):
            c.start()

    def wait(j):
        for c in copies(j):
            c.wait()

    start(nblk - 1)

    qms, carries, accs = [], [], []
    mask = _causal_mask(ts, ts)
    upper_n = _upper_tri(ts)
    for p in range(NP):
        qm = _stack_heads(q_ref[p])
        ab, c = _sb_weights(_dot_nt(qm, kv_ref[p]), upper_n, jnp.zeros((2 * ts, 1), F32), mask)
        qms.append(qm)
        carries.append(jnp.broadcast_to(c, (2 * ts, TK)))
        accs.append(_pv_row_major(ab, kv_ref[NP + p]))

    def cond(st):
        return jnp.logical_and(st[0] >= 0, st[1] > 0)

    def body(st):
        j = st[0]
        wait(j)

        @pl.when(j > 0)
        def _():
            start(j - 1)

        slot = j % 2
        rows = [slice(p * LANES, (p + 1) * LANES) for p in range(NP)]
        zs = [_dot(qms[p], kbuf[slot, rows[p], :].astype(BF16)) for p in range(NP)]
        ws, cs = _sb_weights_multi(zs, upper_ones, list(st[2]), [None] * NP)
        as_ = [st[3][p] + _pv_feature_major(ws[p], vbuf[slot, rows[p], :].astype(BF16)) for p in range(NP)]
        return (j - 1, _any_live(cs), tuple(cs), tuple(as_))

    st = lax.while_loop(cond, body, (nblk - 1, _any_live(carries), tuple(carries), tuple(accs)))

    @pl.when(st[0] >= 0)
    def _():
        wait(st[0])

    for p in range(NP):
        o_ref[p] = st[3][p].astype(BF16)


def _sb_sample(q_s, kv_s, cache_kt, cache_vt, n_streams, ts):
    d = cache_kt.shape[1]
    return pl.pallas_call(
        _sb_sample_kernel,
        grid=(n_streams,),
        in_specs=[
            pl.BlockSpec((NP, ts, LANES), lambda s: (0, s, 0)),
            pl.BlockSpec((2 * NP, ts, LANES), lambda s: (0, s, 0)),
            pl.BlockSpec(memory_space=pl.ANY),
            pl.BlockSpec(memory_space=pl.ANY),
        ],
        out_specs=pl.BlockSpec((NP, ts, LANES), lambda s: (0, s, 0)),
        out_shape=jax.ShapeDtypeStruct(q_s.shape, BF16),
        scratch_shapes=[pltpu.VMEM((2, d, TK), F32), pltpu.VMEM((2, d, TK), F32),
                        pltpu.SemaphoreType.DMA((2, 2))],
        compiler_params=_cparams(("arbitrary",)),
        name="stick_breaking_sample",
    )(q_s, kv_s, cache_kt, cache_vt)


def _pick(is_sample, prompt_ref, sample_ref):
    return jnp.where(is_sample, sample_ref[...], prompt_ref[...])


def _attn_out(x, o, wo_ref, g1_ref, n2_ref, sh_ref, sc_ref):
    rows = x.shape[0]
    attn = jnp.concatenate([o[p] for p in range(NP)], axis=1)
    x1 = x + _group_rows(g1_ref[...], rows) * _dot(attn, wo_ref[...])
    h2 = _mod_norm(x1, n2_ref[...], sh_ref[...], sc_ref[...])
    return x1, h2


def _post_dense_kernel(xp_ref, xs_ref, op_ref, os_ref, wo_ref, g1_ref, n2_ref, sh_ref, sc_ref, g2_ref,
                       wg_ref, wu_ref, wd_ref, out_ref, x1_sc, h2_sc, acc_sc):
    j = pl.program_id(1)

    @pl.when(j == 0)
    def _():
        is_sample = pl.program_id(0) == pl.num_programs(0) - 1
        x1, h2 = _attn_out(_pick(is_sample, xp_ref, xs_ref), _pick(is_sample, op_ref, os_ref),
                           wo_ref, g1_ref, n2_ref, sh_ref, sc_ref)
        x1_sc[...] = x1
        h2_sc[...] = h2.astype(BF16)
        acc_sc[...] = jnp.zeros_like(acc_sc)

    h2 = h2_sc[...]
    act = (_silu(_dot(h2, wg_ref[...])) * _dot(h2, wu_ref[...])).astype(BF16)
    acc_sc[...] += _dot(act, wd_ref[...])

    @pl.when(j == pl.num_programs(1) - 1)
    def _():
        out_ref[...] = x1_sc[...] + _group_rows(g2_ref[...], x1_sc.shape[0]) * acc_sc[...]


def _post_dense(x_p, x_s, o_p, o_s, wo, mod_g, n2, wg, wu, wd, tpb):
    tp, d = x_p.shape
    ntp = tp // TM
    mrow = lambda i: _mod_block(i, ntp, tpb)
    f = wg.shape[1]
    gpt = TM // GROUP
    row = lambda i, j: (i, 0)
    prow = lambda i, j: (jnp.minimum(i, ntp - 1), 0)
    const = lambda i, j: (0, 0)
    g1_g = sh_g = sc_g = g2_g = mod_g
    return pl.pallas_call(
        _post_dense_kernel,
        grid=(ntp + 1, f // FC),
        in_specs=[
            pl.BlockSpec((TM, d), prow),
            pl.BlockSpec((TM, d), const),
            pl.BlockSpec((NP, TM, LANES), lambda i, j: (0, jnp.minimum(i, ntp - 1), 0)),
            pl.BlockSpec((NP, TM, LANES), lambda i, j: (0, 0, 0)),
            pl.BlockSpec((d, d), const),
            pl.BlockSpec((gpt, d), lambda i, j: (mrow(i), MOD_GATE1)),
            pl.BlockSpec((1, d), const),
            pl.BlockSpec((gpt, d), lambda i, j: (mrow(i), MOD_SHIFT2)),
            pl.BlockSpec((gpt, d), lambda i, j: (mrow(i), MOD_SCALE2)),
            pl.BlockSpec((gpt, d), lambda i, j: (mrow(i), MOD_GATE2)),
            pl.BlockSpec((d, FC), lambda i, j: (0, j)),
            pl.BlockSpec((d, FC), lambda i, j: (0, j)),
            pl.BlockSpec((FC, d), lambda i, j: (j, 0)),
        ],
        out_specs=pl.BlockSpec((TM, d), row),
        out_shape=jax.ShapeDtypeStruct((tp + TM, d), F32),
        scratch_shapes=[pltpu.VMEM((TM, d), F32), pltpu.VMEM((TM, d), BF16), pltpu.VMEM((TM, d), F32)],
        compiler_params=_cparams(("arbitrary", "arbitrary")),
        name="attn_out_dense_ffn",
    )(x_p, x_s, o_p, o_s, wo, g1_g, n2, sh_g, sc_g, g2_g, wg, wu, wd)


def _post_router_kernel(x_ref, op_ref, os_ref, wo_ref, g1_ref, n2_ref, sh_ref, sc_ref, wr_ref,
                        x1_ref, h2_ref, route_ref, cnt_ref, cnt_sc):
    i = pl.program_id(0)
    rows = x_ref.shape[0]

    @pl.when(i == 0)
    def _():
        cnt_sc[...] = jnp.zeros_like(cnt_sc)

    is_sample = i == pl.num_programs(0) - 1
    x1, h2 = _attn_out(x_ref[...], _pick(is_sample, op_ref, os_ref), wo_ref, g1_ref, n2_ref, sh_ref, sc_ref)
    x1_ref[...] = x1
    h2_ref[...] = h2

    hh, hl = _split_hi_lo(h2)
    wh, wl = _split_hi_lo(wr_ref[...])
    logits = _dot(hh, wh) + _dot(hl, wh) + _dot(hh, wl)
    lane = lax.broadcasted_iota(jnp.int32, logits.shape, 1).astype(F32)
    logits = jnp.where(lane < N_EXPERTS, logits, -jnp.inf)
    v1 = logits.max(axis=1, keepdims=True)
    i1 = jnp.where(logits == v1, lane, float(LANES)).min(axis=1, keepdims=True)
    rest = jnp.where(lane == i1, -jnp.inf, logits)
    v2 = rest.max(axis=1, keepdims=True)
    i2 = jnp.where(rest == v2, lane, float(LANES)).min(axis=1, keepdims=True)
    w1 = 1.0 / (1.0 + jnp.exp(v2 - v1))
    w2 = 1.0 - w1

    oh1 = (lane == i1).astype(F32)
    oh2 = (lane == i2).astype(F32)
    cnt = oh1 + oh2
    lower = (lax.broadcasted_iota(jnp.int32, (rows, rows), 0)
             > lax.broadcasted_iota(jnp.int32, (rows, rows), 1)).astype(BF16)
    rank = _dot(lower, cnt.astype(BF16)) + cnt_sc[...]
    r1 = (oh1 * rank).sum(axis=1, keepdims=True)
    r2 = (oh2 * rank).sum(axis=1, keepdims=True)
    cnt_sc[...] += cnt.sum(axis=0, keepdims=True)
    cnt_ref[...] = cnt_sc[...]

    out = jnp.where(lane == 0, i1, 0.0)
    out = jnp.where(lane == 1, i2, out)
    out = jnp.where(lane == 2, w1, out)
    out = jnp.where(lane == 3, w2, out)
    out = jnp.where(lane == 4, r1, out)
    out = jnp.where(lane == 5, r2, out)
    route_ref[...] = out[:, :route_ref.shape[1]]


def _post_router(x_all, o_p, o_s, wo, mod_g, n2, wr_pad, tpb):
    t, d = x_all.shape
    ntp = t // TM - 1
    mrow = lambda i: _mod_block(i, ntp, tpb)
    gpt = TM // GROUP
    row = lambda i: (i, 0)
    const = lambda i: (0, 0)
    g1_g = sh_g = sc_g = mod_g
    return pl.pallas_call(
        _post_router_kernel,
        grid=(t // TM,),
        in_specs=[
            pl.BlockSpec((TM, d), row),
            pl.BlockSpec((NP, TM, LANES), lambda i: (0, jnp.minimum(i, ntp - 1), 0)),
            pl.BlockSpec((NP, TM, LANES), lambda i: (0, 0, 0)),
            pl.BlockSpec((d, d), const),
            pl.BlockSpec((gpt, d), lambda i: (mrow(i), MOD_GATE1)),
            pl.BlockSpec((1, d), const),
            pl.BlockSpec((gpt, d), lambda i: (mrow(i), MOD_SHIFT2)),
            pl.BlockSpec((gpt, d), lambda i: (mrow(i), MOD_SCALE2)),
            pl.BlockSpec((d, LANES), const),
        ],
        out_specs=[
            pl.BlockSpec((TM, d), row),
            pl.BlockSpec((TM, d), row),
            pl.BlockSpec((TM, 8), row),
            pl.BlockSpec((1, LANES), const),
        ],
        out_shape=[
            jax.ShapeDtypeStruct((t, d), F32),
            jax.ShapeDtypeStruct((t, d), F32),
            jax.ShapeDtypeStruct((t, 8), F32),
            jax.ShapeDtypeStruct((1, LANES), F32),
        ],
        scratch_shapes=[pltpu.VMEM((1, LANES), F32)],
        compiler_params=_cparams(("arbitrary",)),
        name="attn_out_router",
    )(x_all, o_p, o_s, wo, g1_g, n2, sh_g, sc_g, wr_pad)


def _dispatch_kernel(fill_start_ref, fill_on_ref, pos_ref, h2_ref, xs_ref, zero_buf, sem):
    rows = h2_ref.shape[0]

    @pl.when(pl.program_id(0) == 0)
    def _():
        zero_buf[...] = jnp.zeros_like(zero_buf)

        def fill(k):
            return pltpu.make_async_copy(
                zero_buf, xs_ref.at[pl.ds(pl.multiple_of(fill_start_ref[k], TR), TR)], sem)

        for k in range(fill_on_ref.shape[0]):
            @pl.when(fill_on_ref[k] > 0)
            def _():
                fill(k).start()
        for k in range(fill_on_ref.shape[0]):
            @pl.when(fill_on_ref[k] > 0)
            def _():
                fill(k).wait()

    def issue(r, c):
        src = h2_ref.at[pl.ds(r, 1)]
        pltpu.make_async_copy(src, xs_ref.at[pl.ds(pos_ref[0, 0, r], 1)], sem).start()
        pltpu.make_async_copy(src, xs_ref.at[pl.ds(pos_ref[0, 0, rows + r], 1)], sem).start()
        return c

    lax.fori_loop(0, rows, issue, 0, unroll=8)
    for _ in range(2):
        pltpu.make_async_copy(h2_ref, xs_ref.at[pl.ds(0, rows)], sem).wait()


def _dispatch(fill_start, fill_on, pos_tiles, h2, r_pad):
    t, d = h2.shape
    grid_spec = pltpu.PrefetchScalarGridSpec(
        num_scalar_prefetch=2,
        grid=(t // TM,),
        in_specs=[
            pl.BlockSpec((1, 1, 2 * TM), lambda i, fs, fo: (i, 0, 0), memory_space=pltpu.SMEM),
            pl.BlockSpec((TM, d), lambda i, fs, fo: (i, 0)),
        ],
        out_specs=pl.BlockSpec(memory_space=pl.ANY),
        scratch_shapes=[pltpu.VMEM((TR, d), F32), pltpu.SemaphoreType.DMA],
    )
    return pl.pallas_call(
        _dispatch_kernel,
        grid_spec=grid_spec,
        out_shape=jax.ShapeDtypeStruct((r_pad, d), F32),
        compiler_params=_cparams(("arbitrary",)),
        name="moe_dispatch",
    )(fill_start, fill_on, pos_tiles, h2)


def _expert_ffn_kernel(te_ref, tv_ref, xs_ref, wg_ref, wu_ref, wd_ref, y_ref, xb_sc, acc_sc):
    r = pl.program_id(0)
    j = pl.program_id(1)
    valid = tv_ref[r] > 0

    @pl.when(j == 0)
    def _():
        xb_sc[...] = xs_ref[...].astype(BF16)
        acc_sc[...] = jnp.zeros_like(acc_sc)

    @pl.when(valid)
    def _():
        x = xb_sc[...]
        act = (_silu(_dot(x, wg_ref[0])) * _dot(x, wu_ref[0])).astype(BF16)
        acc_sc[...] += _dot(act, wd_ref[0])

    @pl.when(j == pl.num_programs(1) - 1)
    def _():
        y_ref[...] = acc_sc[...]


def _expert_ffn(tile_expert, tile_valid, xs, wg, wu, wd):
    r_pad, d = xs.shape
    f = wg.shape[2]
    grid_spec = pltpu.PrefetchScalarGridSpec(
        num_scalar_prefetch=2,
        grid=(r_pad // TR, f // FC),
        in_specs=[
            pl.BlockSpec((TR, d), lambda r, j, te, tv: (r, 0)),
            pl.BlockSpec((1, d, FC), lambda r, j, te, tv: (te[r], 0, j)),
            pl.BlockSpec((1, d, FC), lambda r, j, te, tv: (te[r], 0, j)),
            pl.BlockSpec((1, FC, d), lambda r, j, te, tv: (te[r], j, 0)),
        ],
        out_specs=pl.BlockSpec((TR, d), lambda r, j, te, tv: (r, 0)),
        scratch_shapes=[pltpu.VMEM((TR, d), BF16), pltpu.VMEM((TR, d), F32)],
    )
    return pl.pallas_call(
        _expert_ffn_kernel,
        grid_spec=grid_spec,
        out_shape=jax.ShapeDtypeStruct((r_pad, d), F32),
        compiler_params=_cparams(("arbitrary", "arbitrary")),
        name="moe_expert_ffn",
    )(tile_expert, tile_valid, xs, wg, wu, wd)


def _combine_kernel(pos_ref, posn_ref, x1_ref, g2_ref, route_ref, y_ref, outp_ref, outs_ref, buf, sem, *,
                    ntp):
    i = pl.program_id(0)
    rows = x1_ref.shape[0]
    slot = i % 2

    def gather(p_ref, s):
        def issue(r, c):
            pltpu.make_async_copy(y_ref.at[pl.ds(p_ref[0, 0, r], 1)], buf.at[s, 0, pl.ds(r, 1)],
                                  sem.at[s]).start()
            pltpu.make_async_copy(y_ref.at[pl.ds(p_ref[0, 0, rows + r], 1)], buf.at[s, 1, pl.ds(r, 1)],
                                  sem.at[s]).start()
            return c
        lax.fori_loop(0, rows, issue, 0, unroll=8)

    @pl.when(i == 0)
    def _():
        gather(pos_ref, 0)

    @pl.when(i + 1 < pl.num_programs(0))
    def _():
        gather(posn_ref, 1 - slot)

    for k in range(2):
        pltpu.make_async_copy(y_ref.at[pl.ds(0, rows)], buf.at[slot, k], sem.at[slot]).wait()
    route = route_ref[...]
    moe = route[:, 2:3] * buf[slot, 0] + route[:, 3:4] * buf[slot, 1]
    res = x1_ref[...] + _group_rows(g2_ref[...], rows) * moe

    @pl.when(i < ntp)
    def _():
        outp_ref[...] = res

    @pl.when(i >= ntp)
    def _():
        outs_ref[...] = res


def _combine(pos_tiles, x1, mod_g, route, y, tp, tpb):
    t, d = x1.shape
    ntp = tp // TMC
    gpt = TMC // GROUP
    assert TMC == TM
    row = lambda i: (i, 0)
    last = t // TMC - 1
    g2_g = mod_g
    return pl.pallas_call(
        functools.partial(_combine_kernel, ntp=ntp),
        grid=(t // TMC,),
        in_specs=[
            pl.BlockSpec((1, 1, 2 * TMC), lambda i: (i, 0, 0), memory_space=pltpu.SMEM),
            pl.BlockSpec((1, 1, 2 * TMC), lambda i: (jnp.minimum(i + 1, last), 0, 0), memory_space=pltpu.SMEM),
            pl.BlockSpec((TMC, d), row),
            pl.BlockSpec((gpt, d), lambda i: (_mod_block(i, ntp, tpb), MOD_GATE2)),
            pl.BlockSpec((TMC, 8), row),
            pl.BlockSpec(memory_space=pl.ANY),
        ],
        out_specs=[
            pl.BlockSpec((TMC, d), lambda i: (jnp.minimum(i, ntp - 1), 0)),
            pl.BlockSpec((TMC, d), lambda i: (jnp.maximum(i - ntp, 0), 0)),
        ],
        out_shape=[jax.ShapeDtypeStruct((tp, d), F32), jax.ShapeDtypeStruct((t - tp, d), F32)],
        scratch_shapes=[pltpu.VMEM((2, 2, TMC, d), F32), pltpu.SemaphoreType.DMA((2,))],
        compiler_params=_cparams(("arbitrary",)),
        name="moe_combine",
    )(pos_tiles, pos_tiles, x1, g2_g, route, y)


def _band_valid(q_pos, k_pos):
    qc = q_pos // CHUNK
    kc = k_pos // CHUNK
    return (k_pos[None, :] >= 0) & (kc[None, :] <= qc[:, None]) & (kc[None, :] >= qc[:, None] - N_PAST_CHUNKS)


def _band_bias(rel_bias, q_pos, k_pos):
    nq, nk = len(q_pos), len(k_pos)
    assert np.all(np.diff(q_pos) == 1) and np.all(np.diff(k_pos) == 1)
    period = nq + nk
    dist = (q_pos[0] - k_pos[0]) + (nq - 1) - np.arange(period)
    vec = rel_bias.astype(F32)[:, np.clip(dist, -REL_CLIP, REL_CLIP) + REL_CLIP]
    flat = jnp.tile(jnp.roll(vec, -(nq - 1), axis=1), (1, nq))[:, :nq * (period - 1)]
    tab = flat.reshape(-1, nq, period - 1)[:, :, :nk]
    return jnp.where(jnp.asarray(_band_valid(q_pos, k_pos))[None], tab, NEG_INF)


def _tile_pos(pos2, tile):
    t = pos2.shape[1]
    return pos2.reshape(2, t // tile, tile).transpose(1, 0, 2).reshape(t // tile, 1, 2 * tile)


def _feature_major(cache):
    n, length = cache.shape[0], cache.shape[1]
    return cache.transpose(0, 2, 3, 1).reshape(n, N_HEADS * HEAD_DIM, length)


def _token_major(xt):
    n, _, length = xt.shape
    return xt.reshape(n, N_HEADS, HEAD_DIM, length).transpose(0, 3, 1, 2)


def kernel(x_prompt, x_sample, cache_a_k, cache_a_v, cache_b_k, cache_b_v, c_prompt, c_sample, w_qkv, w_o, norm1_g, norm2_g, w_ada, b_ada, q_norm_g, k_norm_g, rel_bias, w_gate_d, w_up_d, w_down_d, w_router, w_gate_e, w_up_e, w_down_e):
    bp, sp, d = x_prompt.shape
    bs, ts, _ = x_sample.shape
    past_len = cache_b_k.shape[2]
    win = cache_a_k.shape[2]
    depth = w_qkv.shape[0]
    assert depth == 2 and d == N_HEADS * HEAD_DIM and ts == GROUP
    tp, tsm = bp * sp, bs * ts
    t_all = tp + tsm
    assert sp % TM == 0 and tsm == TM and sp % (BAND_UNROLL * TQ) == 0 and t_all % TMC == 0 and tp % TMC == 0
    assert win == BAND_PAST == TM and sp >= BAND_KEYS and past_len % TK == 0 and past_len >= win

    x_p = x_prompt.reshape(tp, d)
    x_s = x_sample.reshape(tsm, d)

    mod = _ada(jnp.concatenate([c_prompt, c_sample], axis=0), w_ada, b_ada)
    mod = jnp.concatenate([mod[:, bp:], jnp.repeat(mod[:, :bp], TM // GROUP, axis=1)], axis=1)
    tpb = sp // TM

    bd = (np.arange(256)[:, None] // HEAD_DIM == np.arange(256)[None, :] // HEAD_DIM)
    bd = jnp.asarray(bd.astype(np.float32) / HEAD_DIM, BF16)
    ones_row = jnp.ones((1, d), F32)
    ones_col = jnp.ones((d, 1), F32)

    for i in range(depth):
        mod_g = mod[i]
        band = i % 2 == 0
        wq = w_qkv[i][:, :d].astype(BF16)
        wkvt = w_qkv[i][:, d:].T.astype(BF16)
        qg = jnp.tile(q_norm_g[0], N_HEADS)[None] if band else ones_row
        kg = jnp.tile(k_norm_g[0], N_HEADS)[:, None] if band else ones_col
        n1 = norm1_g[i][None]
        if band:
            q_p, ktb, vtb, ktf, vtf = _qkv_prompt(x_p, n1, mod_g, wq, wkvt, qg, kg, bd, bp, sp, True)
            q_s, kv_s, ksf, vsf = _qkv_sample(x_s, 0, n1, mod_g, wq, wkvt, qg, kg, bd, True)
            q_pos = np.arange(TQ) + BAND_PAST
            bias_p = _band_bias(rel_bias[0], q_pos, np.arange(BAND_KEYS))
            bias_p = jnp.pad(bias_p, ((0, 0), (0, 0), (0, BAND_PAST)), constant_values=NEG_INF)
            o_p = _band_prompt(q_p, ktb, vtb, bias_p, bp, sp)
            pos_s = past_len + np.arange(ts)
            pos_win = past_len - win + np.arange(win)
            ca_k, ca_v = _feature_major(cache_a_k[0]), _feature_major(cache_a_v[0])
            right = ((0, 0), (0, 0), (LANES - ts, 0))
            new_k = jnp.pad(ksf.reshape(d, bs, ts).transpose(1, 0, 2), right)
            new_v = jnp.pad(vsf.reshape(d, bs, ts).transpose(1, 0, 2), right)
            o_s, k_win, v_win = _band_sample(q_s, kv_s, ca_k, ca_v, _band_bias(rel_bias[0], pos_s, pos_win),
                                             _band_bias(rel_bias[0], pos_s, pos_s), new_k, new_v, bs, ts)
            a_k_p = _token_major(ktf)[None]
            a_v_p = _token_major(vtf)[None]
            a_k_s = _token_major(k_win)[None]
            a_v_s = _token_major(v_win)[None]
            x_all = _post_dense(x_p, x_s, o_p, o_s, w_o[i].astype(BF16), mod_g, norm2_g[i][None],
                                w_gate_d[0].astype(BF16), w_up_d[0].astype(BF16), w_down_d[0].astype(BF16), tpb)
        else:
            q_p, ktb, vtb, ktf, vtf = _qkv_prompt(x_all, n1, mod_g, wq, wkvt, qg, kg, bd, bp, sp, False)
            q_s, kv_s, ksf, vsf = _qkv_sample(x_all, tp // TM, n1, mod_g, wq, wkvt, qg, kg, bd, False)
            o_p = _sb_prompt(q_p, ktb, vtb, bp, sp)
            o_s = _sb_sample(q_s, kv_s, _feature_major(cache_b_k[0]), _feature_major(cache_b_v[0]), bs, ts)
            b_k_p = _token_major(ktf)[None]
            b_v_p = _token_major(vtf)[None]
            b_k_s = ksf.reshape(1, bs, ts, N_HEADS, HEAD_DIM)
            b_v_s = vsf.reshape(1, bs, ts, N_HEADS, HEAD_DIM)
            wr_pad = jnp.pad(w_router[0], ((0, 0), (0, LANES - N_EXPERTS)))
            x1, h2, route, cnt = _post_router(x_all, o_p, o_s, w_o[i].astype(BF16), mod_g, norm2_g[i][None],
                                              wr_pad, tpb)
            counts = cnt[0, :N_EXPERTS].astype(jnp.int32)
            sizes = (counts + TR - 1) // TR * TR
            ends = jnp.cumsum(sizes)
            offs = ends - sizes
            route_t = route.T
            e12 = route_t[0:2].astype(jnp.int32)
            off12 = sum(jnp.where(e12 == e, offs[e], 0) for e in range(N_EXPERTS))
            pos2 = off12 + route_t[4:6].astype(jnp.int32)
            r_pad = (2 * t_all // TR + N_EXPERTS) * TR
            tile_start = jnp.arange(r_pad // TR, dtype=jnp.int32) * TR
            tile_expert = jnp.minimum(jnp.sum(tile_start[:, None] >= ends[None, :], axis=1),
                                      N_EXPERTS - 1).astype(jnp.int32)
            tile_valid = (tile_start < ends[-1]).astype(jnp.int32)
            tail_start = ends[-1] + jnp.arange(N_EXPERTS, dtype=jnp.int32) * TR
            fill_start = jnp.concatenate([ends - TR, tail_start]).astype(jnp.int32)
            fill_on = jnp.concatenate([sizes > 0, tail_start < r_pad]).astype(jnp.int32)
            xs = _dispatch(jnp.where(fill_on > 0, fill_start, 0), fill_on, _tile_pos(pos2, TM), h2, r_pad)
            y = _expert_ffn(tile_expert, tile_valid, xs, w_gate_e[0].astype(BF16),
                            w_up_e[0].astype(BF16), w_down_e[0].astype(BF16))
            y_p, y_s = _combine(_tile_pos(pos2, TMC), x1, mod_g, route, y, tp, tpb)

    return (y_p.reshape(bp, sp, d), y_s.reshape(bs, ts, d), a_k_p, a_v_p, a_k_s, a_v_s,
            b_k_p, b_v_p, b_k_s, b_v_s)
```

```python
import functools

import numpy as np
import jax
import jax.numpy as jnp
from jax import lax
from jax.experimental import pallas as pl
from jax.experimental.pallas import tpu as pltpu

F32 = jnp.float32
BF16 = jnp.bfloat16

CHUNK = 64
N_HEADS = 16
HEAD_DIM = 64
N_PAST_CHUNKS = 8
BAND_PAST = N_PAST_CHUNKS * CHUNK
REL_CLIP = 128
N_EXPERTS = 8
EPS = 1e-6
NEG_INF = -1e30

LANES = 128
VMEM_LIMIT = 56 * 1024 * 1024

MOD_SHIFT1, MOD_SCALE1, MOD_GATE1, MOD_SHIFT2, MOD_SCALE2, MOD_GATE2 = range(6)

GROUP = 32
TM = 512
FC = 1792
TQ = 128
TK = 128
NP = N_HEADS // 2
BAND_KEYS = BAND_PAST + TQ
BAND_UNROLL = 8
BAND_ROWS = 32
SB_PG = 2
SB_TILES = 4
SB_STOP = -90.0
CAST_BLOCKS = 128
TR = 512
TMC = 512


def _cparams(sem):
    return pltpu.CompilerParams(dimension_semantics=sem, vmem_limit_bytes=VMEM_LIMIT)


def _dot(a, b):
    return jnp.dot(a, b, preferred_element_type=F32)


def _dot_nt(a, b):
    return lax.dot_general(a, b, (((1,), (1,)), ((), ())), preferred_element_type=F32)


def _split_hi_lo(x):
    hi = x.astype(BF16)
    lo = (x - hi.astype(F32)).astype(BF16)
    return hi, lo


def _silu(x):
    return x * (1.0 / (1.0 + jnp.exp(-x)))


def _lane_lt64(shape):
    return lax.broadcasted_iota(jnp.int32, shape, len(shape) - 1) < HEAD_DIM


def _group_rows(vec_rows, rows):
    g, d = vec_rows.shape
    return jnp.broadcast_to(vec_rows[:, None, :], (g, rows // g, d)).reshape(rows, d)


def _mod_block(i, n_prompt_tiles, tiles_per_batch):
    return jnp.where(i >= n_prompt_tiles, 0, 1 + i // tiles_per_batch)


def _mod_norm(x, g, shift_g, scale_g):
    rows = x.shape[0]
    ms = jnp.mean(x * x, axis=-1, keepdims=True)
    y = x * lax.rsqrt(ms + EPS) * g
    return y * (1.0 + _group_rows(scale_g, rows)) + _group_rows(shift_g, rows)


def _stack_heads(q2):
    m = _lane_lt64(q2.shape)
    z = jnp.zeros_like(q2)
    return jnp.concatenate([jnp.where(m, q2, z), jnp.where(m, z, q2)], axis=0)


def _pv_feature_major(pb, vt):
    r = pb.shape[0] // 2
    both = _dot_nt(pb, vt)
    return jnp.where(_lane_lt64((r, LANES)), both[:r], both[r:])


def _pv_row_major(pb, v2):
    r = pb.shape[0] // 2
    both = _dot(pb, v2)
    return jnp.where(_lane_lt64((r, LANES)), both[:r], both[r:])


def _ada_kernel(c_ref, w_ref, b_ref, o_ref):
    s = _silu(c_ref[...]).astype(BF16)
    o_ref[0] = _dot(s, w_ref[0].astype(BF16)) + b_ref[0]


def _ada(c_all, w_ada, b_ada):
    depth, d, n = w_ada.shape
    nb = c_all.shape[0]
    tn = 1536
    return pl.pallas_call(
        _ada_kernel,
        grid=(depth, n // tn),
        in_specs=[
            pl.BlockSpec((nb, d), lambda l, j: (0, 0)),
            pl.BlockSpec((1, d, tn), lambda l, j: (l, 0, j)),
            pl.BlockSpec((1, 1, tn), lambda l, j: (l, 0, j)),
        ],
        out_specs=pl.BlockSpec((1, nb, tn), lambda l, j: (l, 0, j)),
        out_shape=jax.ShapeDtypeStruct((depth, nb, n), F32),
        compiler_params=_cparams(("arbitrary", "arbitrary")),
        name="ada_modulation",
    )(c_all, w_ada, b_ada.reshape(depth, 1, n))


def _head_mean_sq_rows(x, bd_ref):
    hi, lo = _split_hi_lo(x * x)
    w = bd_ref.shape[0]
    cols = []
    for c in range(x.shape[1] // w):
        sl = slice(c * w, (c + 1) * w)
        cols.append(_dot(hi[:, sl], bd_ref[...]) + _dot(lo[:, sl], bd_ref[...]))
    return jnp.concatenate(cols, axis=1)


def _head_norm_cols(xt, gain_col):
    d, c = xt.shape
    x3 = xt.reshape(N_HEADS, HEAD_DIM, c)
    ms = jnp.mean(x3 * x3, axis=1, keepdims=True)
    return (x3 * lax.rsqrt(ms + EPS)).reshape(d, c) * gain_col


def _project(x, g_ref, sh_ref, sc_ref, wq_ref, wkvt_ref, qg_ref, kg_ref, bd_ref, qk_norm):
    d = x.shape[1]
    h = _mod_norm(x, g_ref[...], sh_ref[...], sc_ref[...]).astype(BF16)
    q = _dot(h, wq_ref[...])
    kvt = _dot_nt(wkvt_ref[...], h)
    kt, vt = kvt[:d], kvt[d:]
    if qk_norm:
        q = q * lax.rsqrt(_head_mean_sq_rows(q, bd_ref) + EPS) * qg_ref[...]
        kt = _head_norm_cols(kt, kg_ref[...])
    return (q * (HEAD_DIM ** -0.5)).astype(BF16), kt, vt


def _qkv_prompt_kernel(x_ref, g_ref, sh_ref, sc_ref, wq_ref, wkvt_ref, qg_ref, kg_ref, bd_ref,
                       q_ref, ktb_ref, vtb_ref, ktf_ref, vtf_ref, *, qk_norm, tail_only):
    qs, kt, vt = _project(x_ref[...], g_ref, sh_ref, sc_ref, wq_ref, wkvt_ref, qg_ref, kg_ref, bd_ref,
                          qk_norm)
    for p in range(NP):
        q_ref[p] = qs[:, p * LANES:(p + 1) * LANES]
    ktb_ref[0] = kt.astype(BF16)
    vtb_ref[0] = vt.astype(BF16)
    if tail_only:
        @pl.when((pl.program_id(0) + 1) % tail_only == 0)
        def _():
            ktf_ref[0] = kt
            vtf_ref[0] = vt
    else:
        ktf_ref[0] = kt
        vtf_ref[0] = vt


def _qkv_prompt(x, g, mod_g, wq, wkvt, qg_row, kg_col, bd, n_batch, s_len, qk_norm):
    d = x.shape[1]
    tpb = s_len // TM
    gpt = TM // GROUP
    row = lambda i: (i, 0)
    const = lambda i: (0, 0)
    shift_g = scale_g = mod_g
    fmaj = lambda i: (i // tpb, 0, i % tpb)
    if qk_norm:
        f_spec = pl.BlockSpec((1, d, TM), lambda i: (i // tpb, 0, 0))
        f_shape = jax.ShapeDtypeStruct((n_batch, d, TM), F32)
    else:
        f_spec = pl.BlockSpec((1, d, TM), fmaj)
        f_shape = jax.ShapeDtypeStruct((n_batch, d, s_len), F32)
    return pl.pallas_call(
        functools.partial(_qkv_prompt_kernel, qk_norm=qk_norm, tail_only=tpb if qk_norm else 0),
        grid=(n_batch * tpb,),
        in_specs=[
            pl.BlockSpec((TM, d), row),
            pl.BlockSpec((1, d), const),
            pl.BlockSpec((gpt, d), lambda i: (_mod_block(i, n_batch * tpb, tpb), MOD_SHIFT1)),
            pl.BlockSpec((gpt, d), lambda i: (_mod_block(i, n_batch * tpb, tpb), MOD_SCALE1)),
            pl.BlockSpec((d, d), const),
            pl.BlockSpec((2 * d, d), const),
            pl.BlockSpec((1, d), const),
            pl.BlockSpec((d, 1), const),
            pl.BlockSpec(bd.shape, const),
        ],
        out_specs=[
            pl.BlockSpec((NP, TM, LANES), lambda i: (0, i, 0)),
            pl.BlockSpec((1, d, TM), fmaj),
            pl.BlockSpec((1, d, TM), fmaj),
            f_spec, f_spec,
        ],
        out_shape=[
            jax.ShapeDtypeStruct((NP, n_batch * s_len, LANES), BF16),
            jax.ShapeDtypeStruct((n_batch, d, s_len), BF16),
            jax.ShapeDtypeStruct((n_batch, d, s_len), BF16),
            f_shape, f_shape,
        ],
        compiler_params=_cparams(("arbitrary",)),
        name="qkv_prompt_qknorm" if qk_norm else "qkv_prompt",
    )(x, g, shift_g, scale_g, wq, wkvt, qg_row, kg_col, bd)


def _qkv_sample_kernel(x_ref, g_ref, sh_ref, sc_ref, wq_ref, wkvt_ref, qg_ref, kg_ref, bd_ref,
                       q_ref, kv_ref, kf_ref, vf_ref, *, qk_norm):
    qs, kt, vt = _project(x_ref[...], g_ref, sh_ref, sc_ref, wq_ref, wkvt_ref, qg_ref, kg_ref, bd_ref,
                          qk_norm)
    k, v = kt.T, vt.T
    if qk_norm:
        kf_ref[...] = kt
        vf_ref[...] = vt
    else:
        kf_ref[...] = k
        vf_ref[...] = v
    kb, vb = k.astype(BF16), v.astype(BF16)
    for p in range(NP):
        sl = slice(p * LANES, (p + 1) * LANES)
        q_ref[p] = qs[:, sl]
        kv_ref[p] = kb[:, sl]
        kv_ref[NP + p] = vb[:, sl]


def _qkv_sample(x, row_block, g, mod_g, wq, wkvt, qg_row, kg_col, bd, qk_norm):
    d = x.shape[1]
    gpt = TM // GROUP
    row = lambda i: (row_block, 0)
    const = lambda i: (0, 0)
    shift_g = scale_g = mod_g
    f_shape = jax.ShapeDtypeStruct((d, TM) if qk_norm else (TM, d), F32)
    return pl.pallas_call(
        functools.partial(_qkv_sample_kernel, qk_norm=qk_norm),
        grid=(1,),
        in_specs=[
            pl.BlockSpec((TM, d), row),
            pl.BlockSpec((1, d), const),
            pl.BlockSpec((gpt, d), lambda i: (0, MOD_SHIFT1)),
            pl.BlockSpec((gpt, d), lambda i: (0, MOD_SCALE1)),
            pl.BlockSpec((d, d), const),
            pl.BlockSpec((2 * d, d), const),
            pl.BlockSpec((1, d), const),
            pl.BlockSpec((d, 1), const),
            pl.BlockSpec(bd.shape, const),
        ],
        out_specs=[
            pl.BlockSpec((NP, TM, LANES), lambda i: (0, 0, 0)),
            pl.BlockSpec((2 * NP, TM, LANES), lambda i: (0, 0, 0)),
            pl.BlockSpec(f_shape.shape, const),
            pl.BlockSpec(f_shape.shape, const),
        ],
        out_shape=[
            jax.ShapeDtypeStruct((NP, TM, LANES), BF16),
            jax.ShapeDtypeStruct((2 * NP, TM, LANES), BF16),
            f_shape, f_shape,
        ],
        compiler_params=_cparams(("arbitrary",)),
        name="qkv_sample_qknorm" if qk_norm else "qkv_sample",
    )(x, g, shift_g, scale_g, wq, wkvt, qg_row, kg_col, bd)


def _band_prompt_kernel(q_ref, kt_ref, vt_ref, bias_ref, o_ref):
    s_len = q_ref.shape[1]
    back = BAND_PAST // TK

    def tiles(n, carry):
        idx = [n * BAND_UNROLL + u for u in range(BAND_UNROLL)]
        row0 = [pl.multiple_of(i * TQ, TQ) for i in idx]
        key0 = [pl.multiple_of(jnp.maximum(i - back, 0) * TK, TK) for i in idx]
        col0 = [pl.multiple_of(jnp.maximum(back - i, 0) * TK, TK) for i in idx]
        scores = [_dot(_stack_heads(q_ref[0, pl.ds(row0[u], TQ), :]), kt_ref[0, :, pl.ds(key0[u], BAND_KEYS)])
                  for u in range(BAND_UNROLL)]
        for u in range(BAND_UNROLL):
            ps, ls = [], []
            for h in range(2):
                for r0 in range(0, TQ, BAND_ROWS):
                    s = (scores[u][h * TQ + r0:h * TQ + r0 + BAND_ROWS]
                         + bias_ref[h, r0:r0 + BAND_ROWS, pl.ds(col0[u], BAND_KEYS)])
                    p = jnp.exp(s - s.max(axis=1, keepdims=True))
                    ls.append(p.sum(axis=1, keepdims=True))
                    ps.append(p.astype(BF16))
            l = jnp.concatenate(ls, axis=0)
            acc = _pv_feature_major(jnp.concatenate(ps, axis=0), vt_ref[0, :, pl.ds(key0[u], BAND_KEYS)])
            linv = jnp.where(_lane_lt64((TQ, LANES)), 1.0 / l[:TQ], 1.0 / l[TQ:])
            o_ref[0, pl.ds(row0[u], TQ), :] = (acc * linv).astype(BF16)
        return carry

    lax.fori_loop(0, s_len // (BAND_UNROLL * TQ), tiles, 0)


def _band_prompt(q_pm, ktb, vtb, bias_p, n_batch, s_len):
    return pl.pallas_call(
        _band_prompt_kernel,
        grid=(n_batch, NP),
        in_specs=[
            pl.BlockSpec((1, s_len, LANES), lambda b, p: (p, b, 0)),
            pl.BlockSpec((1, LANES, s_len), lambda b, p: (b, p, 0)),
            pl.BlockSpec((1, LANES, s_len), lambda b, p: (b, p, 0)),
            pl.BlockSpec((2, TQ, bias_p.shape[2]), lambda b, p: (p, 0, 0)),
        ],
        out_specs=pl.BlockSpec((1, s_len, LANES), lambda b, p: (p, b, 0)),
        out_shape=jax.ShapeDtypeStruct((NP, n_batch * s_len, LANES), BF16),
        compiler_params=_cparams(("arbitrary", "arbitrary")),
        name="band_attention_prompt",
    )(q_pm, ktb, vtb, bias_p)


def _band_sample_kernel(q_ref, kv_ref, kc_ref, vc_ref, bc_ref, bn_ref, knew_ref, vnew_ref,
                        o_ref, kwin_ref, vwin_ref):
    ts = q_ref.shape[1]
    win = kc_ref.shape[2]
    new_lanes = lax.broadcasted_iota(jnp.int32, knew_ref.shape[1:], 1) >= LANES - ts
    for c_ref, n_ref, w_ref in ((kc_ref, knew_ref, kwin_ref), (vc_ref, vnew_ref, vwin_ref)):
        rolled = pltpu.roll(c_ref[0], win - ts, 1)
        w_ref[0, :, :win - LANES] = rolled[:, :win - LANES]
        w_ref[0, :, win - LANES:] = jnp.where(new_lanes, n_ref[0], rolled[:, win - LANES:])
    for p in range(NP):
        rows = slice(p * LANES, (p + 1) * LANES)
        qm = _stack_heads(q_ref[p])
        kn, vn = kv_ref[p], kv_ref[NP + p]
        s1 = _dot(qm, kc_ref[0, rows, :].astype(BF16)) + bc_ref[2 * p:2 * p + 2].reshape(2 * ts, -1)
        s2 = _dot_nt(qm, kn) + bn_ref[2 * p:2 * p + 2].reshape(2 * ts, ts)
        m = jnp.maximum(s1.max(axis=1, keepdims=True), s2.max(axis=1, keepdims=True))
        p1 = jnp.exp(s1 - m)
        p2 = jnp.exp(s2 - m)
        l = p1.sum(axis=1, keepdims=True) + p2.sum(axis=1, keepdims=True)
        acc = (_pv_feature_major(p1.astype(BF16), vc_ref[0, rows, :].astype(BF16))
               + _pv_row_major(p2.astype(BF16), vn))
        linv = jnp.where(_lane_lt64((ts, LANES)), 1.0 / l[:ts], 1.0 / l[ts:])
        o_ref[p] = (acc * linv).astype(BF16)


def _band_sample(q_s, kv_s, cache_kt, cache_vt, bias_c, bias_n, new_kt, new_vt, n_streams, ts):
    d, win = cache_kt.shape[1], cache_kt.shape[2]
    stream = lambda s: (s, 0, 0)
    return pl.pallas_call(
        _band_sample_kernel,
        grid=(n_streams,),
        in_specs=[
            pl.BlockSpec((NP, ts, LANES), lambda s: (0, s, 0)),
            pl.BlockSpec((2 * NP, ts, LANES), lambda s: (0, s, 0)),
            pl.BlockSpec((1, d, win), stream),
            pl.BlockSpec((1, d, win), stream),
            pl.BlockSpec(bias_c.shape, lambda s: (0, 0, 0)),
            pl.BlockSpec(bias_n.shape, lambda s: (0, 0, 0)),
            pl.BlockSpec((1, d, LANES), stream),
            pl.BlockSpec((1, d, LANES), stream),
        ],
        out_specs=[
            pl.BlockSpec((NP, ts, LANES), lambda s: (0, s, 0)),
            pl.BlockSpec((1, d, win), stream),
            pl.BlockSpec((1, d, win), stream),
        ],
        out_shape=[
            jax.ShapeDtypeStruct(q_s.shape, BF16),
            jax.ShapeDtypeStruct(cache_kt.shape, F32),
            jax.ShapeDtypeStruct(cache_vt.shape, F32),
        ],
        compiler_params=_cparams(("arbitrary",)),
        name="band_attention_sample",
    )(q_s, kv_s, cache_kt, cache_vt, bias_c, bias_n, new_kt, new_vt)


def _sb_weights(z, upper, carry, mask):
    sp = jnp.maximum(z, 0.0) + jnp.log(1.0 + jnp.exp(-jnp.abs(z)))
    lk = -sp
    if mask is not None:
        lk = jnp.where(mask, lk, 0.0)
    hi, lo = _split_hi_lo(lk)
    after = _dot(hi, upper) + _dot(lo, upper) + carry
    a = jnp.exp(z - sp + after)
    if mask is not None:
        a = jnp.where(mask, a, 0.0)
    return a.astype(BF16), carry + lk.sum(axis=1, keepdims=True)


def _sb_weights_multi(zs, upper_ones, carries, masks):
    c = zs[0].shape[1]
    halves = []
    for z, mask in zip(zs, masks):
        nz = -z
        lk = jnp.minimum(nz, 0.0) - jnp.log(1.0 + jnp.exp(jnp.minimum(z, nz)))
        if mask is not None:
            lk = jnp.where(mask, lk, 0.0)
        halves.append(jnp.concatenate(_split_hi_lo(lk), axis=1))
    sums = [_dot(hl, upper_ones) for hl in halves]
    weights, new_carries = [], []
    for z, sm, carry, mask in zip(zs, sums, carries, masks):
        a = jnp.exp(z + (sm[:, :c] + carry))
        if mask is not None:
            a = jnp.where(mask, a, 0.0)
        weights.append(a.astype(BF16))
        new_carries.append(carry + sm[:, c:])
    return weights, new_carries


def _upper_tri(n):
    return (lax.broadcasted_iota(jnp.int32, (n, n), 0)
            > lax.broadcasted_iota(jnp.int32, (n, n), 1)).astype(BF16)


def _upper_tri_ones(n):
    incl = (lax.broadcasted_iota(jnp.int32, (n, n), 0)
            >= lax.broadcasted_iota(jnp.int32, (n, n), 1)).astype(BF16)
    half = jnp.concatenate([incl, jnp.ones((n, n), BF16)], axis=1)
    return jnp.concatenate([half, half], axis=0)


def _causal_mask(rows, cols):
    r = lax.broadcasted_iota(jnp.int32, (2 * rows, cols), 0)
    r = jnp.where(r >= rows, r - rows, r)
    c = lax.broadcasted_iota(jnp.int32, (2 * rows, cols), 1)
    return c < r


def _any_live(carries):
    m = carries[0].max()
    for c in carries[1:]:
        m = jnp.maximum(m, c.max())
    return (m >= SB_STOP).astype(jnp.int32)


def _sb_prompt_kernel(q_ref, kt_ref, vt_ref, o_ref):
    npg, s_len = q_ref.shape[0], q_ref.shape[1]
    upper_ones = _upper_tri_ones(TK)
    diag_mask = _causal_mask(TQ, TK)
    all_true = diag_mask | True

    units = [(g, t) for t in range(SB_TILES) for g in range(npg)]

    def blocks(qms, blks, carries, accs, masks):
        cols = [pl.ds(pl.multiple_of(b * TK, TK), TK) for b in blks]
        rows = [slice(g * LANES, (g + 1) * LANES) for g, _ in units]
        zs = [_dot(qms[u], kt_ref[0, rows[u], cols[u]]) for u in range(len(units))]
        ws, carries = _sb_weights_multi(zs, upper_ones, carries, masks)
        accs = [accs[u] + _pv_feature_major(ws[u], vt_ref[0, rows[u], cols[u]]) for u in range(len(units))]
        return carries, accs

    def tile_group(i2, c0):
        first = SB_TILES * i2
        qms = [_stack_heads(q_ref[g, pl.ds(pl.multiple_of((first + t) * TQ, TQ), TQ), :]) for g, t in units]
        carries, accs = blocks(qms, [first + t for _, t in units],
                               [jnp.zeros((2 * TQ, TK), F32)] * len(units),
                               [jnp.zeros((TQ, LANES), F32)] * len(units), [diag_mask] * len(units))

        def cond(st):
            return jnp.logical_and(first + SB_TILES - 1 - st[0] >= 0, st[1] > 0)

        def body(st):
            k = st[0]
            blks = [first + t - k for _, t in units]
            masks = [None if t == SB_TILES - 1 else jnp.logical_and(all_true, blks[u] >= 0)
                     for u, (_, t) in enumerate(units)]
            cs, as_ = blocks(qms, [jnp.maximum(b, 0) for b in blks], list(st[2]), list(st[3]), masks)
            return (k + 1, _any_live(cs), tuple(cs), tuple(as_))

        st = lax.while_loop(cond, body, (1, _any_live(carries), tuple(carries), tuple(accs)))
        for u, (g, t) in enumerate(units):
            row0 = pl.multiple_of((first + t) * TQ, TQ)
            o_ref[g, pl.ds(row0, TQ), :] = st[3][u].astype(BF16)
        return c0

    lax.fori_loop(0, s_len // (SB_TILES * TQ), tile_group, 0)


def _sb_prompt(q_pm, ktb, vtb, n_batch, s_len):
    ng = NP // SB_PG
    return pl.pallas_call(
        _sb_prompt_kernel,
        grid=(n_batch, ng),
        in_specs=[
            pl.BlockSpec((SB_PG, s_len, LANES), lambda b, g: (g, b, 0)),
            pl.BlockSpec((1, SB_PG * LANES, s_len), lambda b, g: (b, g, 0)),
            pl.BlockSpec((1, SB_PG * LANES, s_len), lambda b, g: (b, g, 0)),
        ],
        out_specs=pl.BlockSpec((SB_PG, s_len, LANES), lambda b, g: (g, b, 0)),
        out_shape=jax.ShapeDtypeStruct((NP, n_batch * s_len, LANES), BF16),
        compiler_params=_cparams(("arbitrary", "arbitrary")),
        name="stick_breaking_prompt",
    )(q_pm, ktb, vtb)


def _sb_sample_kernel(q_ref, kv_ref, kc_hbm, vc_hbm, o_ref, kbuf, vbuf, sem):
    s = pl.program_id(0)
    ts = q_ref.shape[1]
    nblk = kc_hbm.shape[2] // TK
    upper_ones = _upper_tri_ones(TK)

    def copies(j):
        slot = j % 2
        cols = pl.ds(pl.multiple_of(j * TK, TK), TK)
        return (pltpu.make_async_copy(kc_hbm.at[s, :, cols], kbuf.at[slot], sem.at[0, slot]),
                pltpu.make_async_copy(vc_hbm.at[s, :, cols], vbuf.at[slot], sem.at[1, slot]))

    def start(j):
        for c in copies(j):
            c.start()

    def wait(j):
        for c in copies(j):
            c.wait()

    start(nblk - 1)

    qms, carries, accs = [], [], []
    mask = _causal_mask(ts, ts)
    upper_n = _upper_tri(ts)
    for p in range(NP):
        qm = _stack_heads(q_ref[p])
        ab, c = _sb_weights(_dot_nt(qm, kv_ref[p]), upper_n, jnp.zeros((2 * ts, 1), F32), mask)
        qms.append(qm)
        carries.append(jnp.broadcast_to(c, (2 * ts, TK)))
        accs.append(_pv_row_major(ab, kv_ref[NP + p]))

    def cond(st):
        return jnp.logical_and(st[0] >= 0, st[1] > 0)

    def body(st):
        j = st[0]
        wait(j)

        @pl.when(j > 0)
        def _():
            start(j - 1)

        slot = j % 2
        rows = [slice(p * LANES, (p + 1) * LANES) for p in range(NP)]
        zs = [_dot(qms[p], kbuf[slot, rows[p], :].astype(BF16)) for p in range(NP)]
        ws, cs = _sb_weights_multi(zs, upper_ones, list(st[2]), [None] * NP)
        as_ = [st[3][p] + _pv_feature_major(ws[p], vbuf[slot, rows[p], :].astype(BF16)) for p in range(NP)]
        return (j - 1, _any_live(cs), tuple(cs), tuple(as_))

    st = lax.while_loop(cond, body, (nblk - 1, _any_live(carries), tuple(carries), tuple(accs)))

    @pl.when(st[0] >= 0)
    def _():
        wait(st[0])

    for p in range(NP):
        o_ref[p] = st[3][p].astype(BF16)


def _sb_sample(q_s, kv_s, cache_kt, cache_vt, n_streams, ts):
    d = cache_kt.shape[1]
    return pl.pallas_call(
        _sb_sample_kernel,
        grid=(n_streams,),
        in_specs=[
            pl.BlockSpec((NP, ts, LANES), lambda s: (0, s, 0)),
            pl.BlockSpec((2 * NP, ts, LANES), lambda s: (0, s, 0)),
            pl.BlockSpec(memory_space=pl.ANY),
            pl.BlockSpec(memory_space=pl.ANY),
        ],
        out_specs=pl.BlockSpec((NP, ts, LANES), lambda s: (0, s, 0)),
        out_shape=jax.ShapeDtypeStruct(q_s.shape, BF16),
        scratch_shapes=[pltpu.VMEM((2, d, TK), F32), pltpu.VMEM((2, d, TK), F32),
                        pltpu.SemaphoreType.DMA((2, 2))],
        compiler_params=_cparams(("arbitrary",)),
        name="stick_breaking_sample",
    )(q_s, kv_s, cache_kt, cache_vt)


def _pick(is_sample, prompt_ref, sample_ref):
    return jnp.where(is_sample, sample_ref[...], prompt_ref[...])


def _attn_out(x, o, wo_ref, g1_ref, n2_ref, sh_ref, sc_ref):
    rows = x.shape[0]
    attn = jnp.concatenate([o[p] for p in range(NP)], axis=1)
    x1 = x + _group_rows(g1_ref[...], rows) * _dot(attn, wo_ref[...])
    h2 = _mod_norm(x1, n2_ref[...], sh_ref[...], sc_ref[...])
    return x1, h2


def _post_dense_kernel(xp_ref, xs_ref, op_ref, os_ref, wo_ref, g1_ref, n2_ref, sh_ref, sc_ref, g2_ref,
                       wg_ref, wu_ref, wd_ref, *rest):
    n_cast = (len(rest) - 4) // 2
    cast_in, out_ref, cast_out = rest[:n_cast], rest[n_cast], rest[n_cast + 1:2 * n_cast + 1]
    x1_sc, h2_sc, acc_sc = rest[2 * n_cast + 1:]
    j = pl.program_id(1)

    for src, dst in zip(cast_in, cast_out):
        dst[...] = src[...].astype(BF16)

    @pl.when(j == 0)
    def _():
        is_sample = pl.program_id(0) == pl.num_programs(0) - 1
        x1, h2 = _attn_out(_pick(is_sample, xp_ref, xs_ref), _pick(is_sample, op_ref, os_ref),
                           wo_ref, g1_ref, n2_ref, sh_ref, sc_ref)
        x1_sc[...] = x1
        h2_sc[...] = h2.astype(BF16)
        acc_sc[...] = jnp.zeros_like(acc_sc)

    h2 = h2_sc[...]
    act = (_silu(_dot(h2, wg_ref[...])) * _dot(h2, wu_ref[...])).astype(BF16)
    acc_sc[...] += _dot(act, wd_ref[...])

    @pl.when(j == pl.num_programs(1) - 1)
    def _():
        out_ref[...] = x1_sc[...] + _group_rows(g2_ref[...], x1_sc.shape[0]) * acc_sc[...]


def _post_dense(x_p, x_s, o_p, o_s, wo, mod_g, n2, wg, wu, wd, tpb, casts=()):
    tp, d = x_p.shape
    ntp = tp // TM
    mrow = lambda i: _mod_block(i, ntp, tpb)
    f = wg.shape[1]
    nj = f // FC
    gpt = TM // GROUP
    row = lambda i, j: (i, 0)
    prow = lambda i, j: (jnp.minimum(i, ntp - 1), 0)
    const = lambda i, j: (0, 0)
    g1_g = sh_g = sc_g = g2_g = mod_g
    assert not casts or (ntp + 1) * nj >= CAST_BLOCKS
    cast_blk = lambda i, j: (jnp.minimum(i * nj + j, CAST_BLOCKS - 1), 0)
    cast_specs = [pl.BlockSpec((c.shape[0] // CAST_BLOCKS, c.shape[1]), cast_blk) for c in casts]
    return pl.pallas_call(
        _post_dense_kernel,
        grid=(ntp + 1, nj),
        in_specs=[
            pl.BlockSpec((TM, d), prow),
            pl.BlockSpec((TM, d), const),
            pl.BlockSpec((NP, TM, LANES), lambda i, j: (0, jnp.minimum(i, ntp - 1), 0)),
            pl.BlockSpec((NP, TM, LANES), lambda i, j: (0, 0, 0)),
            pl.BlockSpec((d, d), const),
            pl.BlockSpec((gpt, d), lambda i, j: (mrow(i), MOD_GATE1)),
            pl.BlockSpec((1, d), const),
            pl.BlockSpec((gpt, d), lambda i, j: (mrow(i), MOD_SHIFT2)),
            pl.BlockSpec((gpt, d), lambda i, j: (mrow(i), MOD_SCALE2)),
            pl.BlockSpec((gpt, d), lambda i, j: (mrow(i), MOD_GATE2)),
            pl.BlockSpec((d, FC), lambda i, j: (0, j)),
            pl.BlockSpec((d, FC), lambda i, j: (0, j)),
            pl.BlockSpec((FC, d), lambda i, j: (j, 0)),
        ] + cast_specs,
        out_specs=[pl.BlockSpec((TM, d), row)] + cast_specs,
        out_shape=[jax.ShapeDtypeStruct((tp + TM, d), F32)]
        + [jax.ShapeDtypeStruct(c.shape, BF16) for c in casts],
        scratch_shapes=[pltpu.VMEM((TM, d), F32), pltpu.VMEM((TM, d), BF16), pltpu.VMEM((TM, d), F32)],
        compiler_params=_cparams(("arbitrary", "arbitrary")),
        name="attn_out_dense_ffn",
    )(x_p, x_s, o_p, o_s, wo, g1_g, n2, sh_g, sc_g, g2_g, wg, wu, wd, *casts)


def _post_router_kernel(x_ref, op_ref, os_ref, wo_ref, g1_ref, n2_ref, sh_ref, sc_ref, wr_ref,
                        x1_ref, h2_ref, route_ref, cnt_ref, cnt_sc):
    i = pl.program_id(0)
    rows = x_ref.shape[0]

    @pl.when(i == 0)
    def _():
        cnt_sc[...] = jnp.zeros_like(cnt_sc)

    is_sample = i == pl.num_programs(0) - 1
    x1, h2 = _attn_out(x_ref[...], _pick(is_sample, op_ref, os_ref), wo_ref, g1_ref, n2_ref, sh_ref, sc_ref)
    x1_ref[...] = x1
    h2_ref[...] = h2

    hh, hl = _split_hi_lo(h2)
    wh, wl = _split_hi_lo(wr_ref[...])
    logits = _dot(hh, wh) + _dot(hl, wh) + _dot(hh, wl)
    lane = lax.broadcasted_iota(jnp.int32, logits.shape, 1).astype(F32)
    logits = jnp.where(lane < N_EXPERTS, logits, -jnp.inf)
    v1 = logits.max(axis=1, keepdims=True)
    i1 = jnp.where(logits == v1, lane, float(LANES)).min(axis=1, keepdims=True)
    rest = jnp.where(lane == i1, -jnp.inf, logits)
    v2 = rest.max(axis=1, keepdims=True)
    i2 = jnp.where(rest == v2, lane, float(LANES)).min(axis=1, keepdims=True)
    w1 = 1.0 / (1.0 + jnp.exp(v2 - v1))
    w2 = 1.0 - w1

    oh1 = (lane == i1).astype(F32)
    oh2 = (lane == i2).astype(F32)
    cnt = oh1 + oh2
    lower = (lax.broadcasted_iota(jnp.int32, (rows, rows), 0)
             > lax.broadcasted_iota(jnp.int32, (rows, rows), 1)).astype(BF16)
    rank = _dot(lower, cnt.astype(BF16)) + cnt_sc[...]
    r1 = (oh1 * rank).sum(axis=1, keepdims=True)
    r2 = (oh2 * rank).sum(axis=1, keepdims=True)
    cnt_sc[...] += cnt.sum(axis=0, keepdims=True)
    cnt_ref[...] = cnt_sc[...]

    out = jnp.where(lane == 0, i1, 0.0)
    out = jnp.where(lane == 1, i2, out)
    out = jnp.where(lane == 2, w1, out)
    out = jnp.where(lane == 3, w2, out)
    out = jnp.where(lane == 4, r1, out)
    out = jnp.where(lane == 5, r2, out)
    route_ref[...] = out[:, :route_ref.shape[1]]


def _post_router(x_all, o_p, o_s, wo, mod_g, n2, wr_pad, tpb):
    t, d = x_all.shape
    ntp = t // TM - 1
    mrow = lambda i: _mod_block(i, ntp, tpb)
    gpt = TM // GROUP
    row = lambda i: (i, 0)
    const = lambda i: (0, 0)
    g1_g = sh_g = sc_g = mod_g
    return pl.pallas_call(
        _post_router_kernel,
        grid=(t // TM,),
        in_specs=[
            pl.BlockSpec((TM, d), row),
            pl.BlockSpec((NP, TM, LANES), lambda i: (0, jnp.minimum(i, ntp - 1), 0)),
            pl.BlockSpec((NP, TM, LANES), lambda i: (0, 0, 0)),
            pl.BlockSpec((d, d), const),
            pl.BlockSpec((gpt, d), lambda i: (mrow(i), MOD_GATE1)),
            pl.BlockSpec((1, d), const),
            pl.BlockSpec((gpt, d), lambda i: (mrow(i), MOD_SHIFT2)),
            pl.BlockSpec((gpt, d), lambda i: (mrow(i), MOD_SCALE2)),
            pl.BlockSpec((d, LANES), const),
        ],
        out_specs=[
            pl.BlockSpec((TM, d), row),
            pl.BlockSpec((TM, d), row),
            pl.BlockSpec((TM, 8), row),
            pl.BlockSpec((1, LANES), const),
        ],
        out_shape=[
            jax.ShapeDtypeStruct((t, d), F32),
            jax.ShapeDtypeStruct((t, d), F32),
            jax.ShapeDtypeStruct((t, 8), F32),
            jax.ShapeDtypeStruct((1, LANES), F32),
        ],
        scratch_shapes=[pltpu.VMEM((1, LANES), F32)],
        compiler_params=_cparams(("arbitrary",)),
        name="attn_out_router",
    )(x_all, o_p, o_s, wo, g1_g, n2, sh_g, sc_g, wr_pad)


def _dispatch_kernel(fill_start_ref, fill_on_ref, pos_ref, h2_ref, xs_ref, zero_buf, sem):
    rows = h2_ref.shape[0]

    @pl.when(pl.program_id(0) == 0)
    def _():
        zero_buf[...] = jnp.zeros_like(zero_buf)

        def fill(k):
            return pltpu.make_async_copy(
                zero_buf, xs_ref.at[pl.ds(pl.multiple_of(fill_start_ref[k], TR), TR)], sem)

        for k in range(fill_on_ref.shape[0]):
            @pl.when(fill_on_ref[k] > 0)
            def _():
                fill(k).start()
        for k in range(fill_on_ref.shape[0]):
            @pl.when(fill_on_ref[k] > 0)
            def _():
                fill(k).wait()

    def issue(r, c):
        src = h2_ref.at[pl.ds(r, 1)]
        pltpu.make_async_copy(src, xs_ref.at[pl.ds(pos_ref[0, 0, r], 1)], sem).start()
        pltpu.make_async_copy(src, xs_ref.at[pl.ds(pos_ref[0, 0, rows + r], 1)], sem).start()
        return c

    lax.fori_loop(0, rows, issue, 0, unroll=8)
    for _ in range(2):
        pltpu.make_async_copy(h2_ref, xs_ref.at[pl.ds(0, rows)], sem).wait()


def _dispatch(fill_start, fill_on, pos_tiles, h2, r_pad):
    t, d = h2.shape
    grid_spec = pltpu.PrefetchScalarGridSpec(
        num_scalar_prefetch=2,
        grid=(t // TM,),
        in_specs=[
            pl.BlockSpec((1, 1, 2 * TM), lambda i, fs, fo: (i, 0, 0), memory_space=pltpu.SMEM),
            pl.BlockSpec((TM, d), lambda i, fs, fo: (i, 0)),
        ],
        out_specs=pl.BlockSpec(memory_space=pl.ANY),
        scratch_shapes=[pltpu.VMEM((TR, d), F32), pltpu.SemaphoreType.DMA],
    )
    return pl.pallas_call(
        _dispatch_kernel,
        grid_spec=grid_spec,
        out_shape=jax.ShapeDtypeStruct((r_pad, d), F32),
        compiler_params=_cparams(("arbitrary",)),
        name="moe_dispatch",
    )(fill_start, fill_on, pos_tiles, h2)


def _expert_ffn_kernel(te_ref, tv_ref, xs_ref, wg_ref, wu_ref, wd_ref, y_ref, xb_sc, acc_sc):
    r = pl.program_id(0)
    j = pl.program_id(1)
    valid = tv_ref[r] > 0

    @pl.when(j == 0)
    def _():
        xb_sc[...] = xs_ref[...].astype(BF16)
        acc_sc[...] = jnp.zeros_like(acc_sc)

    @pl.when(valid)
    def _():
        x = xb_sc[...]
        act = (_silu(_dot(x, wg_ref[0])) * _dot(x, wu_ref[0])).astype(BF16)
        acc_sc[...] += _dot(act, wd_ref[0])

    @pl.when(j == pl.num_programs(1) - 1)
    def _():
        y_ref[...] = acc_sc[...]


def _expert_ffn(tile_expert, tile_valid, xs, wg, wu, wd):
    r_pad, d = xs.shape
    f = wg.shape[2]
    grid_spec = pltpu.PrefetchScalarGridSpec(
        num_scalar_prefetch=2,
        grid=(r_pad // TR, f // FC),
        in_specs=[
            pl.BlockSpec((TR, d), lambda r, j, te, tv: (r, 0)),
            pl.BlockSpec((1, d, FC), lambda r, j, te, tv: (te[r], 0, j)),
            pl.BlockSpec((1, d, FC), lambda r, j, te, tv: (te[r], 0, j)),
            pl.BlockSpec((1, FC, d), lambda r, j, te, tv: (te[r], j, 0)),
        ],
        out_specs=pl.BlockSpec((TR, d), lambda r, j, te, tv: (r, 0)),
        scratch_shapes=[pltpu.VMEM((TR, d), BF16), pltpu.VMEM((TR, d), F32)],
    )
    return pl.pallas_call(
        _expert_ffn_kernel,
        grid_spec=grid_spec,
        out_shape=jax.ShapeDtypeStruct((r_pad, d), F32),
        compiler_params=_cparams(("arbitrary", "arbitrary")),
        name="moe_expert_ffn",
    )(tile_expert, tile_valid, xs, wg, wu, wd)


def _combine_kernel(pos_ref, posn_ref, x1_ref, g2_ref, route_ref, y_ref, outp_ref, outs_ref, buf, sem, *,
                    ntp):
    i = pl.program_id(0)
    rows = x1_ref.shape[0]
    slot = i % 2

    def gather(p_ref, s):
        def issue(r, c):
            pltpu.make_async_copy(y_ref.at[pl.ds(p_ref[0, 0, r], 1)], buf.at[s, 0, pl.ds(r, 1)],
                                  sem.at[s]).start()
            pltpu.make_async_copy(y_ref.at[pl.ds(p_ref[0, 0, rows + r], 1)], buf.at[s, 1, pl.ds(r, 1)],
                                  sem.at[s]).start()
            return c
        lax.fori_loop(0, rows, issue, 0, unroll=8)

    @pl.when(i == 0)
    def _():
        gather(pos_ref, 0)

    @pl.when(i + 1 < pl.num_programs(0))
    def _():
        gather(posn_ref, 1 - slot)

    for k in range(2):
        pltpu.make_async_copy(y_ref.at[pl.ds(0, rows)], buf.at[slot, k], sem.at[slot]).wait()
    route = route_ref[...]
    moe = route[:, 2:3] * buf[slot, 0] + route[:, 3:4] * buf[slot, 1]
    res = x1_ref[...] + _group_rows(g2_ref[...], rows) * moe

    @pl.when(i < ntp)
    def _():
        outp_ref[...] = res

    @pl.when(i >= ntp)
    def _():
        outs_ref[...] = res


def _combine(pos_tiles, x1, mod_g, route, y, tp, tpb):
    t, d = x1.shape
    ntp = tp // TMC
    gpt = TMC // GROUP
    assert TMC == TM
    row = lambda i: (i, 0)
    last = t // TMC - 1
    g2_g = mod_g
    return pl.pallas_call(
        functools.partial(_combine_kernel, ntp=ntp),
        grid=(t // TMC,),
        in_specs=[
            pl.BlockSpec((1, 1, 2 * TMC), lambda i: (i, 0, 0), memory_space=pltpu.SMEM),
            pl.BlockSpec((1, 1, 2 * TMC), lambda i: (jnp.minimum(i + 1, last), 0, 0), memory_space=pltpu.SMEM),
            pl.BlockSpec((TMC, d), row),
            pl.BlockSpec((gpt, d), lambda i: (_mod_block(i, ntp, tpb), MOD_GATE2)),
            pl.BlockSpec((TMC, 8), row),
            pl.BlockSpec(memory_space=pl.ANY),
        ],
        out_specs=[
            pl.BlockSpec((TMC, d), lambda i: (jnp.minimum(i, ntp - 1), 0)),
            pl.BlockSpec((TMC, d), lambda i: (jnp.maximum(i - ntp, 0), 0)),
        ],
        out_shape=[jax.ShapeDtypeStruct((tp, d), F32), jax.ShapeDtypeStruct((t - tp, d), F32)],
        scratch_shapes=[pltpu.VMEM((2, 2, TMC, d), F32), pltpu.SemaphoreType.DMA((2,))],
        compiler_params=_cparams(("arbitrary",)),
        name="moe_combine",
    )(pos_tiles, pos_tiles, x1, g2_g, route, y)


def _band_valid(q_pos, k_pos):
    qc = q_pos // CHUNK
    kc = k_pos // CHUNK
    return (k_pos[None, :] >= 0) & (kc[None, :] <= qc[:, None]) & (kc[None, :] >= qc[:, None] - N_PAST_CHUNKS)


def _band_bias(rel_bias, q_pos, k_pos):
    nq, nk = len(q_pos), len(k_pos)
    assert np.all(np.diff(q_pos) == 1) and np.all(np.diff(k_pos) == 1)
    period = nq + nk
    dist = (q_pos[0] - k_pos[0]) + (nq - 1) - np.arange(period)
    vec = rel_bias.astype(F32)[:, np.clip(dist, -REL_CLIP, REL_CLIP) + REL_CLIP]
    flat = jnp.tile(jnp.roll(vec, -(nq - 1), axis=1), (1, nq))[:, :nq * (period - 1)]
    tab = flat.reshape(-1, nq, period - 1)[:, :, :nk]
    return jnp.where(jnp.asarray(_band_valid(q_pos, k_pos))[None], tab, NEG_INF)


def _tile_pos(pos2, tile):
    t = pos2.shape[1]
    return pos2.reshape(2, t // tile, tile).transpose(1, 0, 2).reshape(t // tile, 1, 2 * tile)


def _feature_major(cache):
    n, length = cache.shape[0], cache.shape[1]
    return cache.transpose(0, 2, 3, 1).reshape(n, N_HEADS * HEAD_DIM, length)


def _token_major(xt):
    n, _, length = xt.shape
    return xt.reshape(n, N_HEADS, HEAD_DIM, length).transpose(0, 3, 1, 2)


def kernel(x_prompt, x_sample, cache_a_k, cache_a_v, cache_b_k, cache_b_v, c_prompt, c_sample, w_qkv, w_o, norm1_g, norm2_g, w_ada, b_ada, q_norm_g, k_norm_g, rel_bias, w_gate_d, w_up_d, w_down_d, w_router, w_gate_e, w_up_e, w_down_e):
    bp, sp, d = x_prompt.shape
    bs, ts, _ = x_sample.shape
    past_len = cache_b_k.shape[2]
    win = cache_a_k.shape[2]
    depth = w_qkv.shape[0]
    assert depth == 2 and d == N_HEADS * HEAD_DIM and ts == GROUP
    tp, tsm = bp * sp, bs * ts
    t_all = tp + tsm
    assert sp % TM == 0 and tsm == TM and sp % (BAND_UNROLL * TQ) == 0 and t_all % TMC == 0 and tp % TMC == 0
    assert win == BAND_PAST == TM and sp >= BAND_KEYS and past_len % TK == 0 and past_len >= win

    x_p = x_prompt.reshape(tp, d)
    x_s = x_sample.reshape(tsm, d)

    mod = _ada(jnp.concatenate([c_prompt, c_sample], axis=0), w_ada, b_ada)
    mod = jnp.concatenate([mod[:, bp:], jnp.repeat(mod[:, :bp], TM // GROUP, axis=1)], axis=1)
    tpb = sp // TM

    bd = (np.arange(256)[:, None] // HEAD_DIM == np.arange(256)[None, :] // HEAD_DIM)
    bd = jnp.asarray(bd.astype(np.float32) / HEAD_DIM, BF16)
    ones_row = jnp.ones((1, d), F32)
    ones_col = jnp.ones((d, 1), F32)

    for i in range(depth):
        mod_g = mod[i]
        band = i % 2 == 0
        wq = w_qkv[i][:, :d].astype(BF16)
        wkvt = w_qkv[i][:, d:].T.astype(BF16)
        qg = jnp.tile(q_norm_g[0], N_HEADS)[None] if band else ones_row
        kg = jnp.tile(k_norm_g[0], N_HEADS)[:, None] if band else ones_col
        n1 = norm1_g[i][None]
        if band:
            q_p, ktb, vtb, ktf, vtf = _qkv_prompt(x_p, n1, mod_g, wq, wkvt, qg, kg, bd, bp, sp, True)
            q_s, kv_s, ksf, vsf = _qkv_sample(x_s, 0, n1, mod_g, wq, wkvt, qg, kg, bd, True)
            q_pos = np.arange(TQ) + BAND_PAST
            bias_p = _band_bias(rel_bias[0], q_pos, np.arange(BAND_KEYS))
            bias_p = jnp.pad(bias_p, ((0, 0), (0, 0), (0, BAND_PAST)), constant_values=NEG_INF)
            o_p = _band_prompt(q_p, ktb, vtb, bias_p, bp, sp)
            pos_s = past_len + np.arange(ts)
            pos_win = past_len - win + np.arange(win)
            ca_k, ca_v = _feature_major(cache_a_k[0]), _feature_major(cache_a_v[0])
            right = ((0, 0), (0, 0), (LANES - ts, 0))
            new_k = jnp.pad(ksf.reshape(d, bs, ts).transpose(1, 0, 2), right)
            new_v = jnp.pad(vsf.reshape(d, bs, ts).transpose(1, 0, 2), right)
            o_s, k_win, v_win = _band_sample(q_s, kv_s, ca_k, ca_v, _band_bias(rel_bias[0], pos_s, pos_win),
                                             _band_bias(rel_bias[0], pos_s, pos_s), new_k, new_v, bs, ts)
            a_k_p = _token_major(ktf)[None]
            a_v_p = _token_major(vtf)[None]
            a_k_s = _token_major(k_win)[None]
            a_v_s = _token_major(v_win)[None]
            f = w_gate_d.shape[2]
            experts = (w_gate_e[0].reshape(N_EXPERTS * d, f), w_up_e[0].reshape(N_EXPERTS * d, f),
                       w_down_e[0].reshape(N_EXPERTS * f, d))
            side = (tp // TM + 1) * (f // FC) >= CAST_BLOCKS and all(
                e.shape[0] % (16 * CAST_BLOCKS) == 0 for e in experts)
            x_all, *experts_bf = _post_dense(
                x_p, x_s, o_p, o_s, w_o[i].astype(BF16), mod_g, norm2_g[i][None], w_gate_d[0].astype(BF16),
                w_up_d[0].astype(BF16), w_down_d[0].astype(BF16), tpb, experts if side else ())
            if not side:
                experts_bf = [e.astype(BF16) for e in experts]
            wge, wue = (e.reshape(N_EXPERTS, d, f) for e in experts_bf[:2])
            wde = experts_bf[2].reshape(N_EXPERTS, f, d)
        else:
            q_p, ktb, vtb, ktf, vtf = _qkv_prompt(x_all, n1, mod_g, wq, wkvt, qg, kg, bd, bp, sp, False)
            q_s, kv_s, ksf, vsf = _qkv_sample(x_all, tp // TM, n1, mod_g, wq, wkvt, qg, kg, bd, False)
            o_p = _sb_prompt(q_p, ktb, vtb, bp, sp)
            o_s = _sb_sample(q_s, kv_s, _feature_major(cache_b_k[0]), _feature_major(cache_b_v[0]), bs, ts)
            b_k_p = _token_major(ktf)[None]
            b_v_p = _token_major(vtf)[None]
            b_k_s = ksf.reshape(1, bs, ts, N_HEADS, HEAD_DIM)
            b_v_s = vsf.reshape(1, bs, ts, N_HEADS, HEAD_DIM)
            wr_pad = jnp.pad(w_router[0], ((0, 0), (0, LANES - N_EXPERTS)))
            x1, h2, route, cnt = _post_router(x_all, o_p, o_s, w_o[i].astype(BF16), mod_g, norm2_g[i][None],
                                              wr_pad, tpb)
            counts = cnt[0, :N_EXPERTS].astype(jnp.int32)
            sizes = (counts + TR - 1) // TR * TR
            ends = jnp.cumsum(sizes)
            offs = ends - sizes
            route_t = route.T
            e12 = route_t[0:2].astype(jnp.int32)
            off12 = sum(jnp.where(e12 == e, offs[e], 0) for e in range(N_EXPERTS))
            pos2 = off12 + route_t[4:6].astype(jnp.int32)
            r_pad = (2 * t_all // TR + N_EXPERTS) * TR
            tile_start = jnp.arange(r_pad // TR, dtype=jnp.int32) * TR
            tile_expert = jnp.minimum(jnp.sum(tile_start[:, None] >= ends[None, :], axis=1),
                                      N_EXPERTS - 1).astype(jnp.int32)
            tile_valid = (tile_start < ends[-1]).astype(jnp.int32)
            tail_start = ends[-1] + jnp.arange(N_EXPERTS, dtype=jnp.int32) * TR
            fill_start = jnp.concatenate([ends - TR, tail_start]).astype(jnp.int32)
            fill_on = jnp.concatenate([sizes > 0, tail_start < r_pad]).astype(jnp.int32)
            xs = _dispatch(jnp.where(fill_on > 0, fill_start, 0), fill_on, _tile_pos(pos2, TM), h2, r_pad)
            y = _expert_ffn(tile_expert, tile_valid, xs, wge, wue, wde)
            y_p, y_s = _combine(_tile_pos(pos2, TMC), x1, mod_g, route, y, tp, tpb)

    return (y_p.reshape(bp, sp, d), y_s.reshape(bs, ts, d), a_k_p, a_v_p, a_k_s, a_v_s,
            b_k_p, b_v_p, b_k_s, b_v_s)
```

```python
import functools

import numpy as np
import jax
import jax.numpy as jnp
from jax import lax
from jax.experimental import pallas as pl
from jax.experimental.pallas import tpu as pltpu

F32 = jnp.float32
BF16 = jnp.bfloat16

CHUNK = 64
N_HEADS = 16
HEAD_DIM = 64
N_PAST_CHUNKS = 8
BAND_PAST = N_PAST_CHUNKS * CHUNK
REL_CLIP = 128
N_EXPERTS = 8
EPS = 1e-6
NEG_INF = -1e30

LANES = 128
VMEM_LIMIT = 56 * 1024 * 1024

MOD_SHIFT1, MOD_SCALE1, MOD_GATE1, MOD_SHIFT2, MOD_SCALE2, MOD_GATE2 = range(6)

GROUP = 32
TM = 512
FC = 1792
TQ = 128
TK = 128
NP = N_HEADS // 2
BAND_KEYS = BAND_PAST + TQ
BAND_UNROLL = 8
BAND_ROWS = 32
SB_PG = 2
SB_TILES = 4
SB_STOP = -90.0
CAST_BLOCKS = 128
TR = 512
TMC = 512


def _cparams(sem):
    return pltpu.CompilerParams(dimension_semantics=sem, vmem_limit_bytes=VMEM_LIMIT)


def _dot(a, b):
    return jnp.dot(a, b, preferred_element_type=F32)


def _dot_nt(a, b):
    return lax.dot_general(a, b, (((1,), (1,)), ((), ())), preferred_element_type=F32)


def _split_hi_lo(x):
    hi = x.astype(BF16)
    lo = (x - hi.astype(F32)).astype(BF16)
    return hi, lo


def _silu(x):
    return x * (1.0 / (1.0 + jnp.exp(-x)))


def _lane_lt64(shape):
    return lax.broadcasted_iota(jnp.int32, shape, len(shape) - 1) < HEAD_DIM


def _group_rows(vec_rows, rows):
    g, d = vec_rows.shape
    return jnp.broadcast_to(vec_rows[:, None, :], (g, rows // g, d)).reshape(rows, d)


def _mod_block(i, n_prompt_tiles, tiles_per_batch):
    return jnp.where(i >= n_prompt_tiles, 0, 1 + i // tiles_per_batch)


def _mod_norm(x, g, shift_g, scale_g):
    rows = x.shape[0]
    ms = jnp.mean(x * x, axis=-1, keepdims=True)
    y = x * lax.rsqrt(ms + EPS) * g
    return y * (1.0 + _group_rows(scale_g, rows)) + _group_rows(shift_g, rows)


def _stack_heads(q2):
    m = _lane_lt64(q2.shape)
    z = jnp.zeros_like(q2)
    return jnp.concatenate([jnp.where(m, q2, z), jnp.where(m, z, q2)], axis=0)


def _pv_feature_major(pb, vt):
    r = pb.shape[0] // 2
    both = _dot_nt(pb, vt)
    return jnp.where(_lane_lt64((r, LANES)), both[:r], both[r:])


def _pv_row_major(pb, v2):
    r = pb.shape[0] // 2
    both = _dot(pb, v2)
    return jnp.where(_lane_lt64((r, LANES)), both[:r], both[r:])


def _ada_kernel(c_ref, w_ref, b_ref, o_ref):
    s = _silu(c_ref[...]).astype(BF16)
    o_ref[0] = _dot(s, w_ref[0].astype(BF16)) + b_ref[0]


def _ada(c_all, w_ada, b_ada):
    depth, d, n = w_ada.shape
    nb = c_all.shape[0]
    tn = 1536
    return pl.pallas_call(
        _ada_kernel,
        grid=(depth, n // tn),
        in_specs=[
            pl.BlockSpec((nb, d), lambda l, j: (0, 0)),
            pl.BlockSpec((1, d, tn), lambda l, j: (l, 0, j)),
            pl.BlockSpec((1, 1, tn), lambda l, j: (l, 0, j)),
        ],
        out_specs=pl.BlockSpec((1, nb, tn), lambda l, j: (l, 0, j)),
        out_shape=jax.ShapeDtypeStruct((depth, nb, n), F32),
        compiler_params=_cparams(("arbitrary", "arbitrary")),
        name="ada_modulation",
    )(c_all, w_ada, b_ada.reshape(depth, 1, n))


def _head_mean_sq_rows(x, bd_ref):
    hi, lo = _split_hi_lo(x * x)
    w = bd_ref.shape[0]
    cols = []
    for c in range(x.shape[1] // w):
        sl = slice(c * w, (c + 1) * w)
        cols.append(_dot(hi[:, sl], bd_ref[...]) + _dot(lo[:, sl], bd_ref[...]))
    return jnp.concatenate(cols, axis=1)


def _head_norm_cols(xt, gain_col):
    d, c = xt.shape
    x3 = xt.reshape(N_HEADS, HEAD_DIM, c)
    ms = jnp.mean(x3 * x3, axis=1, keepdims=True)
    return (x3 * lax.rsqrt(ms + EPS)).reshape(d, c) * gain_col


def _project(x, g_ref, sh_ref, sc_ref, wq_ref, wkvt_ref, qg_ref, kg_ref, bd_ref, qk_norm):
    d = x.shape[1]
    h = _mod_norm(x, g_ref[...], sh_ref[...], sc_ref[...]).astype(BF16)
    q = _dot(h, wq_ref[...])
    kvt = _dot_nt(wkvt_ref[...], h)
    kt, vt = kvt[:d], kvt[d:]
    if qk_norm:
        q = q * lax.rsqrt(_head_mean_sq_rows(q, bd_ref) + EPS) * qg_ref[...]
        kt = _head_norm_cols(kt, kg_ref[...])
    return (q * (HEAD_DIM ** -0.5)).astype(BF16), kt, vt


def _qkv_prompt_kernel(x_ref, g_ref, sh_ref, sc_ref, wq_ref, wkvt_ref, qg_ref, kg_ref, bd_ref,
                       q_ref, ktb_ref, vtb_ref, ktf_ref, vtf_ref, *, qk_norm, tail_only):
    qs, kt, vt = _project(x_ref[...], g_ref, sh_ref, sc_ref, wq_ref, wkvt_ref, qg_ref, kg_ref, bd_ref,
                          qk_norm)
    for p in range(NP):
        q_ref[p] = qs[:, p * LANES:(p + 1) * LANES]
    ktb_ref[0] = kt.astype(BF16)
    vtb_ref[0] = vt.astype(BF16)
    if tail_only:
        @pl.when((pl.program_id(0) + 1) % tail_only == 0)
        def _():
            ktf_ref[0] = kt
            vtf_ref[0] = vt
    else:
        ktf_ref[0] = kt
        vtf_ref[0] = vt


def _qkv_prompt(x, g, mod_g, wq, wkvt, qg_row, kg_col, bd, n_batch, s_len, qk_norm):
    d = x.shape[1]
    tpb = s_len // TM
    gpt = TM // GROUP
    row = lambda i: (i, 0)
    const = lambda i: (0, 0)
    shift_g = scale_g = mod_g
    fmaj = lambda i: (i // tpb, 0, i % tpb)
    if qk_norm:
        f_spec = pl.BlockSpec((1, d, TM), lambda i: (i // tpb, 0, 0))
        f_shape = jax.ShapeDtypeStruct((n_batch, d, TM), F32)
    else:
        f_spec = pl.BlockSpec((1, d, TM), fmaj)
        f_shape = jax.ShapeDtypeStruct((n_batch, d, s_len), F32)
    return pl.pallas_call(
        functools.partial(_qkv_prompt_kernel, qk_norm=qk_norm, tail_only=tpb if qk_norm else 0),
        grid=(n_batch * tpb,),
        in_specs=[
            pl.BlockSpec((TM, d), row),
            pl.BlockSpec((1, d), const),
            pl.BlockSpec((gpt, d), lambda i: (_mod_block(i, n_batch * tpb, tpb), MOD_SHIFT1)),
            pl.BlockSpec((gpt, d), lambda i: (_mod_block(i, n_batch * tpb, tpb), MOD_SCALE1)),
            pl.BlockSpec((d, d), const),
            pl.BlockSpec((2 * d, d), const),
            pl.BlockSpec((1, d), const),
            pl.BlockSpec((d, 1), const),
            pl.BlockSpec(bd.shape, const),
        ],
        out_specs=[
            pl.BlockSpec((NP, TM, LANES), lambda i: (0, i, 0)),
            pl.BlockSpec((1, d, TM), fmaj),
            pl.BlockSpec((1, d, TM), fmaj),
            f_spec, f_spec,
        ],
        out_shape=[
            jax.ShapeDtypeStruct((NP, n_batch * s_len, LANES), BF16),
            jax.ShapeDtypeStruct((n_batch, d, s_len), BF16),
            jax.ShapeDtypeStruct((n_batch, d, s_len), BF16),
            f_shape, f_shape,
        ],
        compiler_params=_cparams(("arbitrary",)),
        name="qkv_prompt_qknorm" if qk_norm else "qkv_prompt",
    )(x, g, shift_g, scale_g, wq, wkvt, qg_row, kg_col, bd)


def _qkv_sample_kernel(x_ref, g_ref, sh_ref, sc_ref, wq_ref, wkvt_ref, qg_ref, kg_ref, bd_ref,
                       q_ref, kv_ref, kf_ref, vf_ref, *, qk_norm):
    qs, kt, vt = _project(x_ref[...], g_ref, sh_ref, sc_ref, wq_ref, wkvt_ref, qg_ref, kg_ref, bd_ref,
                          qk_norm)
    k, v = kt.T, vt.T
    if qk_norm:
        kf_ref[...] = kt
        vf_ref[...] = vt
    else:
        kf_ref[...] = k
        vf_ref[...] = v
    kb, vb = k.astype(BF16), v.astype(BF16)
    for p in range(NP):
        sl = slice(p * LANES, (p + 1) * LANES)
        q_ref[p] = qs[:, sl]
        kv_ref[p] = kb[:, sl]
        kv_ref[NP + p] = vb[:, sl]


def _qkv_sample(x, row_block, g, mod_g, wq, wkvt, qg_row, kg_col, bd, qk_norm):
    d = x.shape[1]
    gpt = TM // GROUP
    row = lambda i: (row_block, 0)
    const = lambda i: (0, 0)
    shift_g = scale_g = mod_g
    f_shape = jax.ShapeDtypeStruct((d, TM) if qk_norm else (TM, d), F32)
    return pl.pallas_call(
        functools.partial(_qkv_sample_kernel, qk_norm=qk_norm),
        grid=(1,),
        in_specs=[
            pl.BlockSpec((TM, d), row),
            pl.BlockSpec((1, d), const),
            pl.BlockSpec((gpt, d), lambda i: (0, MOD_SHIFT1)),
            pl.BlockSpec((gpt, d), lambda i: (0, MOD_SCALE1)),
            pl.BlockSpec((d, d), const),
            pl.BlockSpec((2 * d, d), const),
            pl.BlockSpec((1, d), const),
            pl.BlockSpec((d, 1), const),
            pl.BlockSpec(bd.shape, const),
        ],
        out_specs=[
            pl.BlockSpec((NP, TM, LANES), lambda i: (0, 0, 0)),
            pl.BlockSpec((2 * NP, TM, LANES), lambda i: (0, 0, 0)),
            pl.BlockSpec(f_shape.shape, const),
            pl.BlockSpec(f_shape.shape, const),
        ],
        out_shape=[
            jax.ShapeDtypeStruct((NP, TM, LANES), BF16),
            jax.ShapeDtypeStruct((2 * NP, TM, LANES), BF16),
            f_shape, f_shape,
        ],
        compiler_params=_cparams(("arbitrary",)),
        name="qkv_sample_qknorm" if qk_norm else "qkv_sample",
    )(x, g, shift_g, scale_g, wq, wkvt, qg_row, kg_col, bd)


def _band_prompt_kernel(q_ref, kt_ref, vt_ref, bias_ref, o_ref):
    s_len = q_ref.shape[1]
    back = BAND_PAST // TK

    def tiles(n, carry):
        idx = [n * BAND_UNROLL + u for u in range(BAND_UNROLL)]
        row0 = [pl.multiple_of(i * TQ, TQ) for i in idx]
        key0 = [pl.multiple_of(jnp.maximum(i - back, 0) * TK, TK) for i in idx]
        col0 = [pl.multiple_of(jnp.maximum(back - i, 0) * TK, TK) for i in idx]
        scores = [_dot(_stack_heads(q_ref[0, pl.ds(row0[u], TQ), :]), kt_ref[0, :, pl.ds(key0[u], BAND_KEYS)])
                  for u in range(BAND_UNROLL)]
        for u in range(BAND_UNROLL):
            ps, ls = [], []
            for h in range(2):
                for r0 in range(0, TQ, BAND_ROWS):
                    s = (scores[u][h * TQ + r0:h * TQ + r0 + BAND_ROWS]
                         + bias_ref[h, r0:r0 + BAND_ROWS, pl.ds(col0[u], BAND_KEYS)])
                    p = jnp.exp(s - s.max(axis=1, keepdims=True))
                    ls.append(p.sum(axis=1, keepdims=True))
                    ps.append(p.astype(BF16))
            l = jnp.concatenate(ls, axis=0)
            acc = _pv_feature_major(jnp.concatenate(ps, axis=0), vt_ref[0, :, pl.ds(key0[u], BAND_KEYS)])
            linv = jnp.where(_lane_lt64((TQ, LANES)), 1.0 / l[:TQ], 1.0 / l[TQ:])
            o_ref[0, pl.ds(row0[u], TQ), :] = (acc * linv).astype(BF16)
        return carry

    lax.fori_loop(0, s_len // (BAND_UNROLL * TQ), tiles, 0)


def _band_prompt(q_pm, ktb, vtb, bias_p, n_batch, s_len):
    return pl.pallas_call(
        _band_prompt_kernel,
        grid=(n_batch, NP),
        in_specs=[
            pl.BlockSpec((1, s_len, LANES), lambda b, p: (p, b, 0)),
            pl.BlockSpec((1, LANES, s_len), lambda b, p: (b, p, 0)),
            pl.BlockSpec((1, LANES, s_len), lambda b, p: (b, p, 0)),
            pl.BlockSpec((2, TQ, bias_p.shape[2]), lambda b, p: (p, 0, 0)),
        ],
        out_specs=pl.BlockSpec((1, s_len, LANES), lambda b, p: (p, b, 0)),
        out_shape=jax.ShapeDtypeStruct((NP, n_batch * s_len, LANES), BF16),
        compiler_params=_cparams(("arbitrary", "arbitrary")),
        name="band_attention_prompt",
    )(q_pm, ktb, vtb, bias_p)


def _band_sample_kernel(q_ref, kv_ref, kc_ref, vc_ref, bc_ref, bn_ref, knew_ref, vnew_ref,
                        o_ref, kwin_ref, vwin_ref):
    ts = q_ref.shape[1]
    win = kc_ref.shape[2]
    new_lanes = lax.broadcasted_iota(jnp.int32, knew_ref.shape[1:], 1) >= LANES - ts
    for c_ref, n_ref, w_ref in ((kc_ref, knew_ref, kwin_ref), (vc_ref, vnew_ref, vwin_ref)):
        rolled = pltpu.roll(c_ref[0], win - ts, 1)
        w_ref[0, :, :win - LANES] = rolled[:, :win - LANES]
        w_ref[0, :, win - LANES:] = jnp.where(new_lanes, n_ref[0], rolled[:, win - LANES:])
    for p in range(NP):
        rows = slice(p * LANES, (p + 1) * LANES)
        qm = _stack_heads(q_ref[p])
        kn, vn = kv_ref[p], kv_ref[NP + p]
        s1 = _dot(qm, kc_ref[0, rows, :].astype(BF16)) + bc_ref[2 * p:2 * p + 2].reshape(2 * ts, -1)
        s2 = _dot_nt(qm, kn) + bn_ref[2 * p:2 * p + 2].reshape(2 * ts, ts)
        m = jnp.maximum(s1.max(axis=1, keepdims=True), s2.max(axis=1, keepdims=True))
        p1 = jnp.exp(s1 - m)
        p2 = jnp.exp(s2 - m)
        l = p1.sum(axis=1, keepdims=True) + p2.sum(axis=1, keepdims=True)
        acc = (_pv_feature_major(p1.astype(BF16), vc_ref[0, rows, :].astype(BF16))
               + _pv_row_major(p2.astype(BF16), vn))
        linv = jnp.where(_lane_lt64((ts, LANES)), 1.0 / l[:ts], 1.0 / l[ts:])
        o_ref[p] = (acc * linv).astype(BF16)


def _band_sample(q_s, kv_s, cache_kt, cache_vt, bias_c, bias_n, new_kt, new_vt, n_streams, ts):
    d, win = cache_kt.shape[1], cache_kt.shape[2]
    stream = lambda s: (s, 0, 0)
    return pl.pallas_call(
        _band_sample_kernel,
        grid=(n_streams,),
        in_specs=[
            pl.BlockSpec((NP, ts, LANES), lambda s: (0, s, 0)),
            pl.BlockSpec((2 * NP, ts, LANES), lambda s: (0, s, 0)),
            pl.BlockSpec((1, d, win), stream),
            pl.BlockSpec((1, d, win), stream),
            pl.BlockSpec(bias_c.shape, lambda s: (0, 0, 0)),
            pl.BlockSpec(bias_n.shape, lambda s: (0, 0, 0)),
            pl.BlockSpec((1, d, LANES), stream),
            pl.BlockSpec((1, d, LANES), stream),
        ],
        out_specs=[
            pl.BlockSpec((NP, ts, LANES), lambda s: (0, s, 0)),
            pl.BlockSpec((1, d, win), stream),
            pl.BlockSpec((1, d, win), stream),
        ],
        out_shape=[
            jax.ShapeDtypeStruct(q_s.shape, BF16),
            jax.ShapeDtypeStruct(cache_kt.shape, F32),
            jax.ShapeDtypeStruct(cache_vt.shape, F32),
        ],
        compiler_params=_cparams(("arbitrary",)),
        name="band_attention_sample",
    )(q_s, kv_s, cache_kt, cache_vt, bias_c, bias_n, new_kt, new_vt)


def _sb_weights(z, upper, carry, mask):
    sp = jnp.maximum(z, 0.0) + jnp.log(1.0 + jnp.exp(-jnp.abs(z)))
    lk = -sp
    if mask is not None:
        lk = jnp.where(mask, lk, 0.0)
    hi, lo = _split_hi_lo(lk)
    after = _dot(hi, upper) + _dot(lo, upper) + carry
    a = jnp.exp(z - sp + after)
    if mask is not None:
        a = jnp.where(mask, a, 0.0)
    return a.astype(BF16), carry + lk.sum(axis=1, keepdims=True)


def _sb_weights_multi(zs, upper_ones, carries, masks):
    c = zs[0].shape[1]
    halves = []
    for z, mask in zip(zs, masks):
        nz = -z
        lk = jnp.minimum(nz, 0.0) - jnp.log(1.0 + jnp.exp(jnp.minimum(z, nz)))
        if mask is not None:
            lk = jnp.where(mask, lk, 0.0)
        halves.append(jnp.concatenate(_split_hi_lo(lk), axis=1))
    sums = [_dot(hl, upper_ones) for hl in halves]
    weights, new_carries = [], []
    for z, sm, carry, mask in zip(zs, sums, carries, masks):
        a = jnp.exp(z + (sm[:, :c] + carry))
        if mask is not None:
            a = jnp.where(mask, a, 0.0)
        weights.append(a.astype(BF16))
        new_carries.append(carry + sm[:, c:])
    return weights, new_carries


def _upper_tri(n):
    return (lax.broadcasted_iota(jnp.int32, (n, n), 0)
            > lax.broadcasted_iota(jnp.int32, (n, n), 1)).astype(BF16)


def _upper_tri_ones(n):
    incl = (lax.broadcasted_iota(jnp.int32, (n, n), 0)
            >= lax.broadcasted_iota(jnp.int32, (n, n), 1)).astype(BF16)
    half = jnp.concatenate([incl, jnp.ones((n, n), BF16)], axis=1)
    return jnp.concatenate([half, half], axis=0)


def _causal_mask(rows, cols):
    r = lax.broadcasted_iota(jnp.int32, (2 * rows, cols), 0)
    r = jnp.where(r >= rows, r - rows, r)
    c = lax.broadcasted_iota(jnp.int32, (2 * rows, cols), 1)
    return c < r


def _any_live(carries):
    m = carries[0].max()
    for c in carries[1:]:
        m = jnp.maximum(m, c.max())
    return (m >= SB_STOP).astype(jnp.int32)


def _sb_prompt_kernel(q_ref, kt_ref, vt_ref, o_ref):
    npg, s_len = q_ref.shape[0], q_ref.shape[1]
    upper_ones = _upper_tri_ones(TK)
    diag_mask = _causal_mask(TQ, TK)
    all_true = diag_mask | True

    units = [(g, t) for t in range(SB_TILES) for g in range(npg)]

    def blocks(qms, blks, carries, accs, masks):
        cols = [pl.ds(pl.multiple_of(b * TK, TK), TK) for b in blks]
        rows = [slice(g * LANES, (g + 1) * LANES) for g, _ in units]
        zs = [_dot(qms[u], kt_ref[0, rows[u], cols[u]]) for u in range(len(units))]
        ws, carries = _sb_weights_multi(zs, upper_ones, carries, masks)
        accs = [accs[u] + _pv_feature_major(ws[u], vt_ref[0, rows[u], cols[u]]) for u in range(len(units))]
        return carries, accs

    def tile_group(i2, c0):
        first = SB_TILES * i2
        qms = [_stack_heads(q_ref[g, pl.ds(pl.multiple_of((first + t) * TQ, TQ), TQ), :]) for g, t in units]
        carries, accs = blocks(qms, [first + t for _, t in units],
                               [jnp.zeros((2 * TQ, TK), F32)] * len(units),
                               [jnp.zeros((TQ, LANES), F32)] * len(units), [diag_mask] * len(units))

        def cond(st):
            return jnp.logical_and(first + SB_TILES - 1 - st[0] >= 0, st[1] > 0)

        def body(st):
            k = st[0]
            blks = [first + t - k for _, t in units]
            masks = [None if t == SB_TILES - 1 else jnp.logical_and(all_true, blks[u] >= 0)
                     for u, (_, t) in enumerate(units)]
            cs, as_ = blocks(qms, [jnp.maximum(b, 0) for b in blks], list(st[2]), list(st[3]), masks)
            return (k + 1, _any_live(cs), tuple(cs), tuple(as_))

        st = lax.while_loop(cond, body, (1, _any_live(carries), tuple(carries), tuple(accs)))
        for u, (g, t) in enumerate(units):
            row0 = pl.multiple_of((first + t) * TQ, TQ)
            o_ref[g, pl.ds(row0, TQ), :] = st[3][u].astype(BF16)
        return c0

    lax.fori_loop(0, s_len // (SB_TILES * TQ), tile_group, 0)


def _sb_prompt(q_pm, ktb, vtb, n_batch, s_len):
    ng = NP // SB_PG
    return pl.pallas_call(
        _sb_prompt_kernel,
        grid=(n_batch, ng),
        in_specs=[
            pl.BlockSpec((SB_PG, s_len, LANES), lambda b, g: (g, b, 0)),
            pl.BlockSpec((1, SB_PG * LANES, s_len), lambda b, g: (b, g, 0)),
            pl.BlockSpec((1, SB_PG * LANES, s_len), lambda b, g: (b, g, 0)),
        ],
        out_specs=pl.BlockSpec((SB_PG, s_len, LANES), lambda b, g: (g, b, 0)),
        out_shape=jax.ShapeDtypeStruct((NP, n_batch * s_len, LANES), BF16),
        compiler_params=_cparams(("arbitrary", "arbitrary")),
        name="stick_breaking_prompt",
    )(q_pm, ktb, vtb)


def _sb_sample_kernel(q_ref, kv_ref, kc_hbm, vc_hbm, o_ref, kbuf, vbuf, sem):
    s = pl.program_id(0)
    ts = q_ref.shape[1]
    nblk = kc_hbm.shape[2] // TK
    upper_ones = _upper_tri_ones(TK)

    def copies(j):
        slot = j % 2
        cols = pl.ds(pl.multiple_of(j * TK, TK), TK)
        return (pltpu.make_async_copy(kc_hbm.at[s, :, cols], kbuf.at[slot], sem.at[0, slot]),
                pltpu.make_async_copy(vc_hbm.at[s, :, cols], vbuf.at[slot], sem.at[1, slot]))

    def start(j):
        for c in copies(j):
            c.start()

    def wait(j):
        for c in copies(j):
            c.wait()

    start(nblk - 1)

    qms, carries, accs = [], [], []
    mask = _causal_mask(ts, ts)
    upper_n = _upper_tri(ts)
    for p in range(NP):
        qm = _stack_heads(q_ref[p])
        ab, c = _sb_weights(_dot_nt(qm, kv_ref[p]), upper_n, jnp.zeros((2 * ts, 1), F32), mask)
        qms.append(qm)
        carries.append(jnp.broadcast_to(c, (2 * ts, TK)))
        accs.append(_pv_row_major(ab, kv_ref[NP + p]))

    def cond(st):
        return jnp.logical_and(st[0] >= 0, st[1] > 0)

    def body(st):
        j = st[0]
        wait(j)

        @pl.when(j > 0)
        def _():
            start(j - 1)

        slot = j % 2
        rows = [slice(p * LANES, (p + 1) * LANES) for p in range(NP)]
        zs = [_dot(qms[p], kbuf[slot, rows[p], :].astype(BF16)) for p in range(NP)]
        ws, cs = _sb_weights_multi(zs, upper_ones, list(st[2]), [None] * NP)
        as_ = [st[3][p] + _pv_feature_major(ws[p], vbuf[slot, rows[p], :].astype(BF16)) for p in range(NP)]
        return (j - 1, _any_live(cs), tuple(cs), tuple(as_))

    st = lax.while_loop(cond, body, (nblk - 1, _any_live(carries), tuple(carries), tuple(accs)))

    @pl.when(st[0] >= 0)
    def _():
        wait(st[0])

    for p in range(NP):
        o_ref[p] = st[3][p].astype(BF16)


def _sb_sample(q_s, kv_s, cache_kt, cache_vt, n_streams, ts):
    d = cache_kt.shape[1]
    return pl.pallas_call(
        _sb_sample_kernel,
        grid=(n_streams,),
        in_specs=[
            pl.BlockSpec((NP, ts, LANES), lambda s: (0, s, 0)),
            pl.BlockSpec((2 * NP, ts, LANES), lambda s: (0, s, 0)),
            pl.BlockSpec(memory_space=pl.ANY),
            pl.BlockSpec(memory_space=pl.ANY),
        ],
        out_specs=pl.BlockSpec((NP, ts, LANES), lambda s: (0, s, 0)),
        out_shape=jax.ShapeDtypeStruct(q_s.shape, BF16),
        scratch_shapes=[pltpu.VMEM((2, d, TK), F32), pltpu.VMEM((2, d, TK), F32),
                        pltpu.SemaphoreType.DMA((2, 2))],
        compiler_params=_cparams(("arbitrary",)),
        name="stick_breaking_sample",
    )(q_s, kv_s, cache_kt, cache_vt)


def _pick(is_sample, prompt_ref, sample_ref):
    return jnp.where(is_sample, sample_ref[...], prompt_ref[...])


def _attn_out(x, o, wo_ref, g1_ref, n2_ref, sh_ref, sc_ref):
    rows = x.shape[0]
    attn = jnp.concatenate([o[p] for p in range(NP)], axis=1)
    x1 = x + _group_rows(g1_ref[...], rows) * _dot(attn, wo_ref[...])
    h2 = _mod_norm(x1, n2_ref[...], sh_ref[...], sc_ref[...])
    return x1, h2


def _post_dense_kernel(xp_ref, xs_ref, op_ref, os_ref, wo_ref, g1_ref, n2_ref, sh_ref, sc_ref, g2_ref,
                       wg_ref, wu_ref, wd_ref, *rest):
    n_cast = (len(rest) - 4) // 2
    cast_in, out_ref, cast_out = rest[:n_cast], rest[n_cast], rest[n_cast + 1:2 * n_cast + 1]
    x1_sc, h2_sc, acc_sc = rest[2 * n_cast + 1:]
    j = pl.program_id(1)

    for src, dst in zip(cast_in, cast_out):
        dst[...] = src[...].astype(BF16)

    @pl.when(j == 0)
    def _():
        is_sample = pl.program_id(0) == pl.num_programs(0) - 1
        x1, h2 = _attn_out(_pick(is_sample, xp_ref, xs_ref), _pick(is_sample, op_ref, os_ref),
                           wo_ref, g1_ref, n2_ref, sh_ref, sc_ref)
        x1_sc[...] = x1
        h2_sc[...] = h2.astype(BF16)
        acc_sc[...] = jnp.zeros_like(acc_sc)

    h2 = h2_sc[...]
    act = (_silu(_dot(h2, wg_ref[...])) * _dot(h2, wu_ref[...])).astype(BF16)
    acc_sc[...] += _dot(act, wd_ref[...])

    @pl.when(j == pl.num_programs(1) - 1)
    def _():
        out_ref[...] = x1_sc[...] + _group_rows(g2_ref[...], x1_sc.shape[0]) * acc_sc[...]


def _post_dense(x_p, x_s, o_p, o_s, wo, mod_g, n2, wg, wu, wd, tpb, casts=()):
    tp, d = x_p.shape
    ntp = tp // TM
    mrow = lambda i: _mod_block(i, ntp, tpb)
    f = wg.shape[1]
    nj = f // FC
    gpt = TM // GROUP
    row = lambda i, j: (i, 0)
    prow = lambda i, j: (jnp.minimum(i, ntp - 1), 0)
    const = lambda i, j: (0, 0)
    g1_g = sh_g = sc_g = g2_g = mod_g
    assert not casts or (ntp + 1) * nj >= CAST_BLOCKS
    cast_blk = lambda i, j: (jnp.minimum(i * nj + j, CAST_BLOCKS - 1), 0)
    cast_specs = [pl.BlockSpec((c.shape[0] // CAST_BLOCKS, c.shape[1]), cast_blk) for c in casts]
    return pl.pallas_call(
        _post_dense_kernel,
        grid=(ntp + 1, nj),
        in_specs=[
            pl.BlockSpec((TM, d), prow),
            pl.BlockSpec((TM, d), const),
            pl.BlockSpec((NP, TM, LANES), lambda i, j: (0, jnp.minimum(i, ntp - 1), 0)),
            pl.BlockSpec((NP, TM, LANES), lambda i, j: (0, 0, 0)),
            pl.BlockSpec((d, d), const),
            pl.BlockSpec((gpt, d), lambda i, j: (mrow(i), MOD_GATE1)),
            pl.BlockSpec((1, d), const),
            pl.BlockSpec((gpt, d), lambda i, j: (mrow(i), MOD_SHIFT2)),
            pl.BlockSpec((gpt, d), lambda i, j: (mrow(i), MOD_SCALE2)),
            pl.BlockSpec((gpt, d), lambda i, j: (mrow(i), MOD_GATE2)),
            pl.BlockSpec((d, FC), lambda i, j: (0, j)),
            pl.BlockSpec((d, FC), lambda i, j: (0, j)),
            pl.BlockSpec((FC, d), lambda i, j: (j, 0)),
        ] + cast_specs,
        out_specs=[pl.BlockSpec((TM, d), row)] + cast_specs,
        out_shape=[jax.ShapeDtypeStruct((tp + TM, d), F32)]
        + [jax.ShapeDtypeStruct(c.shape, BF16) for c in casts],
        scratch_shapes=[pltpu.VMEM((TM, d), F32), pltpu.VMEM((TM, d), BF16), pltpu.VMEM((TM, d), F32)],
        compiler_params=_cparams(("arbitrary", "arbitrary")),
        name="attn_out_dense_ffn",
    )(x_p, x_s, o_p, o_s, wo, g1_g, n2, sh_g, sc_g, g2_g, wg, wu, wd, *casts)


def _post_router_kernel(x_ref, op_ref, os_ref, wo_ref, g1_ref, n2_ref, sh_ref, sc_ref, wr_ref,
                        x1_ref, h2_ref, route_ref, cnt_ref, cnt_sc):
    i = pl.program_id(0)
    rows = x_ref.shape[0]

    @pl.when(i == 0)
    def _():
        cnt_sc[...] = jnp.zeros_like(cnt_sc)

    is_sample = i == pl.num_programs(0) - 1
    x1, h2 = _attn_out(x_ref[...], _pick(is_sample, op_ref, os_ref), wo_ref, g1_ref, n2_ref, sh_ref, sc_ref)
    x1_ref[...] = x1
    h2_ref[...] = h2

    hh, hl = _split_hi_lo(h2)
    wh, wl = _split_hi_lo(wr_ref[...])
    logits = _dot(hh, wh) + _dot(hl, wh) + _dot(hh, wl)
    lane = lax.broadcasted_iota(jnp.int32, logits.shape, 1).astype(F32)
    logits = jnp.where(lane < N_EXPERTS, logits, -jnp.inf)
    v1 = logits.max(axis=1, keepdims=True)
    i1 = jnp.where(logits == v1, lane, float(LANES)).min(axis=1, keepdims=True)
    rest = jnp.where(lane == i1, -jnp.inf, logits)
    v2 = rest.max(axis=1, keepdims=True)
    i2 = jnp.where(rest == v2, lane, float(LANES)).min(axis=1, keepdims=True)
    w1 = 1.0 / (1.0 + jnp.exp(v2 - v1))
    w2 = 1.0 - w1

    oh1 = (lane == i1).astype(F32)
    oh2 = (lane == i2).astype(F32)
    cnt = oh1 + oh2
    lower = (lax.broadcasted_iota(jnp.int32, (rows, rows), 0)
             > lax.broadcasted_iota(jnp.int32, (rows, rows), 1)).astype(BF16)
    rank = _dot(lower, cnt.astype(BF16)) + cnt_sc[...]
    r1 = (oh1 * rank).sum(axis=1, keepdims=True)
    r2 = (oh2 * rank).sum(axis=1, keepdims=True)
    cnt_sc[...] += cnt.sum(axis=0, keepdims=True)
    cnt_ref[...] = cnt_sc[...]

    out = jnp.where(lane == 0, i1, 0.0)
    out = jnp.where(lane == 1, i2, out)
    out = jnp.where(lane == 2, w1, out)
    out = jnp.where(lane == 3, w2, out)
    out = jnp.where(lane == 4, r1, out)
    out = jnp.where(lane == 5, r2, out)
    route_ref[...] = out[:, :route_ref.shape[1]]


def _post_router(x_all, o_p, o_s, wo, mod_g, n2, wr_pad, tpb):
    t, d = x_all.shape
    ntp = t // TM - 1
    mrow = lambda i: _mod_block(i, ntp, tpb)
    gpt = TM // GROUP
    row = lambda i: (i, 0)
    const = lambda i: (0, 0)
    g1_g = sh_g = sc_g = mod_g
    return pl.pallas_call(
        _post_router_kernel,
        grid=(t // TM,),
        in_specs=[
            pl.BlockSpec((TM, d), row),
            pl.BlockSpec((NP, TM, LANES), lambda i: (0, jnp.minimum(i, ntp - 1), 0)),
            pl.BlockSpec((NP, TM, LANES), lambda i: (0, 0, 0)),
            pl.BlockSpec((d, d), const),
            pl.BlockSpec((gpt, d), lambda i: (mrow(i), MOD_GATE1)),
            pl.BlockSpec((1, d), const),
            pl.BlockSpec((gpt, d), lambda i: (mrow(i), MOD_SHIFT2)),
            pl.BlockSpec((gpt, d), lambda i: (mrow(i), MOD_SCALE2)),
            pl.BlockSpec((d, LANES), const),
        ],
        out_specs=[
            pl.BlockSpec((TM, d), row),
            pl.BlockSpec((TM, d), row),
            pl.BlockSpec((TM, 8), row),
            pl.BlockSpec((1, LANES), const),
        ],
        out_shape=[
            jax.ShapeDtypeStruct((t, d), F32),
            jax.ShapeDtypeStruct((t, d), F32),
            jax.ShapeDtypeStruct((t, 8), F32),
            jax.ShapeDtypeStruct((1, LANES), F32),
        ],
        scratch_shapes=[pltpu.VMEM((1, LANES), F32)],
        compiler_params=_cparams(("arbitrary",)),
        name="attn_out_router",
    )(x_all, o_p, o_s, wo, g1_g, n2, sh_g, sc_g, wr_pad)


def _dispatch_kernel(fill_start_ref, fill_on_ref, pos_ref, h2_ref, xs_ref, zero_buf, sem):
    rows = h2_ref.shape[0]

    @pl.when(pl.program_id(0) == 0)
    def _():
        zero_buf[...] = jnp.zeros_like(zero_buf)

        def fill(k):
            return pltpu.make_async_copy(
                zero_buf, xs_ref.at[pl.ds(pl.multiple_of(fill_start_ref[k], TR), TR)], sem)

        for k in range(fill_on_ref.shape[0]):
            @pl.when(fill_on_ref[k] > 0)
            def _():
                fill(k).start()
        for k in range(fill_on_ref.shape[0]):
            @pl.when(fill_on_ref[k] > 0)
            def _():
                fill(k).wait()

    def issue(r, c):
        src = h2_ref.at[pl.ds(r, 1)]
        pltpu.make_async_copy(src, xs_ref.at[pl.ds(pos_ref[0, 0, r], 1)], sem).start(priority=0)
        pltpu.make_async_copy(src, xs_ref.at[pl.ds(pos_ref[0, 0, rows + r], 1)], sem).start(priority=1)
        return c

    lax.fori_loop(0, rows, issue, 0, unroll=8)
    for _ in range(2):
        pltpu.make_async_copy(h2_ref, xs_ref.at[pl.ds(0, rows)], sem).wait()


def _dispatch(fill_start, fill_on, pos_tiles, h2, r_pad):
    t, d = h2.shape
    grid_spec = pltpu.PrefetchScalarGridSpec(
        num_scalar_prefetch=2,
        grid=(t // TM,),
        in_specs=[
            pl.BlockSpec((1, 1, 2 * TM), lambda i, fs, fo: (i, 0, 0), memory_space=pltpu.SMEM),
            pl.BlockSpec((TM, d), lambda i, fs, fo: (i, 0)),
        ],
        out_specs=pl.BlockSpec(memory_space=pl.ANY),
        scratch_shapes=[pltpu.VMEM((TR, d), F32), pltpu.SemaphoreType.DMA],
    )
    return pl.pallas_call(
        _dispatch_kernel,
        grid_spec=grid_spec,
        out_shape=jax.ShapeDtypeStruct((r_pad, d), F32),
        compiler_params=_cparams(("arbitrary",)),
        name="moe_dispatch",
    )(fill_start, fill_on, pos_tiles, h2)


def _expert_ffn_kernel(te_ref, tv_ref, xs_ref, wg_ref, wu_ref, wd_ref, y_ref, xb_sc, acc_sc):
    r = pl.program_id(0)
    j = pl.program_id(1)
    valid = tv_ref[r] > 0

    @pl.when(j == 0)
    def _():
        xb_sc[...] = xs_ref[...].astype(BF16)
        acc_sc[...] = jnp.zeros_like(acc_sc)

    @pl.when(valid)
    def _():
        x = xb_sc[...]
        act = (_silu(_dot(x, wg_ref[0])) * _dot(x, wu_ref[0])).astype(BF16)
        acc_sc[...] += _dot(act, wd_ref[0])

    @pl.when(j == pl.num_programs(1) - 1)
    def _():
        y_ref[...] = acc_sc[...]


def _expert_ffn(tile_expert, tile_valid, xs, wg, wu, wd):
    r_pad, d = xs.shape
    f = wg.shape[2]
    nj = f // FC
    wcol = lambda r, j, tv: jnp.where(tv[r] > 0, j, nj - 1)
    grid_spec = pltpu.PrefetchScalarGridSpec(
        num_scalar_prefetch=2,
        grid=(r_pad // TR, nj),
        in_specs=[
            pl.BlockSpec((TR, d), lambda r, j, te, tv: (r, 0)),
            pl.BlockSpec((1, d, FC), lambda r, j, te, tv: (te[r], 0, wcol(r, j, tv))),
            pl.BlockSpec((1, d, FC), lambda r, j, te, tv: (te[r], 0, wcol(r, j, tv))),
            pl.BlockSpec((1, FC, d), lambda r, j, te, tv: (te[r], wcol(r, j, tv), 0)),
        ],
        out_specs=pl.BlockSpec((TR, d), lambda r, j, te, tv: (r, 0)),
        scratch_shapes=[pltpu.VMEM((TR, d), BF16), pltpu.VMEM((TR, d), F32)],
    )
    return pl.pallas_call(
        _expert_ffn_kernel,
        grid_spec=grid_spec,
        out_shape=jax.ShapeDtypeStruct((r_pad, d), F32),
        compiler_params=_cparams(("arbitrary", "arbitrary")),
        name="moe_expert_ffn",
    )(tile_expert, tile_valid, xs, wg, wu, wd)


def _combine_kernel(pos_ref, posn_ref, x1_ref, g2_ref, route_ref, y_ref, outp_ref, outs_ref, buf, sem, *,
                    ntp):
    i = pl.program_id(0)
    rows = x1_ref.shape[0]
    slot = i % 2

    def gather(p_ref, s):
        def issue(r, c):
            pltpu.make_async_copy(y_ref.at[pl.ds(p_ref[0, 0, r], 1)], buf.at[s, 0, pl.ds(r, 1)],
                                  sem.at[s]).start(priority=0)
            pltpu.make_async_copy(y_ref.at[pl.ds(p_ref[0, 0, rows + r], 1)], buf.at[s, 1, pl.ds(r, 1)],
                                  sem.at[s]).start(priority=1)
            return c
        lax.fori_loop(0, rows, issue, 0, unroll=8)

    @pl.when(i == 0)
    def _():
        gather(pos_ref, 0)

    @pl.when(i + 1 < pl.num_programs(0))
    def _():
        gather(posn_ref, 1 - slot)

    for k in range(2):
        pltpu.make_async_copy(y_ref.at[pl.ds(0, rows)], buf.at[slot, k], sem.at[slot]).wait()
    route = route_ref[...]
    moe = route[:, 2:3] * buf[slot, 0] + route[:, 3:4] * buf[slot, 1]
    res = x1_ref[...] + _group_rows(g2_ref[...], rows) * moe

    @pl.when(i < ntp)
    def _():
        outp_ref[...] = res

    @pl.when(i >= ntp)
    def _():
        outs_ref[...] = res


def _combine(pos_tiles, x1, mod_g, route, y, tp, tpb):
    t, d = x1.shape
    ntp = tp // TMC
    gpt = TMC // GROUP
    assert TMC == TM
    row = lambda i: (i, 0)
    last = t // TMC - 1
    g2_g = mod_g
    return pl.pallas_call(
        functools.partial(_combine_kernel, ntp=ntp),
        grid=(t // TMC,),
        in_specs=[
            pl.BlockSpec((1, 1, 2 * TMC), lambda i: (i, 0, 0), memory_space=pltpu.SMEM),
            pl.BlockSpec((1, 1, 2 * TMC), lambda i: (jnp.minimum(i + 1, last), 0, 0), memory_space=pltpu.SMEM),
            pl.BlockSpec((TMC, d), row),
            pl.BlockSpec((gpt, d), lambda i: (_mod_block(i, ntp, tpb), MOD_GATE2)),
            pl.BlockSpec((TMC, 8), row),
            pl.BlockSpec(memory_space=pl.ANY),
        ],
        out_specs=[
            pl.BlockSpec((TMC, d), lambda i: (jnp.minimum(i, ntp - 1), 0)),
            pl.BlockSpec((TMC, d), lambda i: (jnp.maximum(i - ntp, 0), 0)),
        ],
        out_shape=[jax.ShapeDtypeStruct((tp, d), F32), jax.ShapeDtypeStruct((t - tp, d), F32)],
        scratch_shapes=[pltpu.VMEM((2, 2, TMC, d), F32), pltpu.SemaphoreType.DMA((2,))],
        compiler_params=_cparams(("arbitrary",)),
        name="moe_combine",
    )(pos_tiles, pos_tiles, x1, g2_g, route, y)


def _band_valid(q_pos, k_pos):
    qc = q_pos // CHUNK
    kc = k_pos // CHUNK
    return (k_pos[None, :] >= 0) & (kc[None, :] <= qc[:, None]) & (kc[None, :] >= qc[:, None] - N_PAST_CHUNKS)


def _band_bias(rel_bias, q_pos, k_pos):
    nq, nk = len(q_pos), len(k_pos)
    assert np.all(np.diff(q_pos) == 1) and np.all(np.diff(k_pos) == 1)
    period = nq + nk
    dist = (q_pos[0] - k_pos[0]) + (nq - 1) - np.arange(period)
    vec = rel_bias.astype(F32)[:, np.clip(dist, -REL_CLIP, REL_CLIP) + REL_CLIP]
    flat = jnp.tile(jnp.roll(vec, -(nq - 1), axis=1), (1, nq))[:, :nq * (period - 1)]
    tab = flat.reshape(-1, nq, period - 1)[:, :, :nk]
    return jnp.where(jnp.asarray(_band_valid(q_pos, k_pos))[None], tab, NEG_INF)


def _tile_pos(pos2, tile):
    t = pos2.shape[1]
    return pos2.reshape(2, t // tile, tile).transpose(1, 0, 2).reshape(t // tile, 1, 2 * tile)


def _feature_major(cache):
    n, length = cache.shape[0], cache.shape[1]
    return cache.transpose(0, 2, 3, 1).reshape(n, N_HEADS * HEAD_DIM, length)


def _token_major(xt):
    n, _, length = xt.shape
    return xt.reshape(n, N_HEADS, HEAD_DIM, length).transpose(0, 3, 1, 2)


def kernel(x_prompt, x_sample, cache_a_k, cache_a_v, cache_b_k, cache_b_v, c_prompt, c_sample, w_qkv, w_o, norm1_g, norm2_g, w_ada, b_ada, q_norm_g, k_norm_g, rel_bias, w_gate_d, w_up_d, w_down_d, w_router, w_gate_e, w_up_e, w_down_e):
    bp, sp, d = x_prompt.shape
    bs, ts, _ = x_sample.shape
    past_len = cache_b_k.shape[2]
    win = cache_a_k.shape[2]
    depth = w_qkv.shape[0]
    assert depth == 2 and d == N_HEADS * HEAD_DIM and ts == GROUP
    tp, tsm = bp * sp, bs * ts
    t_all = tp + tsm
    assert sp % TM == 0 and tsm == TM and sp % (BAND_UNROLL * TQ) == 0 and t_all % TMC == 0 and tp % TMC == 0
    assert win == BAND_PAST == TM and sp >= BAND_KEYS and past_len % TK == 0 and past_len >= win

    x_p = x_prompt.reshape(tp, d)
    x_s = x_sample.reshape(tsm, d)

    mod = _ada(jnp.concatenate([c_prompt, c_sample], axis=0), w_ada, b_ada)
    mod = jnp.concatenate([mod[:, bp:], jnp.repeat(mod[:, :bp], TM // GROUP, axis=1)], axis=1)
    tpb = sp // TM

    bd = (np.arange(256)[:, None] // HEAD_DIM == np.arange(256)[None, :] // HEAD_DIM)
    bd = jnp.asarray(bd.astype(np.float32) / HEAD_DIM, BF16)
    ones_row = jnp.ones((1, d), F32)
    ones_col = jnp.ones((d, 1), F32)

    for i in range(depth):
        mod_g = mod[i]
        band = i % 2 == 0
        wq = w_qkv[i][:, :d].astype(BF16)
        wkvt = w_qkv[i][:, d:].T.astype(BF16)
        qg = jnp.tile(q_norm_g[0], N_HEADS)[None] if band else ones_row
        kg = jnp.tile(k_norm_g[0], N_HEADS)[:, None] if band else ones_col
        n1 = norm1_g[i][None]
        if band:
            q_p, ktb, vtb, ktf, vtf = _qkv_prompt(x_p, n1, mod_g, wq, wkvt, qg, kg, bd, bp, sp, True)
            q_s, kv_s, ksf, vsf = _qkv_sample(x_s, 0, n1, mod_g, wq, wkvt, qg, kg, bd, True)
            q_pos = np.arange(TQ) + BAND_PAST
            bias_p = _band_bias(rel_bias[0], q_pos, np.arange(BAND_KEYS))
            bias_p = jnp.pad(bias_p, ((0, 0), (0, 0), (0, BAND_PAST)), constant_values=NEG_INF)
            o_p = _band_prompt(q_p, ktb, vtb, bias_p, bp, sp)
            pos_s = past_len + np.arange(ts)
            pos_win = past_len - win + np.arange(win)
            ca_k, ca_v = _feature_major(cache_a_k[0]), _feature_major(cache_a_v[0])
            right = ((0, 0), (0, 0), (LANES - ts, 0))
            new_k = jnp.pad(ksf.reshape(d, bs, ts).transpose(1, 0, 2), right)
            new_v = jnp.pad(vsf.reshape(d, bs, ts).transpose(1, 0, 2), right)
            o_s, k_win, v_win = _band_sample(q_s, kv_s, ca_k, ca_v, _band_bias(rel_bias[0], pos_s, pos_win),
                                             _band_bias(rel_bias[0], pos_s, pos_s), new_k, new_v, bs, ts)
            a_k_p = _token_major(ktf)[None]
            a_v_p = _token_major(vtf)[None]
            a_k_s = _token_major(k_win)[None]
            a_v_s = _token_major(v_win)[None]
            f = w_gate_d.shape[2]
            experts = (w_gate_e[0].reshape(N_EXPERTS * d, f), w_up_e[0].reshape(N_EXPERTS * d, f),
                       w_down_e[0].reshape(N_EXPERTS * f, d))
            side = (tp // TM + 1) * (f // FC) >= CAST_BLOCKS and all(
                e.shape[0] % (16 * CAST_BLOCKS) == 0 for e in experts)
            x_all, *experts_bf = _post_dense(
                x_p, x_s, o_p, o_s, w_o[i].astype(BF16), mod_g, norm2_g[i][None], w_gate_d[0].astype(BF16),
                w_up_d[0].astype(BF16), w_down_d[0].astype(BF16), tpb, experts if side else ())
            if not side:
                experts_bf = [e.astype(BF16) for e in experts]
            wge, wue = (e.reshape(N_EXPERTS, d, f) for e in experts_bf[:2])
            wde = experts_bf[2].reshape(N_EXPERTS, f, d)
        else:
            q_p, ktb, vtb, ktf, vtf = _qkv_prompt(x_all, n1, mod_g, wq, wkvt, qg, kg, bd, bp, sp, False)
            q_s, kv_s, ksf, vsf = _qkv_sample(x_all, tp // TM, n1, mod_g, wq, wkvt, qg, kg, bd, False)
            o_p = _sb_prompt(q_p, ktb, vtb, bp, sp)
            o_s = _sb_sample(q_s, kv_s, _feature_major(cache_b_k[0]), _feature_major(cache_b_v[0]), bs, ts)
            b_k_p = _token_major(ktf)[None]
            b_v_p = _token_major(vtf)[None]
            b_k_s = ksf.reshape(1, bs, ts, N_HEADS, HEAD_DIM)
            b_v_s = vsf.reshape(1, bs, ts, N_HEADS, HEAD_DIM)
            wr_pad = jnp.pad(w_router[0], ((0, 0), (0, LANES - N_EXPERTS)))
            x1, h2, route, cnt = _post_router(x_all, o_p, o_s, w_o[i].astype(BF16), mod_g, norm2_g[i][None],
                                              wr_pad, tpb)
            counts = cnt[0, :N_EXPERTS].astype(jnp.int32)
            sizes = (counts + TR - 1) // TR * TR
            ends = jnp.cumsum(sizes)
            offs = ends - sizes
            route_t = route.T
            e12 = route_t[0:2].astype(jnp.int32)
            off12 = sum(jnp.where(e12 == e, offs[e], 0) for e in range(N_EXPERTS))
            pos2 = off12 + route_t[4:6].astype(jnp.int32)
            r_pad = (2 * t_all // TR + N_EXPERTS) * TR
            tile_start = jnp.arange(r_pad // TR, dtype=jnp.int32) * TR
            tile_expert = jnp.minimum(jnp.sum(tile_start[:, None] >= ends[None, :], axis=1),
                                      N_EXPERTS - 1).astype(jnp.int32)
            tile_valid = (tile_start < ends[-1]).astype(jnp.int32)
            tail_start = ends[-1] + jnp.arange(N_EXPERTS, dtype=jnp.int32) * TR
            fill_start = jnp.concatenate([ends - TR, tail_start]).astype(jnp.int32)
            fill_on = jnp.concatenate([sizes > 0, tail_start < r_pad]).astype(jnp.int32)
            xs = _dispatch(jnp.where(fill_on > 0, fill_start, 0), fill_on, _tile_pos(pos2, TM), h2, r_pad)
            y = _expert_ffn(tile_expert, tile_valid, xs, wge, wue, wde)
            y_p, y_s = _combine(_tile_pos(pos2, TMC), x1, mod_g, route, y, tp, tpb)

    return (y_p.reshape(bp, sp, d), y_s.reshape(bs, ts, d), a_k_p, a_v_p, a_k_s, a_v_s,
            b_k_p, b_v_p, b_k_s, b_v_s)
```
